```python
import math
import jax, jax.numpy as jnp
from jax import lax
import numpy as np

D_MODEL = 4096
BATCH = 4
SEQ = 2048
DEPTH = 2
DEC_BATCH = 8
DEC_SEQ = 4
PAST_LEN = 16384
PAGE_SIZE = 128

HEAD_DIM = 128
ROT_DIM = HEAD_DIM // 4
ROPE_THETA = 500000.0
NSA_HEADS = 16
NSA_KV = 4
NSA_REP = NSA_HEADS // NSA_KV
CMP_BLOCK = 32
CMP_STRIDE = 16
SLC_BLOCK = 64
TOP_BLOCKS = 16
WINDOW = 512
FORCE_SCORE = 1.0e9
Q_BLOCK = 128
W_A = NSA_HEADS * HEAD_DIM
CHUNK = 128
GMLP_GROUPS = 16
GMLP_GDIM = 128
W_B = GMLP_GROUPS * GMLP_GDIM
MLSTM_HEADS = 8
MLSTM_DQK = 128
MLSTM_DV = 256
W_C = MLSTM_HEADS * MLSTM_DV
MLSTM_CHUNK = 64
PEER_HEADS = 8
PEER_DQ = 256
N_KEYS = 128
N_EXPERTS = N_KEYS * N_KEYS
PEER_TOPK = 16
PEER_BLOCK = 128
DN_ALPHA = (2 * DEPTH) ** 0.25
DN_BETA = (8 * DEPTH) ** -0.25
IN_SIZES = (W_A, 6 * NSA_KV * HEAD_DIM, 3 * NSA_HEADS,
            2 * W_B,
            MLSTM_HEADS * MLSTM_DQK, MLSTM_HEADS * MLSTM_DQK, W_C, MLSTM_HEADS, MLSTM_HEADS, W_C,
            3 * D_MODEL)
N_IN = sum(IN_SIZES)

kernel_name = "hybrid_nsa_gmlp_mlstm_peer_step"


def layer_norm(x, g, b, eps=1e-5):
    xf = x.astype(jnp.float32)
    mu = jnp.mean(xf, -1, keepdims=True)
    var = jnp.mean(jnp.square(xf - mu), -1, keepdims=True)
    return ((xf - mu) * lax.rsqrt(var + eps) * g + b).astype(x.dtype)


def rope(x, pos):
    inv = ROPE_THETA ** (-jnp.arange(0, ROT_DIM, 2, dtype=jnp.float32) / ROT_DIM)
    ang = pos.astype(jnp.float32)[:, None] * inv[None, :]
    cos, sin = jnp.cos(ang)[:, None, :], jnp.sin(ang)[:, None, :]
    xr = x[..., :ROT_DIM].astype(jnp.float32)
    x1, x2 = xr[..., :ROT_DIM // 2], xr[..., ROT_DIM // 2:]
    rot = jnp.concatenate([x1 * cos - x2 * sin, x2 * cos + x1 * sin], -1).astype(x.dtype)
    return jnp.concatenate([rot, x[..., ROT_DIM:]], -1)


def masked_softmax(logits, mask):
    lg = jnp.where(mask, logits.astype(jnp.float32), -jnp.inf)
    mx = jnp.max(lg, -1, keepdims=True)
    mx = jnp.where(jnp.isfinite(mx), mx, 0.0)
    e = jnp.where(mask, jnp.exp(lg - mx), 0.0)
    return e / jnp.maximum(jnp.sum(e, -1, keepdims=True), 1e-30)


def compress(rows, pe, w1, b1, w2):
    b_, L = rows.shape[:2]
    n_cmp = (L - CMP_BLOCK) // CMP_STRIDE + 1
    n_half = n_cmp + 1
    halves = rows[:, :n_half * CMP_STRIDE].reshape(b_, n_half, CMP_STRIDE, NSA_KV, HEAD_DIM)
    pa = jnp.einsum('bnjgd,jde->bnge', halves + pe[:CMP_STRIDE][:, None, :], w1[:CMP_STRIDE])
    pb = jnp.einsum('bnjgd,jde->bnge', halves + pe[CMP_STRIDE:][:, None, :], w1[CMP_STRIDE:])
    h = jax.nn.gelu(pa[:, :n_cmp] + pb[:, 1:] + b1)
    return jnp.einsum('bnge,ef->bngf', h, w2)


def cmp_to_slc(n_cmp, n_slc):
    c = np.arange(n_cmp)[:, None]
    s = np.arange(n_slc)[None, :]
    start = c * CMP_STRIDE
    ov = (start < (s + 1) * SLC_BLOCK) & (start + CMP_BLOCK > s * SLC_BLOCK)
    return jnp.asarray(ov.astype(np.float32))


def nsa_attention(q, k_cmp, v_cmp, k_slc, v_slc, k_win, v_win, gates, q_start, cmp_params):
    b_, Tq = q.shape[:2]
    L = k_cmp.shape[1]
    P = k_win.shape[1] - Tq
    scale = HEAD_DIM ** -0.5
    q_pos = q_start + jnp.arange(Tq)
    qg = q.reshape(b_, Tq, NSA_KV, NSA_REP, HEAD_DIM)
    qrg = rope(q, q_pos).reshape(b_, Tq, NSA_KV, NSA_REP, HEAD_DIM)
    pe, w1, b1, w2 = cmp_params
    kc = compress(k_cmp, pe[0], w1[0], b1[0], w2[0])
    vc = compress(v_cmp, pe[1], w1[1], b1[1], w2[1])
    n_cmp = kc.shape[1]
    cmp_end = jnp.arange(n_cmp) * CMP_STRIDE + CMP_BLOCK - 1
    lg = jnp.einsum('btgrd,bcgd->bgrtc', qg, kc) * scale
    p_cmp = masked_softmax(lg, cmp_end[None, :] <= q_pos[:, None])
    o_cmp = jnp.einsum('bgrtc,bcgd->btgrd', p_cmp, vc)
    n_slc = -(-L // SLC_BLOCK)
    imp = jnp.einsum('bgrtc,cs->bgts', p_cmp, cmp_to_slc(n_cmp, n_slc))
    blk = jnp.arange(n_slc)
    forced = (blk[None, :] == 0) | (blk[None, :] == (q_pos // SLC_BLOCK)[:, None])
    future = blk[None, :] * SLC_BLOCK > q_pos[:, None]
    imp = jnp.where(forced, FORCE_SCORE, jnp.where(future, -jnp.inf, imp))
    k_top = min(TOP_BLOCKS, n_slc)
    _, sel = lax.top_k(imp, k_top)
    pad = n_slc * SLC_BLOCK - L
    ks_blk = jnp.pad(k_slc, ((0, 0), (0, pad), (0, 0), (0, 0))).reshape(
        b_, n_slc, SLC_BLOCK, NSA_KV, HEAD_DIM).transpose(0, 3, 1, 2, 4)
    vs_blk = jnp.pad(v_slc, ((0, 0), (0, pad), (0, 0), (0, 0))).reshape(
        b_, n_slc, SLC_BLOCK, NSA_KV, HEAD_DIM).transpose(0, 3, 1, 2, 4)
    win_pos = q_start - P + jnp.arange(P + Tq)
    qblk = math.gcd(Tq, Q_BLOCK)
    nqb = Tq // qblk
    bi = jnp.arange(b_)[:, None, None, None]
    gi = jnp.arange(NSA_KV)[None, :, None, None]

    def block_fn(i):
        t0 = i * qblk
        qpb = lax.dynamic_slice_in_dim(q_pos, t0, qblk)
        qr = lax.dynamic_slice_in_dim(qrg, t0, qblk, axis=1)
        idx = lax.dynamic_slice_in_dim(sel, t0, qblk, axis=2)
        kg = ks_blk[bi, gi, idx]
        vg = vs_blk[bi, gi, idx]
        ls = jnp.einsum('btgrd,bgtknd->bgrtkn', qr, kg) * scale
        kpos = idx[..., None] * SLC_BLOCK + jnp.arange(SLC_BLOCK)
        smask = (kpos <= qpb[None, None, :, None, None])[:, :, None]
        ps = masked_softmax(ls.reshape(b_, NSA_KV, NSA_REP, qblk, k_top * SLC_BLOCK),
                            smask.reshape(b_, NSA_KV, 1, qblk, k_top * SLC_BLOCK))
        o_s = jnp.einsum('bgrtm,bgtmd->btgrd', ps, vg.reshape(b_, NSA_KV, qblk, k_top * SLC_BLOCK, HEAD_DIM))
        kw = lax.dynamic_slice_in_dim(k_win, t0, qblk + P, axis=1)
        vw = lax.dynamic_slice_in_dim(v_win, t0, qblk + P, axis=1)
        wp = lax.dynamic_slice_in_dim(win_pos, t0, qblk + P)
        lw = jnp.einsum('btgrd,bsgd->bgrts', qr, kw) * scale
        dpos = qpb[:, None] - wp[None, :]
        wmask = (wp[None, :] >= 0) & (dpos >= 0) & (dpos <= WINDOW)
        o_w = jnp.einsum('bgrts,bsgd->btgrd', masked_softmax(lw, wmask), vw)
        return o_s, o_w

    o_s, o_w = lax.map(block_fn, jnp.arange(nqb))
    o_s = o_s.transpose(1, 0, 2, 3, 4, 5).reshape(b_, Tq, NSA_KV, NSA_REP, HEAD_DIM)
    o_w = o_w.transpose(1, 0, 2, 3, 4, 5).reshape(b_, Tq, NSA_KV, NSA_REP, HEAD_DIM)
    gt = jax.nn.sigmoid(gates.astype(jnp.float32)).reshape(b_, Tq, NSA_KV, NSA_REP, 3)
    o = gt[..., 0:1] * o_cmp + gt[..., 1:2] * o_s + gt[..., 2:3] * o_w
    return o.reshape(b_, Tq, W_A).astype(q.dtype)


def gmlp_mix(z, ln_g, ln_b, ws, bs):
    u, v = jnp.split(z, 2, axis=-1)
    v = layer_norm(v, ln_g, ln_b)
    b_, T = v.shape[:2]
    Tp = -(-T // CHUNK) * CHUNK
    vp = jnp.pad(v, ((0, 0), (0, Tp - T), (0, 0))).reshape(b_, Tp // CHUNK, CHUNK, GMLP_GROUPS, GMLP_GDIM)
    w = ws * jnp.tril(jnp.ones((CHUNK, CHUNK), ws.dtype))
    mixed = jnp.einsum('gts,bnsgc->bntgc', w, vp) + bs.T[None, None, :, :, None]
    mixed = mixed.reshape(b_, Tp, W_B)[:, :T]
    return u * mixed, v


def _chunks(a, nc, lc):
    b_, T, H = a.shape[:3]
    a = a.reshape(b_, nc, lc, H, *a.shape[3:])
    return a.transpose((1, 0, 3, 2) + tuple(range(4, a.ndim)))


def mlstm(q, k, v, ig, logf, C0, n0, m0):
    b_, T = q.shape[:2]
    lc = math.gcd(T, MLSTM_CHUNK)
    nc = T // lc
    f32 = jnp.float32
    xs = (_chunks(q.astype(f32), nc, lc), _chunks(k.astype(f32) * MLSTM_DQK ** -0.5, nc, lc),
          _chunks(v.astype(f32), nc, lc), _chunks(ig, nc, lc), _chunks(logf, nc, lc))
    tril = jnp.tril(jnp.ones((lc, lc), bool))

    def step(carry, inp):
        C, n, m = carry
        qc, kc, vc, ic, fc = inp
        F = jnp.cumsum(fc, -1)
        D = jnp.where(tril, F[..., :, None] - F[..., None, :] + ic[..., None, :], -jnp.inf)
        m_inter = F + m[..., None]
        m_t = jnp.maximum(m_inter, jnp.max(D, -1))
        S = jnp.einsum('bhtd,bhsd->bhts', qc, kc) * jnp.exp(D - m_t[..., None])
        dec = jnp.exp(m_inter - m_t)
        num = jnp.einsum('bhts,bhse->bhte', S, vc) + dec[..., None] * jnp.einsum('bhtd,bhed->bhte', qc, C)
        den = jnp.sum(S, -1) + dec * jnp.einsum('bhtd,bhd->bht', qc, n)
        h = num / jnp.maximum(jnp.abs(den), 1.0)[..., None]
        m_new = m_t[..., -1]
        wl = jnp.exp(F[..., -1:] - F + ic - m_new[..., None])
        dl = jnp.exp(F[..., -1] + m - m_new)
        C_new = dl[..., None, None] * C + jnp.einsum('bhs,bhse,bhsd->bhed', wl, vc, kc)
        n_new = dl[..., None] * n + jnp.einsum('bhs,bhsd->bhd', wl, kc)
        return (C_new, n_new, m_new), h

    (C, n, m), h = lax.scan(step, (C0.astype(f32), n0.astype(f32), m0.astype(f32)), xs)
    h = h.transpose(1, 0, 3, 2, 4).reshape(b_, T, MLSTM_HEADS, MLSTM_DV)
    return h, C, n, m


def peer(x, wq, k1, k2, u_tab, v_tab):
    b_, T, D = x.shape
    xt = x.reshape(b_ * T, D)
    n = xt.shape[0]
    q = (xt @ wq).reshape(n, PEER_HEADS, PEER_DQ)
    s1 = jnp.einsum('nhd,kd->nhk', q[..., :PEER_DQ // 2], k1).astype(jnp.float32)
    s2 = jnp.einsum('nhd,kd->nhk', q[..., PEER_DQ // 2:], k2).astype(jnp.float32)
    v1, i1 = lax.top_k(s1, PEER_TOPK)
    v2, i2 = lax.top_k(s2, PEER_TOPK)
    cand = (v1[..., :, None] + v2[..., None, :]).reshape(n, PEER_HEADS, PEER_TOPK * PEER_TOPK)
    sc, j = lax.top_k(cand, PEER_TOPK)
    e = (jnp.take_along_axis(i1, j // PEER_TOPK, -1) * N_KEYS
         + jnp.take_along_axis(i2, j % PEER_TOPK, -1)).reshape(n, PEER_HEADS * PEER_TOPK)
    g = jax.nn.softmax(sc, -1).reshape(n, PEER_HEADS * PEER_TOPK)
    blk = min(PEER_BLOCK, n)
    nb = -(-n // blk)
    pad = nb * blk - n
    xb = jnp.pad(xt, ((0, pad), (0, 0))).reshape(nb, blk, D)
    eb = jnp.pad(e, ((0, pad), (0, 0))).reshape(nb, blk, PEER_HEADS * PEER_TOPK)
    gb = jnp.pad(g, ((0, pad), (0, 0))).reshape(nb, blk, PEER_HEADS * PEER_TOPK)

    def block_fn(args):
        xx, ee, gg = args
        act = jax.nn.gelu(jnp.einsum('nd,ned->ne', xx, u_tab[ee]).astype(jnp.float32)) * gg
        return jnp.einsum('ne,ned->nd', act.astype(xx.dtype), v_tab[ee])

    y = lax.map(block_fn, (xb, eb, gb)).reshape(nb * blk, D)[:n]
    return y.reshape(b_, T, D)


def trunk_layer(x, q_start, past_kv, win_buf, mstate, win_keep, lw):
    (w_in, cmp_pe, cmp_w1, cmp_b1, cmp_w2, gmlp_ln_g, gmlp_ln_b, gmlp_ws, gmlp_bs,
     mlstm_b_i, mlstm_b_f, mlstm_norm_g, w_br_a, w_br_b, w_br_c, w_out, ln1_g, ln1_b,
     peer_wq, peer_k1, peer_k2, peer_u, peer_v, ln2_g, ln2_b) = lw
    b_, T, _ = x.shape
    z = x @ w_in
    splits = [int(s) for s in np.cumsum(IN_SIZES)[:-1]]
    a_q, a_kv, a_g, b_uv, c_q, c_k, c_v, c_i, c_f, c_o, mg = jnp.split(z, splits, axis=-1)
    pos = q_start + jnp.arange(T)
    kv = a_kv.reshape(b_, T, 6, NSA_KV, HEAD_DIM)
    k_slc = rope(kv[:, :, 2], pos)
    k_win = rope(kv[:, :, 4], pos)
    new_kv = jnp.stack([kv[:, :, 0], kv[:, :, 1], k_slc, kv[:, :, 3]], 2)
    full = new_kv if past_kv is None else jnp.concatenate([past_kv, new_kv], 1)
    if win_buf is None:
        win_buf = jnp.zeros((b_, WINDOW, 2, NSA_KV, HEAD_DIM), x.dtype)
    win_ext = jnp.concatenate([win_buf, jnp.stack([k_win, kv[:, :, 5]], 2)], 1)
    o_a = nsa_attention(a_q.reshape(b_, T, NSA_HEADS, HEAD_DIM), full[:, :, 0], full[:, :, 1],
                        full[:, :, 2], full[:, :, 3], win_ext[:, :, 0], win_ext[:, :, 1],
                        a_g.reshape(b_, T, NSA_HEADS, 3), q_start, (cmp_pe, cmp_w1, cmp_b1, cmp_w2))
    win_state = win_ext[:, win_ext.shape[1] - win_keep:]
    o_b, v_rows = gmlp_mix(jax.nn.gelu(b_uv), gmlp_ln_g, gmlp_ln_b, gmlp_ws, gmlp_bs)
    if mstate is None:
        mstate = (jnp.zeros((b_, MLSTM_HEADS, MLSTM_DV, MLSTM_DQK), jnp.float32),
                  jnp.zeros((b_, MLSTM_HEADS, MLSTM_DQK), jnp.float32),
                  jnp.zeros((b_, MLSTM_HEADS), jnp.float32))
    ig = c_i.astype(jnp.float32) + mlstm_b_i.astype(jnp.float32)
    logf = jax.nn.log_sigmoid(c_f.astype(jnp.float32) + mlstm_b_f.astype(jnp.float32))
    h, C, n, m = mlstm(c_q.reshape(b_, T, MLSTM_HEADS, MLSTM_DQK), c_k.reshape(b_, T, MLSTM_HEADS, MLSTM_DQK),
                       c_v.reshape(b_, T, MLSTM_HEADS, MLSTM_DV), ig.reshape(b_, T, MLSTM_HEADS),
                       logf.reshape(b_, T, MLSTM_HEADS), *mstate)
    mu = jnp.mean(h, -1, keepdims=True)
    var = jnp.mean(jnp.square(h - mu), -1, keepdims=True)
    hn = ((h - mu) * lax.rsqrt(var + 1e-5) * mlstm_norm_g).reshape(b_, T, W_C)
    o_c = (jax.nn.sigmoid(c_o.astype(jnp.float32)) * hn).astype(x.dtype)
    ga, gb, gc = jnp.split(jax.nn.sigmoid(mg), 3, axis=-1)
    merged = ga * (o_a @ w_br_a) + gb * (o_b @ w_br_b) + gc * (o_c @ w_br_c)
    x = layer_norm(DN_ALPHA * x + merged @ w_out, ln1_g, ln1_b)
    x = layer_norm(DN_ALPHA * x + peer(x, peer_wq, peer_k1, peer_k2, peer_u, peer_v), ln2_g, ln2_b)
    return x, new_kv, win_state, v_rows, C, n, m


def setup_inputs(seed: int = 0) -> dict:
    key = jax.random.key(seed)
    ks = iter(jax.random.split(key, 48))
    nrm = lambda shape, s: jax.random.normal(next(ks), shape, jnp.float32) * s
    n_pages = PAST_LEN // PAGE_SIZE
    n_pool = (DEC_BATCH * n_pages * 5 + 3) // 4
    wbuf = min(WINDOW, PAST_LEN)
    page_table = jax.random.permutation(next(ks), n_pool)[:DEC_BATCH * n_pages].reshape(
        DEC_BATCH, n_pages).astype(jnp.int32)
    f_bias = jnp.broadcast_to(jnp.linspace(3.0, 6.0, MLSTM_HEADS, dtype=jnp.float32), (DEPTH, MLSTM_HEADS))
    return {
        "x_prompt": nrm((BATCH, SEQ, D_MODEL), 1.0),
        "x_sample": nrm((DEC_BATCH, DEC_SEQ, D_MODEL), 1.0),
        "cache_nsa_kv": nrm((DEPTH, n_pool, PAGE_SIZE, 4, NSA_KV, HEAD_DIM), 1.0),
        "state_nsa_win": nrm((DEPTH, DEC_BATCH, wbuf, 2, NSA_KV, HEAD_DIM), 1.0),
        "state_mlstm_C": nrm((DEPTH, DEC_BATCH, MLSTM_HEADS, MLSTM_DV, MLSTM_DQK), 0.1),
        "state_mlstm_n": jnp.abs(nrm((DEPTH, DEC_BATCH, MLSTM_HEADS, MLSTM_DQK), 1.0)),
        "state_mlstm_m": nrm((DEPTH, DEC_BATCH, MLSTM_HEADS), 1.0),
        "page_table": page_table,
        "w_in": nrm((DEPTH, D_MODEL, N_IN), D_MODEL ** -0.5),
        "cmp_pe": nrm((DEPTH, 2, CMP_BLOCK, HEAD_DIM), 0.02),
        "cmp_w1": nrm((DEPTH, 2, CMP_BLOCK, HEAD_DIM, HEAD_DIM), (CMP_BLOCK * HEAD_DIM) ** -0.5),
        "cmp_b1": nrm((DEPTH, 2, HEAD_DIM), 0.01),
        "cmp_w2": nrm((DEPTH, 2, HEAD_DIM, HEAD_DIM), HEAD_DIM ** -0.5),
        "gmlp_ln_g": 1.0 + nrm((DEPTH, W_B), 0.01),
        "gmlp_ln_b": nrm((DEPTH, W_B), 0.01),
        "gmlp_ws": nrm((DEPTH, GMLP_GROUPS, CHUNK, CHUNK), CHUNK ** -0.5),
        "gmlp_bs": 1.0 + nrm((DEPTH, GMLP_GROUPS, CHUNK), 0.01),
        "mlstm_b_i": nrm((DEPTH, MLSTM_HEADS), 0.1),
        "mlstm_b_f": f_bias + nrm((DEPTH, MLSTM_HEADS), 0.01),
        "mlstm_norm_g": 1.0 + nrm((DEPTH, MLSTM_HEADS, MLSTM_DV), 0.01),
        "w_br_a": nrm((DEPTH, W_A, D_MODEL), DN_BETA * W_A ** -0.5),
        "w_br_b": nrm((DEPTH, W_B, D_MODEL), DN_BETA * W_B ** -0.5),
        "w_br_c": nrm((DEPTH, W_C, D_MODEL), DN_BETA * W_C ** -0.5),
        "w_out": nrm((DEPTH, D_MODEL, D_MODEL), DN_BETA * D_MODEL ** -0.5),
        "ln1_g": 1.0 + nrm((DEPTH, D_MODEL), 0.01),
        "ln1_b": nrm((DEPTH, D_MODEL), 0.01),
        "peer_wq": nrm((DEPTH, D_MODEL, PEER_HEADS * PEER_DQ), D_MODEL ** -0.5),
        "peer_k1": nrm((DEPTH, N_KEYS, PEER_DQ // 2), (PEER_DQ // 2) ** -0.5),
        "peer_k2": nrm((DEPTH, N_KEYS, PEER_DQ // 2), (PEER_DQ // 2) ** -0.5),
        "peer_u": nrm((DEPTH, N_EXPERTS, D_MODEL), D_MODEL ** -0.5),
        "peer_v": nrm((DEPTH, N_EXPERTS, D_MODEL), DN_BETA * PEER_HEADS ** -0.5),
        "ln2_g": 1.0 + nrm((DEPTH, D_MODEL), 0.01),
        "ln2_b": nrm((DEPTH, D_MODEL), 0.01),
    }


def reference(x_prompt, x_sample, cache_nsa_kv, state_nsa_win, state_mlstm_C, state_mlstm_n, state_mlstm_m,
              page_table, w_in, cmp_pe, cmp_w1, cmp_b1, cmp_w2, gmlp_ln_g, gmlp_ln_b, gmlp_ws, gmlp_bs,
              mlstm_b_i, mlstm_b_f, mlstm_norm_g, w_br_a, w_br_b, w_br_c, w_out, ln1_g, ln1_b,
              peer_wq, peer_k1, peer_k2, peer_u, peer_v, ln2_g, ln2_b):
    past_len = page_table.shape[1] * PAGE_SIZE
    yp, ys = x_prompt, x_sample
    kvp, kvs, winp, wins, vss, Cp, np_, mp, Cs, ns, ms = ([] for _ in range(11))
    for l in range(DEPTH):
        lw = (w_in[l], cmp_pe[l], cmp_w1[l], cmp_b1[l], cmp_w2[l], gmlp_ln_g[l], gmlp_ln_b[l], gmlp_ws[l],
              gmlp_bs[l], mlstm_b_i[l], mlstm_b_f[l], mlstm_norm_g[l], w_br_a[l], w_br_b[l], w_br_c[l],
              w_out[l], ln1_g[l], ln1_b[l], peer_wq[l], peer_k1[l], peer_k2[l], peer_u[l], peer_v[l],
              ln2_g[l], ln2_b[l])
        yp, a, b, _, c, d, e = trunk_layer(yp, 0, None, None, None, min(WINDOW, yp.shape[1]), lw)
        kvp.append(a); winp.append(b); Cp.append(c); np_.append(d); mp.append(e)
        past = cache_nsa_kv[l][page_table].reshape(ys.shape[0], past_len, 4, NSA_KV, HEAD_DIM)
        ys, a, b, v_rows, c, d, e = trunk_layer(
            ys, past_len, past, state_nsa_win[l], (state_mlstm_C[l], state_mlstm_n[l], state_mlstm_m[l]),
            state_nsa_win.shape[2], lw)
        kvs.append(a); wins.append(b); vss.append(v_rows); Cs.append(c); ns.append(d); ms.append(e)
    new_kv_prompt = jnp.stack(kvp)
    new_kv_sample = jnp.stack(kvs)
    new_win_prompt = jnp.stack(winp)
    new_win_sample = jnp.stack(wins)
    new_gmlp_v_sample = jnp.stack(vss)
    new_C_prompt, new_n_prompt, new_m_prompt = jnp.stack(Cp), jnp.stack(np_), jnp.stack(mp)
    new_C_sample, new_n_sample, new_m_sample = jnp.stack(Cs), jnp.stack(ns), jnp.stack(ms)
    return (yp, ys, new_kv_prompt, new_kv_sample, new_win_prompt, new_win_sample, new_gmlp_v_sample,
            new_C_prompt, new_n_prompt, new_m_prompt, new_C_sample, new_n_sample, new_m_sample)
```

```python
import functools
import math

import numpy as np
import jax
import jax.numpy as jnp
from jax import lax
from jax.experimental import pallas as pl
from jax.experimental.pallas import tpu as pltpu

F32 = jnp.float32
BF16 = jnp.bfloat16
I32 = jnp.int32

HEAD_DIM = 128
ROT_DIM = HEAD_DIM // 4
ROPE_THETA = 500000.0
NSA_HEADS = 16
NSA_KV = 4
NSA_REP = NSA_HEADS // NSA_KV
CMP_BLOCK = 32
CMP_STRIDE = 16
SLC_BLOCK = 64
SLC_SHIFT = 6
TOP_BLOCKS = 16
WINDOW = 512
FORCE_SCORE = 1.0e9
CHUNK = 128
GMLP_GROUPS = 16
GMLP_GDIM = 128
W_B = GMLP_GROUPS * GMLP_GDIM
MLSTM_HEADS = 8
MLSTM_DQK = 128
MLSTM_DV = 256
MLSTM_CHUNK = 64
PEER_HEADS = 8
PEER_DQ = 256
N_KEYS = 128
PEER_TOPK = 16
PAGE_SIZE = 128
LOG2E = 1.4426950408889634
NEG_INF = float("-inf")

VMEM_LIMIT_BYTES = 56 * 1024 * 1024


def _cp(*sem):
    return pltpu.CompilerParams(dimension_semantics=sem, vmem_limit_bytes=VMEM_LIMIT_BYTES)


def _gelu(x):
    return 0.5 * x * (1.0 + jnp.tanh(0.7978845608028654 * (x + 0.044715 * (x * x * x))))


def _sigmoid(x):
    return 1.0 / (1.0 + jnp.exp(-x))


def _log_sigmoid(x):
    return jnp.minimum(x, 0.0) - jnp.log(1.0 + jnp.exp(-jnp.abs(x)))


def _dot(a, b):
    return jnp.dot(a, b, preferred_element_type=F32)


def _dot_nt(a, b):
    return lax.dot_general(a, b, (((1,), (1,)), ((), ())), preferred_element_type=F32)


def _dot_tn(a, b):
    return lax.dot_general(a, b, (((0,), (0,)), ((), ())), preferred_element_type=F32)


def _masked_softmax(lg, mask):
    lg = jnp.where(mask, lg, NEG_INF)
    mx = jnp.max(lg, -1, keepdims=True)
    mx = jnp.where(mx > NEG_INF, mx, 0.0)
    e = jnp.exp(lg - mx)
    return e / jnp.maximum(jnp.sum(e, -1, keepdims=True), 1e-30)


def _split3_dot(p, m_bf16):
    hi = p.astype(BF16)
    r1 = p - hi.astype(F32)
    mid = r1.astype(BF16)
    lo = (r1 - mid.astype(F32)).astype(BF16)
    return _dot(hi, m_bf16) + _dot(mid, m_bf16) + _dot(lo, m_bf16)


def _rope(x, cos, sin):
    lane = lax.broadcasted_iota(I32, x.shape, 1)
    up = pltpu.roll(x, HEAD_DIM - ROT_DIM // 2, 1)
    dn = pltpu.roll(x, ROT_DIM // 2, 1)
    return x * cos + jnp.where(lane < ROT_DIM // 2, up, dn) * sin


def _rope_tables(pos):
    inv = ROPE_THETA ** (-jnp.arange(0, ROT_DIM, 2, dtype=F32) / ROT_DIM)
    ang = pos.astype(F32)[:, None] * inv[None, :]
    c, s = jnp.cos(ang), jnp.sin(ang)
    n = pos.shape[0]
    cos = jnp.concatenate([c, c, jnp.ones((n, HEAD_DIM - ROT_DIM), F32)], -1)
    sin = jnp.concatenate([-s, s, jnp.zeros((n, HEAD_DIM - ROT_DIM), F32)], -1)
    return cos, sin


def _mm_kernel(x_ref, w_ref, o_ref, acc_ref):
    k = pl.program_id(2)

    @pl.when(k == 0)
    def _():
        acc_ref[...] = jnp.zeros_like(acc_ref)

    acc_ref[...] += _dot(x_ref[...], w_ref[...])

    @pl.when(k == pl.num_programs(2) - 1)
    def _():
        o_ref[...] = acc_ref[...].astype(o_ref.dtype)


def _mm(x, w, out_dtype=F32, tm=1024, tn=1024, tk=512):
    M, K = x.shape
    N = w.shape[1]
    tm, tn, tk = min(tm, M), min(tn, N), min(tk, K)
    assert M % tm == 0 and N % tn == 0 and K % tk == 0, (x.shape, w.shape)
    return pl.pallas_call(
        _mm_kernel,
        out_shape=jax.ShapeDtypeStruct((M, N), out_dtype),
        grid=(M // tm, N // tn, K // tk),
        in_specs=[pl.BlockSpec((tm, tk), lambda i, j, k: (i, k)),
                  pl.BlockSpec((tk, tn), lambda i, j, k: (k, j))],
        out_specs=pl.BlockSpec((tm, tn), lambda i, j, k: (i, j)),
        scratch_shapes=[pltpu.VMEM((tm, tn), F32)],
        compiler_params=_cp("parallel", "parallel", "arbitrary"),
        name="mm",
    )(x, w)


def _res_ln_kernel(x_ref, y_ref, g_ref, b_ref, o_ref, ob_ref, *, alpha):
    v = alpha * x_ref[...] + y_ref[...]
    mu = jnp.mean(v, -1, keepdims=True)
    d = v - mu
    var = jnp.mean(d * d, -1, keepdims=True)
    o = d * lax.rsqrt(var + 1e-5) * g_ref[...] + b_ref[...]
    o_ref[...] = o
    ob_ref[...] = o.astype(BF16)


def _res_ln(x, y, g, b, alpha, tm=256):
    M, D = x.shape
    tm = min(tm, M)
    row = pl.BlockSpec((tm, D), lambda i: (i, 0))
    vec = pl.BlockSpec((1, D), lambda i: (0, 0))
    return pl.pallas_call(
        functools.partial(_res_ln_kernel, alpha=alpha),
        out_shape=(jax.ShapeDtypeStruct((M, D), F32), jax.ShapeDtypeStruct((M, D), BF16)),
        grid=(M // tm,),
        in_specs=[row, row, vec, vec],
        out_specs=(row, row),
        compiler_params=_cp("parallel"),
        name="res_ln",
    )(x, y, g.reshape(1, D), b.reshape(1, D))


def _mm_res_ln_kernel(x_ref, w_ref, r_ref, g_ref, b_ref, o_ref, ob_ref, acc_ref, *, alpha):
    k = pl.program_id(1)

    @pl.when(k == 0)
    def _():
        acc_ref[...] = jnp.zeros_like(acc_ref)

    acc_ref[...] += _dot(x_ref[...], w_ref[...])

    @pl.when(k == pl.num_programs(1) - 1)
    def _():
        v = alpha * r_ref[...] + acc_ref[...]
        mu = jnp.mean(v, -1, keepdims=True)
        d = v - mu
        var = jnp.mean(d * d, -1, keepdims=True)
        o = d * lax.rsqrt(var + 1e-5) * g_ref[...] + b_ref[...]
        o_ref[...] = o
        ob_ref[...] = o.astype(BF16)


def _mm_res_ln(x, w, resid, g, b, alpha, tm=256, tk=512):
    M, K = x.shape
    D = w.shape[1]
    tm, tk = min(tm, M), min(tk, K)
    row = pl.BlockSpec((tm, D), lambda i, k: (i, 0))
    vec = pl.BlockSpec((1, D), lambda i, k: (0, 0))
    return pl.pallas_call(
        functools.partial(_mm_res_ln_kernel, alpha=alpha),
        out_shape=(jax.ShapeDtypeStruct((M, D), F32), jax.ShapeDtypeStruct((M, D), BF16)),
        grid=(M // tm, K // tk),
        in_specs=[pl.BlockSpec((tm, tk), lambda i, k: (i, k)),
                  pl.BlockSpec((tk, D), lambda i, k: (k, 0)),
                  row, vec, vec],
        out_specs=(row, row),
        scratch_shapes=[pltpu.VMEM((tm, D), F32)],
        compiler_params=_cp("parallel", "arbitrary"),
        name="mm_res_ln",
    )(x, w, resid, g.reshape(1, D), b.reshape(1, D))


def _kvprep_kernel(z_ref, cos_ref, sin_ref, kv_ref, win_ref):
    cos, sin = cos_ref[...], sin_ref[...]
    gw = NSA_KV * HEAD_DIM
    kv_ref[:, 0:2 * gw] = z_ref[:, 0:2 * gw]
    kv_ref[:, 3 * gw:4 * gw] = z_ref[:, 3 * gw:4 * gw]
    win_ref[:, gw:2 * gw] = z_ref[:, 5 * gw:6 * gw]
    for g in range(NSA_KV):
        a = 2 * gw + g * HEAD_DIM
        kv_ref[:, a:a + HEAD_DIM] = _rope(z_ref[:, a:a + HEAD_DIM], cos, sin)
        a = 4 * gw + g * HEAD_DIM
        win_ref[:, g * HEAD_DIM:(g + 1) * HEAD_DIM] = _rope(z_ref[:, a:a + HEAD_DIM], cos, sin)


def _kvprep(zkv, cos, sin, T, tm=256):
    M = zkv.shape[0]
    tm = min(tm, T)
    nt = T // tm
    gw = NSA_KV * HEAD_DIM
    return pl.pallas_call(
        _kvprep_kernel,
        out_shape=(jax.ShapeDtypeStruct((M, 4 * gw), F32), jax.ShapeDtypeStruct((M, 2 * gw), F32)),
        grid=(M // tm,),
        in_specs=[pl.BlockSpec((tm, 6 * gw), lambda i: (i, 0)),
                  pl.BlockSpec((tm, HEAD_DIM), lambda i: (i % nt, 0)),
                  pl.BlockSpec((tm, HEAD_DIM), lambda i: (i % nt, 0))],
        out_specs=(pl.BlockSpec((tm, 4 * gw), lambda i: (i, 0)),
                   pl.BlockSpec((tm, 2 * gw), lambda i: (i, 0))),
        compiler_params=_cp("parallel"),
        name="kvprep",
    )(zkv, cos, sin)


CMP_PAGES = 8


def _cmp_a_kernel(pt_ref, *refs, P, W):
    page_refs = refs[:P]
    w_ref = refs[P]
    o_ref = refs[P + 1]
    hp = PAGE_SIZE // CMP_STRIDE
    for kv in range(2):
        acc = jnp.zeros((NSA_KV * P * hp, 2 * HEAD_DIM), F32)
        for j in range(CMP_STRIDE):
            pieces = []
            for g in range(NSA_KV):
                c0 = j * W + kv * NSA_KV * HEAD_DIM + g * HEAD_DIM
                for p in range(P):
                    pieces.append(page_refs[p][0, :, c0:c0 + HEAD_DIM])
            lhs = jnp.concatenate(pieces, axis=0).astype(BF16)
            acc = acc + _dot(lhs, w_ref[kv, j])
        rows = P * hp
        for g in range(NSA_KV):
            o_ref[0, kv, g] = acc[g * rows:(g + 1) * rows]


def _cmp_a(src, page_table, w1ab):
    B, n_pages = page_table.shape
    P = math.gcd(CMP_PAGES, n_pages)
    hp = PAGE_SIZE // CMP_STRIDE
    W = src.shape[2]
    src = src.reshape(src.shape[0], hp, CMP_STRIDE * W)

    def page_spec(p):
        return pl.BlockSpec((1, hp, CMP_STRIDE * W), lambda b, c, pt: (pt[b, c * P + p], 0, 0))

    return pl.pallas_call(
        functools.partial(_cmp_a_kernel, P=P, W=W),
        out_shape=jax.ShapeDtypeStruct((B, 2, NSA_KV, n_pages * hp, 2 * HEAD_DIM), F32),
        grid_spec=pltpu.PrefetchScalarGridSpec(
            num_scalar_prefetch=1,
            grid=(B, n_pages // P),
            in_specs=[page_spec(p) for p in range(P)]
            + [pl.BlockSpec((2, CMP_STRIDE, HEAD_DIM, 2 * HEAD_DIM), lambda b, c, pt: (0, 0, 0, 0))],
            out_specs=pl.BlockSpec((1, 2, NSA_KV, P * hp, 2 * HEAD_DIM), lambda b, c, pt: (b, 0, 0, c, 0)),
        ),
        compiler_params=_cp("parallel", "parallel"),
        name="cmp_a",
    )(page_table, *([src] * P), w1ab)


def _cmp_b_kernel(pab_ref, pe_ref, w1f_ref, b1_ref, w2_ref, kc_ref, vc_ref):
    n = pab_ref.shape[3]
    outs = (kc_ref, vc_ref)
    for kv in range(2):
        bias = b1_ref[kv:kv + 1, :]
        for half in range(2):
            bias = bias + _dot(pe_ref[kv, half].astype(BF16), w1f_ref[kv, half])[0:1, :]
        w2 = w2_ref[kv].astype(BF16)
        for g in range(NSA_KV):
            pa = pab_ref[0, kv, g, :, 0:HEAD_DIM]
            pb = pab_ref[0, kv, g, :, HEAD_DIM:2 * HEAD_DIM]
            pb_next = pltpu.roll(pb, n - 1, 0)
            h = _gelu(pa + pb_next + bias)
            outs[kv][0, g] = _dot(h.astype(BF16), w2)


def _cmp_b(pab, pe, w1, b1, w2):
    B, _, _, n, _ = pab.shape
    pe_flat = jnp.broadcast_to(pe.reshape(2, 2, 1, CMP_STRIDE * HEAD_DIM), (2, 2, 8, CMP_STRIDE * HEAD_DIM))
    w1_flat = w1.reshape(2, 2, CMP_STRIDE * HEAD_DIM, HEAD_DIM).astype(BF16)
    full = lambda shape: pl.BlockSpec(shape, lambda b: (0,) * len(shape))
    out_spec = pl.BlockSpec((1, NSA_KV, n, HEAD_DIM), lambda b: (b, 0, 0, 0))
    return pl.pallas_call(
        _cmp_b_kernel,
        out_shape=(jax.ShapeDtypeStruct((B, NSA_KV, n, HEAD_DIM), F32),) * 2,
        grid=(B,),
        in_specs=[pl.BlockSpec((1, 2, NSA_KV, n, 2 * HEAD_DIM), lambda b: (b, 0, 0, 0, 0)),
                  full(pe_flat.shape), full(w1_flat.shape), full(b1.shape), full(w2.shape)],
        out_specs=(out_spec, out_spec),
        compiler_params=_cp("parallel"),
        name="cmp_b",
    )(pab, pe_flat, w1_flat, b1, w2)


def _compress(src, page_table, cmp_pe, cmp_w1, cmp_b1, cmp_w2):
    w1ab = jnp.concatenate([cmp_w1[:, :CMP_STRIDE], cmp_w1[:, CMP_STRIDE:]], -1).astype(BF16)
    pab = _cmp_a(src, page_table, w1ab)
    return _cmp_b(pab, cmp_pe, cmp_w1, cmp_b1, cmp_w2)


def _cmp_to_slc_mask(ncp, nsp, n_cmp, n_slc):
    ci = lax.broadcasted_iota(I32, (ncp, nsp), 0)
    si = lax.broadcasted_iota(I32, (ncp, nsp), 1)
    ov = ((ci * CMP_STRIDE < (si + 1) * SLC_BLOCK) & (ci * CMP_STRIDE + CMP_BLOCK > si * SLC_BLOCK)
          & (ci < n_cmp) & (si < n_slc))
    return jnp.where(ov, 1.0, 0.0).astype(BF16)


def _nsa_prefill_kernel(q_ref, cos_ref, sin_ref, kc_ref, vc_ref, ks_ref, vs_ref, kw_ref, vw_ref, gt_ref, o_ref,
                        *, tq, T, n_cmp, n_slc, wlen):
    i = pl.program_id(2)
    t0 = i * tq
    scale = HEAD_DIM ** -0.5
    rep = NSA_REP
    q4 = q_ref[...]
    Q = jnp.concatenate([q4[:, r * HEAD_DIM:(r + 1) * HEAD_DIM] for r in range(rep)], axis=0)
    cos = jnp.concatenate([cos_ref[...]] * rep, axis=0)
    sin = jnp.concatenate([sin_ref[...]] * rep, axis=0)
    Qb = Q.astype(BF16)
    Qrb = _rope(Q, cos, sin).astype(BF16)
    qpos1 = t0 + lax.broadcasted_iota(I32, (tq, 1), 0)
    qpos = jnp.concatenate([qpos1] * rep, axis=0)
    ncp = kc_ref.shape[2]
    kc = kc_ref[0, 0].astype(BF16)
    vc = vc_ref[0, 0].astype(BF16)
    lg = _dot_nt(Qb, kc) * scale
    cidx = lax.broadcasted_iota(I32, (1, ncp), 1)
    cmask = (cidx * CMP_STRIDE + CMP_BLOCK - 1 <= qpos) & (cidx < n_cmp)
    p = _masked_softmax(lg, cmask)
    o_cmp = _dot(p.astype(BF16), vc)
    psum = p[0:tq]
    for r in range(1, rep):
        psum = psum + p[r * tq:(r + 1) * tq]
    nsp = 128
    imp = _split3_dot(psum, _cmp_to_slc_mask(ncp, nsp, n_cmp, n_slc))
    blk = lax.broadcasted_iota(I32, (1, nsp), 1)
    forced = (blk == 0) | (blk == jnp.right_shift(qpos1, SLC_SHIFT))
    future = blk * SLC_BLOCK > qpos1
    imp = jnp.where(forced, FORCE_SCORE, jnp.where(future, NEG_INF, imp))
    imp = jnp.where(blk < n_slc, imp, NEG_INF)
    rank = jnp.zeros((tq, nsp), F32)
    for s2 in range(n_slc):
        col = imp[:, s2:s2 + 1]
        beats = (col > imp) | ((col == imp) & (blk > s2))
        rank = rank + jnp.where(beats, 1.0, 0.0)
    sel = (rank < float(min(TOP_BLOCKS, n_slc))) & (blk < n_slc)
    srow = lax.broadcasted_iota(I32, (nsp, T), 0)
    kcol = lax.broadcasted_iota(I32, (nsp, T), 1)
    expand = jnp.where(jnp.right_shift(kcol, SLC_SHIFT) == srow, 1.0, 0.0).astype(BF16)
    selk1 = _dot(jnp.where(sel, 1.0, 0.0).astype(BF16), expand)
    selk = jnp.concatenate([selk1] * rep, axis=0)
    kpos = lax.broadcasted_iota(I32, (1, T), 1)
    smask = (selk > 0.5) & (kpos <= qpos)
    ls = _dot_nt(Qrb, ks_ref[...].astype(BF16)) * scale
    ps = _masked_softmax(ls, smask)
    o_s = _dot(ps.astype(BF16), vs_ref[...].astype(BF16))
    start = pl.multiple_of(jnp.clip(t0 - WINDOW, 0, T - wlen), 128)
    kw = kw_ref[pl.ds(start, wlen), :].astype(BF16)
    vw = vw_ref[pl.ds(start, wlen), :].astype(BF16)
    dpos = qpos - (start + lax.broadcasted_iota(I32, (1, wlen), 1))
    lw = _dot_nt(Qrb, kw) * scale
    pw = _masked_softmax(lw, (dpos >= 0) & (dpos <= WINDOW))
    o_w = _dot(pw.astype(BF16), vw)
    gt = _sigmoid(gt_ref[0])
    for r in range(rep):
        rows = slice(r * tq, (r + 1) * tq)
        o = (gt[:, 3 * r:3 * r + 1] * o_cmp[rows] + gt[:, 3 * r + 1:3 * r + 2] * o_s[rows]
             + gt[:, 3 * r + 2:3 * r + 3] * o_w[rows])
        o_ref[:, r * HEAD_DIM:(r + 1) * HEAD_DIM] = o.astype(o_ref.dtype)


def _nsa_prefill(zq, new_kv, win, kc, vc, gates4, cos, sin, B, T, tq=128):
    tq = min(tq, T)
    nt = T // tq
    n_cmp = (T - CMP_BLOCK) // CMP_STRIDE + 1
    n_slc = -(-T // SLC_BLOCK)
    assert n_slc <= 128 and T % tq == 0
    wlen = min(T, WINDOW + tq)
    ncp = kc.shape[2]
    gw = NSA_KV
    kern = functools.partial(_nsa_prefill_kernel, tq=tq, T=T, n_cmp=n_cmp, n_slc=n_slc, wlen=wlen)
    return pl.pallas_call(
        kern,
        out_shape=jax.ShapeDtypeStruct((B * T, NSA_HEADS * HEAD_DIM), BF16),
        grid=(B, NSA_KV, nt),
        in_specs=[pl.BlockSpec((tq, NSA_REP * HEAD_DIM), lambda b, g, i: (b * nt + i, g)),
                  pl.BlockSpec((tq, HEAD_DIM), lambda b, g, i: (i, 0)),
                  pl.BlockSpec((tq, HEAD_DIM), lambda b, g, i: (i, 0)),
                  pl.BlockSpec((1, 1, ncp, HEAD_DIM), lambda b, g, i: (b, g, 0, 0)),
                  pl.BlockSpec((1, 1, ncp, HEAD_DIM), lambda b, g, i: (b, g, 0, 0)),
                  pl.BlockSpec((T, HEAD_DIM), lambda b, g, i: (b, 2 * gw + g)),
                  pl.BlockSpec((T, HEAD_DIM), lambda b, g, i: (b, 3 * gw + g)),
                  pl.BlockSpec((T, HEAD_DIM), lambda b, g, i: (b, g)),
                  pl.BlockSpec((T, HEAD_DIM), lambda b, g, i: (b, gw + g)),
                  pl.BlockSpec((1, tq, 128), lambda b, g, i: (g, b * nt + i, 0))],
        out_specs=pl.BlockSpec((tq, NSA_REP * HEAD_DIM), lambda b, g, i: (b * nt + i, g)),
        compiler_params=_cp("parallel", "parallel", "parallel"),
        name="nsa_prefill",
    )(zq, cos, sin, kc, vc, new_kv, new_kv, win, win, gates4)


def _nsa_dec_sel_kernel(q_ref, kc_ref, vc_ref, ocmp_ref, sel_ref, *, tp, n_cmp, n_slc, q_start):
    scale = HEAD_DIM ** -0.5
    rep = NSA_REP
    ncp = kc_ref.shape[2]
    nsp = -(-n_slc // 128) * 128
    qpos1 = q_start + lax.broadcasted_iota(I32, (tp, 1), 0)
    qpos = jnp.concatenate([qpos1] * rep, axis=0)
    cidx = lax.broadcasted_iota(I32, (1, ncp), 1)
    cmask = (cidx * CMP_STRIDE + CMP_BLOCK - 1 <= qpos) & (cidx < n_cmp)
    mm = _cmp_to_slc_mask(ncp, nsp, n_cmp, n_slc)
    blk = lax.broadcasted_iota(I32, (1, nsp), 1)
    lane16 = lax.broadcasted_iota(I32, (tp, 128), 1)
    for g in range(NSA_KV):
        c0 = g * rep * HEAD_DIM
        Q = jnp.concatenate([q_ref[0, :, c0 + r * HEAD_DIM:c0 + (r + 1) * HEAD_DIM] for r in range(rep)], axis=0)
        lg = _dot_nt(Q.astype(BF16), kc_ref[0, g].astype(BF16)) * scale
        p = _masked_softmax(lg, cmask)
        ocmp_ref[0, g] = _dot(p.astype(BF16), vc_ref[0, g].astype(BF16))
        psum = p[0:tp]
        for r in range(1, rep):
            psum = psum + p[r * tp:(r + 1) * tp]
        imp = _split3_dot(psum, mm)
        forced = (blk == 0) | (blk == jnp.right_shift(qpos1, SLC_SHIFT))
        future = blk * SLC_BLOCK > qpos1
        imp = jnp.where(forced, FORCE_SCORE, jnp.where(future, NEG_INF, imp))
        taken = jnp.broadcast_to(blk >= n_slc, (tp, nsp))
        sel = jnp.zeros((tp, 128), I32)
        for it in range(min(TOP_BLOCKS, n_slc)):
            cand = jnp.where(taken, NEG_INF, imp)
            mx = jnp.max(cand, -1, keepdims=True)
            hit = (cand == mx) & jnp.logical_not(taken)
            idx = jnp.min(jnp.where(hit, blk, nsp), -1, keepdims=True)
            taken = taken | (blk == idx)
            sel = jnp.where(lane16 == it, idx, sel)
        sel_ref[0, g] = sel


def _nsa_dec_sel(zq3, kc, vc, n_cmp, n_slc, q_start):
    B, tp, _ = zq3.shape
    ncp = kc.shape[2]
    kern = functools.partial(_nsa_dec_sel_kernel, tp=tp, n_cmp=n_cmp, n_slc=n_slc, q_start=q_start)
    cspec = pl.BlockSpec((1, NSA_KV, ncp, HEAD_DIM), lambda b: (b, 0, 0, 0))
    return pl.pallas_call(
        kern,
        out_shape=(jax.ShapeDtypeStruct((B, NSA_KV, NSA_REP * tp, HEAD_DIM), F32),
                   jax.ShapeDtypeStruct((B, NSA_KV, tp, 128), I32)),
        grid=(B,),
        in_specs=[pl.BlockSpec((1, tp, NSA_HEADS * HEAD_DIM), lambda b: (b, 0, 0)), cspec, cspec],
        out_specs=(pl.BlockSpec((1, NSA_KV, NSA_REP * tp, HEAD_DIM), lambda b: (b, 0, 0, 0)),
                   pl.BlockSpec((1, NSA_KV, tp, 128), lambda b: (b, 0, 0, 0))),
        compiler_params=_cp("parallel"),
        name="nsa_dec_sel",
    )(zq3, kc, vc)


def _nsa_dec_attn_kernel(pt_ref, sel_ref, q_ref, cos_ref, sin_ref, *refs, tp, tv, n_past_blk, q_start, ktop):
    kv_refs = refs[:2 * tv]
    (nk_ref, nv_ref, wk_ref, wv_ref, nwk_ref, nwv_ref, ocmp_ref, gt_ref, o_ref,
     qr_s, m_s, l_s, acc_s) = refs[2 * tv:]
    b, g, k = pl.program_id(0), pl.program_id(1), pl.program_id(2)
    rep = NSA_REP
    scale = HEAD_DIM ** -0.5
    rows = rep * tp
    trow = lax.broadcasted_iota(I32, (rows, 1), 0) & (tp - 1)

    @pl.when(k == 0)
    def _():
        Q = jnp.concatenate([q_ref[0, :, r * HEAD_DIM:(r + 1) * HEAD_DIM] for r in range(rep)], axis=0)
        cos = jnp.concatenate([cos_ref[...]] * rep, axis=0)
        sin = jnp.concatenate([sin_ref[...]] * rep, axis=0)
        qr_s[...] = _rope(Q, cos, sin)
        m_s[...] = jnp.full_like(m_s, NEG_INF)
        l_s[...] = jnp.zeros_like(l_s)
        acc_s[...] = jnp.zeros_like(acc_s)

    Qrb = qr_s[...].astype(BF16)
    qpos = q_start + trow

    def online_update(lg, mask, v):
        lg = jnp.where(mask, lg, NEG_INF)
        m_old = m_s[...]
        m_new = jnp.maximum(m_old, jnp.max(lg, -1, keepdims=True))
        m_safe = jnp.where(m_new > NEG_INF, m_new, 0.0)
        a = jnp.exp(m_old - m_safe)
        e = jnp.exp(lg - m_safe)
        l_s[...] = a * l_s[...] + jnp.sum(e, -1, keepdims=True)
        acc_s[...] = a * acc_s[...] + _dot(e.astype(BF16), v)
        m_s[...] = m_new

    kb = jnp.concatenate([kv_refs[2 * t][0, 0].astype(BF16) for t in range(tv)], axis=0)
    vb = jnp.concatenate([kv_refs[2 * t + 1][0, 0].astype(BF16) for t in range(tv)], axis=0)
    lg = _dot_nt(Qrb, kb) * scale
    col = lax.broadcasted_iota(I32, (1, tv * SLC_BLOCK), 1)
    cblk = jnp.right_shift(col, SLC_SHIFT)
    far = 1 << 24
    mask = None
    for t in range(tv):
        s = sel_ref[b, g, t, k]
        s = jnp.where(s < n_past_blk, s, far)
        kpos = s * SLC_BLOCK + (col - t * SLC_BLOCK)
        mt = (cblk == t) & (trow == t) & (kpos <= qpos)
        mask = mt if mask is None else (mask | mt)
    online_update(lg, mask, vb)

    @pl.when(k == ktop - 1)
    def _():
        jrow = lax.broadcasted_iota(I32, (1, tp), 1)
        has_new = None
        for t in range(tv):
            f = sel_ref[b, g, t, 0] == n_past_blk
            for kk in range(1, ktop):
                f = f | (sel_ref[b, g, t, kk] == n_past_blk)
            hn = trow == jnp.where(f, t, -1)
            has_new = hn if has_new is None else (has_new | hn)
        nmask = has_new & (jrow <= trow) & (jrow < tv)
        online_update(_dot_nt(Qrb, nk_ref[0].astype(BF16)) * scale, nmask, nv_ref[0].astype(BF16))
        o_s = acc_s[...] / jnp.maximum(l_s[...], 1e-30)
        P = wk_ref.shape[1]
        jw = lax.broadcasted_iota(I32, (1, P), 1)
        dpos_w = (qpos - (q_start - P)) - jw
        mask_w = (dpos_w >= 0) & (dpos_w <= WINDOW) & (q_start - P + jw >= 0)
        mask_n = (jrow <= trow) & (jrow < tv)
        lw = jnp.where(mask_w, _dot_nt(Qrb, wk_ref[0].astype(BF16)) * scale, NEG_INF)
        ln = jnp.where(mask_n, _dot_nt(Qrb, nwk_ref[0].astype(BF16)) * scale, NEG_INF)
        mx = jnp.maximum(jnp.max(lw, -1, keepdims=True), jnp.max(ln, -1, keepdims=True))
        mx = jnp.where(mx > NEG_INF, mx, 0.0)
        ew, en = jnp.exp(lw - mx), jnp.exp(ln - mx)
        den = jnp.maximum(jnp.sum(ew, -1, keepdims=True) + jnp.sum(en, -1, keepdims=True), 1e-30)
        o_w = (_dot(ew.astype(BF16), wv_ref[0].astype(BF16)) + _dot(en.astype(BF16), nwv_ref[0].astype(BF16))) / den
        gt = _sigmoid(gt_ref[0])
        o_c = ocmp_ref[0, 0]
        for r in range(rep):
            rs = slice(r * tp, (r + 1) * tp)
            o = (gt[:, 3 * r:3 * r + 1] * o_c[rs] + gt[:, 3 * r + 1:3 * r + 2] * o_s[rs]
                 + gt[:, 3 * r + 2:3 * r + 3] * o_w[rs])
            o_ref[0, :, r * HEAD_DIM:(r + 1) * HEAD_DIM] = o.astype(o_ref.dtype)


def _nsa_dec_attn(page_table, sel, zq3, cos, sin, cache_l, new_kv3, win_state, win3, ocmp, gates4, tv, q_start):
    B, tp, _ = zq3.shape
    n_pool = cache_l.shape[0]
    bpp = PAGE_SIZE // SLC_BLOCK
    cache4 = cache_l.reshape(n_pool, bpp, SLC_BLOCK, 4 * NSA_KV * HEAD_DIM)
    n_past_blk = page_table.shape[1] * bpp
    ktop = sel.shape[-1]
    P = win_state.shape[1]
    wst = win_state.reshape(B, P, 2 * NSA_KV * HEAD_DIM)
    gw = NSA_KV

    def cache_spec(t, slot):
        def imap(b, g, k, pt, sl):
            s = jnp.minimum(sl[b, g, t, k], n_past_blk - 1)
            return (pt[b, s // bpp], s % bpp, 0, slot * gw + g)
        return pl.BlockSpec((1, 1, SLC_BLOCK, HEAD_DIM), imap)

    cache_specs = []
    for t in range(tv):
        cache_specs += [cache_spec(t, 2), cache_spec(t, 3)]
    kern = functools.partial(_nsa_dec_attn_kernel, tp=tp, tv=tv, n_past_blk=n_past_blk, q_start=q_start, ktop=ktop)
    rows = NSA_REP * tp
    return pl.pallas_call(
        kern,
        out_shape=jax.ShapeDtypeStruct((B, tp, NSA_HEADS * HEAD_DIM), BF16),
        grid_spec=pltpu.PrefetchScalarGridSpec(
            num_scalar_prefetch=2,
            grid=(B, NSA_KV, ktop),
            in_specs=[pl.BlockSpec((1, tp, NSA_REP * HEAD_DIM), lambda b, g, k, pt, sl: (b, 0, g)),
                      pl.BlockSpec((tp, HEAD_DIM), lambda b, g, k, pt, sl: (0, 0)),
                      pl.BlockSpec((tp, HEAD_DIM), lambda b, g, k, pt, sl: (0, 0))]
            + cache_specs
            + [pl.BlockSpec((1, tp, HEAD_DIM), lambda b, g, k, pt, sl: (b, 0, 2 * gw + g)),
               pl.BlockSpec((1, tp, HEAD_DIM), lambda b, g, k, pt, sl: (b, 0, 3 * gw + g)),
               pl.BlockSpec((1, P, HEAD_DIM), lambda b, g, k, pt, sl: (b, 0, g)),
               pl.BlockSpec((1, P, HEAD_DIM), lambda b, g, k, pt, sl: (b, 0, gw + g)),
               pl.BlockSpec((1, tp, HEAD_DIM), lambda b, g, k, pt, sl: (b, 0, g)),
               pl.BlockSpec((1, tp, HEAD_DIM), lambda b, g, k, pt, sl: (b, 0, gw + g)),
               pl.BlockSpec((1, 1, rows, HEAD_DIM), lambda b, g, k, pt, sl: (b, g, 0, 0)),
               pl.BlockSpec((1, tp, 128), lambda b, g, k, pt, sl: (g, b, 0))],
            out_specs=pl.BlockSpec((1, tp, NSA_REP * HEAD_DIM), lambda b, g, k, pt, sl: (b, 0, g)),
            scratch_shapes=[pltpu.VMEM((rows, HEAD_DIM), F32), pltpu.VMEM((rows, 1), F32),
                            pltpu.VMEM((rows, 1), F32), pltpu.VMEM((rows, HEAD_DIM), F32)],
        ),
        compiler_params=_cp("parallel", "parallel", "arbitrary"),
        name="nsa_dec_attn",
    )(page_table, sel, zq3, cos, sin, *([cache4] * (2 * tv)), new_kv3, new_kv3, wst, wst, win3, win3, ocmp, gates4)


def _gmlp_kernel(z_ref, g_ref, b_ref, ws_ref, bst_ref, o_ref, v_ref, *, lc):
    z = _gelu(z_ref[...])
    u, v = z[:, :W_B], z[:, W_B:]
    mu = jnp.mean(v, -1, keepdims=True)
    d = v - mu
    var = jnp.mean(d * d, -1, keepdims=True)
    vn = d * lax.rsqrt(var + 1e-5) * g_ref[...] + b_ref[...]
    v_ref[...] = vn
    vb = vn.astype(BF16)
    ti = lax.broadcasted_iota(I32, (lc, lc), 0)
    si = lax.broadcasted_iota(I32, (lc, lc), 1)
    for g in range(GMLP_GROUPS):
        cs = slice(g * GMLP_GDIM, (g + 1) * GMLP_GDIM)
        w = jnp.where(si <= ti, ws_ref[g], 0.0).astype(BF16)
        mixed = _dot(w, vb[:, cs]) + bst_ref[:, g:g + 1]
        o_ref[:, cs] = (u[:, cs] * mixed).astype(o_ref.dtype)


def _gmlp(zuv, ln_g, ln_b, ws, bs, lc):
    M = zuv.shape[0]
    ws = ws[:, :lc, :lc]
    bst = bs[:, :lc].T
    return pl.pallas_call(
        functools.partial(_gmlp_kernel, lc=lc),
        out_shape=(jax.ShapeDtypeStruct((M, W_B), BF16), jax.ShapeDtypeStruct((M, W_B), F32)),
        grid=(M // lc,),
        in_specs=[pl.BlockSpec((lc, 2 * W_B), lambda i: (i, 0)),
                  pl.BlockSpec((1, W_B), lambda i: (0, 0)),
                  pl.BlockSpec((1, W_B), lambda i: (0, 0)),
                  pl.BlockSpec((GMLP_GROUPS, lc, lc), lambda i: (0, 0, 0)),
                  pl.BlockSpec((lc, GMLP_GROUPS), lambda i: (0, 0))],
        out_specs=(pl.BlockSpec((lc, W_B), lambda i: (i, 0)), pl.BlockSpec((lc, W_B), lambda i: (i, 0))),
        compiler_params=_cp("parallel"),
        name="gmlp",
    )(zuv, ln_g.reshape(1, W_B), ln_b.reshape(1, W_B), ws, bst)


def _mlstm_kernel(bi_ref, bf_ref, q_ref, k_ref, v_ref, og_ref, sm_ref, smt_ref, ng_ref, c0_ref, n0_ref, m0_ref,
                  h_ref, c_ref, n_ref, m_ref, c_s, n_s, m_s, *, lc, valid, ci_col, cf_col):
    c = pl.program_id(1)

    @pl.when(c == 0)
    def _():
        c_s[...] = c0_ref[0]
        n_s[...] = n0_ref[0]
        m_s[...] = m0_ref[0]

    ti = lax.broadcasted_iota(I32, (lc, lc), 0)
    si = lax.broadcasted_iota(I32, (lc, lc), 1)
    tril = (si <= ti) & (si < valid)
    rvalid = lax.broadcasted_iota(I32, (lc, 1), 0) < valid
    cvalid = lax.broadcasted_iota(I32, (1, lc), 1) < valid
    last = valid - 1
    kscale = MLSTM_DQK ** -0.5
    for h in range(MLSTM_HEADS):
        ig_c = sm_ref[:, ci_col + h:ci_col + h + 1] + bi_ref[h]
        lf_c = _log_sigmoid(sm_ref[:, cf_col + h:cf_col + h + 1] + bf_ref[h])
        ig_r = smt_ref[0, h:h + 1, :] + bi_ref[h]
        lf_r = _log_sigmoid(smt_ref[0, MLSTM_HEADS + h:MLSTM_HEADS + h + 1, :] + bf_ref[h])
        F_c = jnp.sum(jnp.where(si <= ti, lf_r, 0.0), axis=1, keepdims=True)
        F_r = jnp.sum(jnp.where(ti <= si, lf_c, 0.0), axis=0, keepdims=True)
        D = jnp.where(tril, F_c - F_r + ig_r, NEG_INF)
        m_prev = m_s[h:h + 1, 0:1]
        m_inter = F_c + m_prev
        m_t = jnp.maximum(m_inter, jnp.max(D, axis=1, keepdims=True))
        qh = q_ref[:, h * MLSTM_DQK:(h + 1) * MLSTM_DQK]
        ks = k_ref[:, h * MLSTM_DQK:(h + 1) * MLSTM_DQK] * kscale
        vh = v_ref[:, h * MLSTM_DV:(h + 1) * MLSTM_DV]
        qb, kb, vb = qh.astype(BF16), ks.astype(BF16), vh.astype(BF16)
        S = _dot_nt(qb, kb) * jnp.exp(D - m_t)
        dec = jnp.exp(m_inter - m_t)
        C = c_s[h]
        n_row = n_s[h:h + 1, :]
        num = _dot(S.astype(BF16), vb) + dec * _dot_nt(qb, C.astype(BF16))
        den = jnp.sum(S, axis=1, keepdims=True) + dec * jnp.sum(qh * n_row, axis=1, keepdims=True)
        hh = num / jnp.maximum(jnp.abs(den), 1.0)
        m_new = m_t[last:last + 1, :]
        F_last = F_c[last:last + 1, :]
        wl_c = jnp.where(rvalid, jnp.exp(F_last - F_c + ig_c - m_new), 0.0)
        dl = jnp.exp(F_last + m_prev - m_new)
        c_s[h] = dl * C + _dot_tn((vh * wl_c).astype(BF16), kb)
        n_s[h:h + 1, :] = dl * n_row + jnp.sum(wl_c * ks, axis=0, keepdims=True)
        m_s[h:h + 1, :] = jnp.broadcast_to(m_new, (1, 128))
        mu = jnp.mean(hh, -1, keepdims=True)
        d = hh - mu
        var = jnp.mean(d * d, -1, keepdims=True)
        hn = d * lax.rsqrt(var + 1e-5) * ng_ref[h:h + 1, :]
        vs = slice(h * MLSTM_DV, (h + 1) * MLSTM_DV)
        h_ref[:, vs] = (_sigmoid(og_ref[:, vs]) * hn).astype(h_ref.dtype)

    @pl.when(c == pl.num_programs(1) - 1)
    def _():
        c_ref[0] = c_s[...]
        n_ref[0] = n_s[...]
        m_ref[0] = m_s[...]


def _mlstm(zcq, zck, zcv, zco, zsm, b_i, b_f, norm_g, C0, n0, m0, B, T, lc, valid, ci_col, cf_col):
    nc = T // lc
    H, DQ, DV = MLSTM_HEADS, MLSTM_DQK, MLSTM_DV
    smt = jnp.concatenate([zsm[:, ci_col:ci_col + H], zsm[:, cf_col:cf_col + H]], -1)
    smt = smt.reshape(B * nc, lc, 2 * H).transpose(0, 2, 1)
    m0b = jnp.broadcast_to(m0[..., None], (B, H, 128))
    row = lambda w: pl.BlockSpec((lc, w), lambda b, c: (b * nc + c, 0))
    smem = pl.BlockSpec(memory_space=pltpu.SMEM)
    kern = functools.partial(_mlstm_kernel, lc=lc, valid=valid, ci_col=ci_col, cf_col=cf_col)
    c_spec = pl.BlockSpec((1, H, DV, DQ), lambda b, c: (b, 0, 0, 0))
    n_spec = pl.BlockSpec((1, H, DQ), lambda b, c: (b, 0, 0))
    h, C, n, m = pl.pallas_call(
        kern,
        out_shape=(jax.ShapeDtypeStruct((B * T, H * DV), BF16), jax.ShapeDtypeStruct((B, H, DV, DQ), F32),
                   jax.ShapeDtypeStruct((B, H, DQ), F32), jax.ShapeDtypeStruct((B, H, 128), F32)),
        grid=(B, nc),
        in_specs=[smem, smem, row(H * DQ), row(H * DQ), row(H * DV), row(H * DV), row(128),
                  pl.BlockSpec((1, 2 * H, lc), lambda b, c: (b * nc + c, 0, 0)),
                  pl.BlockSpec((H, DV), lambda b, c: (0, 0)), c_spec, n_spec, n_spec],
        out_specs=(row(H * DV), c_spec, n_spec, n_spec),
        scratch_shapes=[pltpu.VMEM((H, DV, DQ), F32), pltpu.VMEM((H, DQ), F32), pltpu.VMEM((H, 128), F32)],
        compiler_params=_cp("parallel", "arbitrary"),
        name="mlstm",
    )(b_i, b_f, zcq, zck, zcv, zco, zsm, smt, norm_g, C0, n0, m0b)
    return h, C, n, m[..., 0]


def _merge_kernel(a_ref, b_ref, c_ref, wa_ref, wb_ref, wc_ref, ga_ref, gb_ref, gc_ref, o_ref):
    o = (_sigmoid(ga_ref[...]) * _dot(a_ref[...], wa_ref[...])
         + _sigmoid(gb_ref[...]) * _dot(b_ref[...], wb_ref[...])
         + _sigmoid(gc_ref[...]) * _dot(c_ref[...], wc_ref[...]))
    o_ref[...] = o.astype(o_ref.dtype)


def _merge(oa, ob, oc, wa, wb, wc, zmg, tm=512, tn=512):
    M, K = oa.shape
    D = wa.shape[1]
    tm, tn = min(tm, M), min(tn, D)
    nj = D // tn
    x_spec = pl.BlockSpec((tm, K), lambda i, j: (i, 0))
    w_spec = pl.BlockSpec((K, tn), lambda i, j: (0, j))
    g_spec = lambda o: pl.BlockSpec((tm, tn), lambda i, j: (i, o * nj + j))
    return pl.pallas_call(
        _merge_kernel,
        out_shape=jax.ShapeDtypeStruct((M, D), BF16),
        grid=(M // tm, nj),
        in_specs=[x_spec, x_spec, x_spec, w_spec, w_spec, w_spec, g_spec(0), g_spec(1), g_spec(2)],
        out_specs=pl.BlockSpec((tm, tn), lambda i, j: (i, j)),
        compiler_params=_cp("parallel", "parallel"),
        name="merge",
    )(oa, ob, oc, wa, wb, wc, zmg, zmg, zmg)


def _top16_rows(s, n_rows):
    tm = s.shape[1]
    riota = lax.broadcasted_iota(I32, (n_rows, tm), 0)
    taken = jnp.zeros((n_rows, tm), jnp.bool_)
    pos = jnp.full((n_rows, tm), -1, I32)
    vals, idxs = [], []
    for it in range(PEER_TOPK):
        cand = jnp.where(taken, NEG_INF, s)
        mx = jnp.max(cand, axis=0, keepdims=True)
        hit = (cand == mx) & jnp.logical_not(taken)
        idx = jnp.min(jnp.where(hit, riota, n_rows), axis=0, keepdims=True)
        one = riota == idx
        taken = taken | one
        pos = jnp.where(one, it, pos)
        vals.append(mx)
        idxs.append(idx)
    return vals, idxs, pos


def _peer_route_kernel(qt_ref, k1_ref, k2_ref, s1_ref, rw_ref, s2_ref, b2_ref):
    tm = qt_ref.shape[1]
    half = PEER_DQ // 2
    k1 = k1_ref[...].astype(BF16)
    k2 = k2_ref[...].astype(BF16)
    i16 = lax.broadcasted_iota(I32, (PEER_TOPK, tm), 0)
    for h in range(PEER_HEADS):
        q1 = qt_ref[h * PEER_DQ:h * PEER_DQ + half, :].astype(BF16)
        q2 = qt_ref[h * PEER_DQ + half:(h + 1) * PEER_DQ, :].astype(BF16)
        s1 = _dot(k1, q1)
        s2 = _dot(k2, q2)
        v1, _, pos1 = _top16_rows(s1, N_KEYS)
        v2, _, pos2 = _top16_rows(s2, N_KEYS)
        v2m = jnp.concatenate(v2, axis=0)
        cand = jnp.concatenate([v1[i] + v2m for i in range(PEER_TOPK)], axis=0)
        cv, cj, _ = _top16_rows(cand, PEER_TOPK * PEER_TOPK)
        z = jnp.zeros((1, tm), F32)
        wbits = jnp.zeros((PEER_TOPK, tm), I32)
        for it in range(PEER_TOPK):
            z = z + jnp.exp(cv[it] - cv[0])
            ii = jnp.right_shift(cj[it], 4)
            kk = cj[it] & (PEER_TOPK - 1)
            wbits = jnp.where(i16 == ii, wbits | jnp.left_shift(1, kk), wbits)
        in1, in2 = pos1 >= 0, pos2 >= 0
        s1_ref[h] = jnp.where(in1, (s1 - cv[0]) * LOG2E - jnp.log2(z), NEG_INF)
        s2_ref[h] = jnp.where(in2, s2 * LOG2E, NEG_INF)
        b2_ref[h] = jnp.where(in2, jnp.left_shift(1, jnp.maximum(pos2, 0)), 0)
        rw = jnp.zeros((N_KEYS, tm), I32)
        for i in range(PEER_TOPK):
            rw = jnp.where(pos1 == i, wbits[i:i + 1, :], rw)
        rw_ref[h] = rw


def _peer_route(qt, k1, k2, tm=256):
    M = qt.shape[1]
    tm = min(tm, M)
    spec = pl.BlockSpec((PEER_HEADS, N_KEYS, tm), lambda i: (0, 0, i))
    shp = lambda dt: jax.ShapeDtypeStruct((PEER_HEADS, N_KEYS, M), dt)
    return pl.pallas_call(
        _peer_route_kernel,
        out_shape=(shp(F32), shp(I32), shp(F32), shp(I32)),
        grid=(M // tm,),
        in_specs=[pl.BlockSpec((PEER_HEADS * PEER_DQ, tm), lambda i: (0, i)),
                  pl.BlockSpec((N_KEYS, PEER_DQ // 2), lambda i: (0, 0)),
                  pl.BlockSpec((N_KEYS, PEER_DQ // 2), lambda i: (0, 0))],
        out_specs=(spec, spec, spec, spec),
        compiler_params=_cp("parallel"),
        name="peer_route",
    )(qt, k1, k2)


def _peer_dense_kernel(xt_ref, u_ref, vt_ref, s1_ref, rw_ref, s2_ref, b2_ref, o_ref, *, te):
    e = pl.program_id(1)

    @pl.when(e == 0)
    def _():
        o_ref[...] = jnp.zeros_like(o_ref)

    act = _gelu(_dot(u_ref[...], xt_ref[...]))
    tm = act.shape[1]
    per = te // N_KEYS
    parts = []
    for al in range(per):
        a = e * per + al
        gate = jnp.zeros((N_KEYS, tm), F32)
        for h in range(PEER_HEADS):
            wt = jnp.exp2(s1_ref[h, pl.ds(a, 1), :] + s2_ref[h])
            hit = (rw_ref[h, pl.ds(a, 1), :] & b2_ref[h]) != 0
            gate = gate + jnp.where(hit, wt, 0.0)
        parts.append((act[al * N_KEYS:(al + 1) * N_KEYS] * gate).astype(BF16))
    o_ref[...] += _dot(vt_ref[...], jnp.concatenate(parts, axis=0))


def _peer_dense(xt, u, vt, s1l, rw, s2l, b2, tm=512, te=256):
    D, M = xt.shape
    NE = u.shape[0]
    tm = min(tm, M)
    rspec = pl.BlockSpec((PEER_HEADS, N_KEYS, tm), lambda i, e: (0, 0, i))
    return pl.pallas_call(
        functools.partial(_peer_dense_kernel, te=te),
        out_shape=jax.ShapeDtypeStruct((D, M), F32),
        grid=(M // tm, NE // te),
        in_specs=[pl.BlockSpec((D, tm), lambda i, e: (0, i)),
                  pl.BlockSpec((te, D), lambda i, e: (e, 0)),
                  pl.BlockSpec((D, te), lambda i, e: (0, e)),
                  rspec, rspec, rspec, rspec],
        out_specs=pl.BlockSpec((D, tm), lambda i, e: (0, i)),
        compiler_params=_cp("parallel", "arbitrary"),
        name="peer_dense",
    )(xt, u, vt, s1l, rw, s2l, b2)


def _peer(x1, x1b, lw, alpha):
    M, D = x1.shape
    mp = -(-M // 128) * 128
    xt = x1b.T
    if mp != M:
        xt = jnp.pad(xt, ((0, 0), (0, mp - M)))
    qt = _mm(lw["peer_wqT"], xt)
    s1l, rw, s2l, b2 = _peer_route(qt, lw["peer_k1"], lw["peer_k2"])
    yt = _peer_dense(xt, lw["peer_u"], lw["peer_vT"], s1l, rw, s2l, b2)
    y = yt.T[:M]
    return _res_ln(x1, y, lw["ln2_g"], lw["ln2_b"], alpha)


Z_SEGS = ("a_q", "a_kv", "a_g", "b_uv", "c_q", "c_k", "c_v", "c_i", "c_f", "c_o", "mg")
GATE_COL = 0
CI_COL = 3 * NSA_HEADS
CF_COL = CI_COL + MLSTM_HEADS


def _prep_layer_weights(l, w):
    D = w["w_in"].shape[1]
    sizes = (NSA_HEADS * HEAD_DIM, 6 * NSA_KV * HEAD_DIM, 3 * NSA_HEADS, 2 * W_B,
             MLSTM_HEADS * MLSTM_DQK, MLSTM_HEADS * MLSTM_DQK, MLSTM_HEADS * MLSTM_DV, MLSTM_HEADS, MLSTM_HEADS,
             MLSTM_HEADS * MLSTM_DV, 3 * D)
    offs = np.concatenate([[0], np.cumsum(sizes)])
    w_in = w["w_in"][l]
    seg = {n: (int(offs[i]), int(offs[i + 1])) for i, n in enumerate(Z_SEGS)}
    cut = lambda n: w_in[:, seg[n][0]:seg[n][1]].astype(BF16)
    lw = {n: cut(n) for n in ("a_q", "a_kv", "b_uv", "c_q", "c_k", "c_v", "c_o", "mg")}
    small = jnp.concatenate([w_in[:, seg[n][0]:seg[n][1]] for n in ("a_g", "c_i", "c_f")], -1)
    lw["small"] = jnp.pad(small, ((0, 0), (0, 128 - small.shape[1]))).astype(BF16)
    for n in ("w_br_a", "w_br_b", "w_br_c", "w_out", "peer_u"):
        lw[n] = w[n][l].astype(BF16)
    lw["peer_wqT"] = w["peer_wq"][l].T.astype(BF16)
    lw["peer_vT"] = w["peer_v"][l].T.astype(BF16)
    for n in ("cmp_pe", "cmp_w1", "cmp_b1", "cmp_w2", "gmlp_ln_g", "gmlp_ln_b", "gmlp_ws", "gmlp_bs",
              "mlstm_b_i", "mlstm_b_f", "mlstm_norm_g", "ln1_g", "ln1_b", "peer_k1", "peer_k2", "ln2_g", "ln2_b"):
        lw[n] = w[n][l]
    return lw


def _gates_by_group(zsm):
    M = zsm.shape[0]
    gt = zsm[:, GATE_COL:GATE_COL + 3 * NSA_HEADS].reshape(M, NSA_KV, 3 * NSA_REP).transpose(1, 0, 2)
    return jnp.pad(gt, ((0, 0), (0, 0), (0, 128 - 3 * NSA_REP)))


def _layer(x, B, T, tv, lw, alpha, dec):
    M, D = x.shape
    xb = x.astype(BF16)
    z = {n: _mm(xb, lw[n]) for n in ("a_q", "a_kv", "b_uv", "c_q", "c_k", "c_v", "c_o", "mg", "small")}
    zsm = z["small"]
    q_start = 0 if dec is None else dec["q_start"]
    cos, sin = _rope_tables(q_start + jnp.arange(T))
    new_kv, win = _kvprep(z["a_kv"], cos, sin, T)
    gates4 = _gates_by_group(zsm)
    cmp_w = (lw["cmp_pe"], lw["cmp_w1"], lw["cmp_b1"], lw["cmp_w2"])
    if dec is None:
        n_pages = T // PAGE_SIZE
        ident = jnp.arange(B * n_pages, dtype=I32).reshape(B, n_pages)
        kc, vc = _compress(z["a_kv"].reshape(B * n_pages, PAGE_SIZE, -1), ident, *cmp_w)
        o_a = _nsa_prefill(z["a_q"], new_kv, win, kc, vc, gates4, cos, sin, B, T)
        lc_g, lc_m = CHUNK, math.gcd(T, MLSTM_CHUNK)
        C0 = jnp.zeros((B, MLSTM_HEADS, MLSTM_DV, MLSTM_DQK), F32)
        n0 = jnp.zeros((B, MLSTM_HEADS, MLSTM_DQK), F32)
        m0 = jnp.zeros((B, MLSTM_HEADS), F32)
        valid = lc_m
    else:
        cache_l, page_table = dec["cache"], dec["page_table"]
        n_pool = cache_l.shape[0]
        kc, vc = _compress(cache_l.reshape(n_pool, PAGE_SIZE, -1), page_table, *cmp_w)
        L = q_start + tv
        n_cmp = (L - CMP_BLOCK) // CMP_STRIDE + 1
        n_slc = -(-L // SLC_BLOCK)
        assert n_cmp <= q_start // CMP_STRIDE - 1 and n_slc == q_start // SLC_BLOCK + 1
        assert q_start % SLC_BLOCK == 0 and tv <= T
        zq3 = z["a_q"].reshape(B, T, -1)
        ocmp, sel = _nsa_dec_sel(zq3, kc, vc, n_cmp, n_slc, q_start)
        sel = sel[:, :, :tv, :min(TOP_BLOCKS, n_slc)]
        o_a = _nsa_dec_attn(page_table, sel, zq3, cos, sin, cache_l, new_kv.reshape(B, T, -1), dec["win"],
                            win.reshape(B, T, -1), ocmp, gates4, tv, q_start).reshape(M, -1)
        lc_g = lc_m = T
        C0, n0, m0 = dec["C"], dec["n"], dec["m"]
        valid = tv
    o_b, v_rows = _gmlp(z["b_uv"], lw["gmlp_ln_g"], lw["gmlp_ln_b"], lw["gmlp_ws"], lw["gmlp_bs"], lc_g)
    o_c, C, n, m = _mlstm(z["c_q"], z["c_k"], z["c_v"], z["c_o"], zsm, lw["mlstm_b_i"], lw["mlstm_b_f"],
                          lw["mlstm_norm_g"], C0, n0, m0, B, T, lc_m, valid, CI_COL, CF_COL)
    merged = _merge(o_a, o_b, o_c, lw["w_br_a"], lw["w_br_b"], lw["w_br_c"], z["mg"])
    x1, x1b = _mm_res_ln(merged, lw["w_out"], x, lw["ln1_g"], lw["ln1_b"], alpha)
    y, _ = _peer(x1, x1b, lw, alpha)
    return y, new_kv, win, v_rows, C, n, m


def kernel(x_prompt, x_sample, cache_nsa_kv, state_nsa_win, state_mlstm_C, state_mlstm_n, state_mlstm_m,
           page_table, w_in, cmp_pe, cmp_w1, cmp_b1, cmp_w2, gmlp_ln_g, gmlp_ln_b, gmlp_ws, gmlp_bs,
           mlstm_b_i, mlstm_b_f, mlstm_norm_g, w_br_a, w_br_b, w_br_c, w_out, ln1_g, ln1_b,
           peer_wq, peer_k1, peer_k2, peer_u, peer_v, ln2_g, ln2_b):
    w = dict(w_in=w_in, cmp_pe=cmp_pe, cmp_w1=cmp_w1, cmp_b1=cmp_b1, cmp_w2=cmp_w2, gmlp_ln_g=gmlp_ln_g,
             gmlp_ln_b=gmlp_ln_b, gmlp_ws=gmlp_ws, gmlp_bs=gmlp_bs, mlstm_b_i=mlstm_b_i, mlstm_b_f=mlstm_b_f,
             mlstm_norm_g=mlstm_norm_g, w_br_a=w_br_a, w_br_b=w_br_b, w_br_c=w_br_c, w_out=w_out, ln1_g=ln1_g,
             ln1_b=ln1_b, peer_wq=peer_wq, peer_k1=peer_k1, peer_k2=peer_k2, peer_u=peer_u, peer_v=peer_v,
             ln2_g=ln2_g, ln2_b=ln2_b)
    depth = w_in.shape[0]
    alpha = (2 * depth) ** 0.25
    Bp, Tp, D = x_prompt.shape
    Bs, Ts, _ = x_sample.shape
    Tpad = 8
    past_len = page_table.shape[1] * PAGE_SIZE
    gw = NSA_KV * HEAD_DIM
    yp = x_prompt.reshape(Bp * Tp, D)
    ys = jnp.pad(x_sample, ((0, 0), (0, Tpad - Ts), (0, 0))).reshape(Bs * Tpad, D)
    outs = [[] for _ in range(11)]
    wkeep_p = min(WINDOW, Tp)
    P = state_nsa_win.shape[2]
    for l in range(depth):
        lw = _prep_layer_weights(l, w)
        yp, kv, win, _, C, n, m = _layer(yp, Bp, Tp, Tp, lw, alpha, None)
        outs[0].append(kv.reshape(Bp, Tp, 4, NSA_KV, HEAD_DIM))
        outs[2].append(win.reshape(Bp, Tp, 2, NSA_KV, HEAD_DIM)[:, Tp - wkeep_p:])
        outs[5].append(C); outs[6].append(n); outs[7].append(m)
        dec = dict(q_start=past_len, cache=cache_nsa_kv[l], page_table=page_table, win=state_nsa_win[l],
                   C=state_mlstm_C[l], n=state_mlstm_n[l], m=state_mlstm_m[l])
        ys, kv, win, v_rows, C, n, m = _layer(ys, Bs, Tpad, Ts, lw, alpha, dec)
        outs[1].append(kv.reshape(Bs, Tpad, 4, NSA_KV, HEAD_DIM)[:, :Ts])
        win_new = win.reshape(Bs, Tpad, 2, NSA_KV, HEAD_DIM)[:, :Ts]
        outs[3].append(jnp.concatenate([state_nsa_win[l], win_new], 1)[:, Ts:])
        outs[4].append(v_rows.reshape(Bs, Tpad, W_B)[:, :Ts])
        outs[8].append(C); outs[9].append(n); outs[10].append(m)
    y_prompt = yp.reshape(Bp, Tp, D)
    y_sample = ys.reshape(Bs, Tpad, D)[:, :Ts]
    st = [jnp.stack(o) for o in outs]
    return (y_prompt, y_sample, st[0], st[1], st[2], st[3], st[4], st[5], st[6], st[7], st[8], st[9], st[10])
```

```python
import functools
import math

import numpy as np
import jax
import jax.numpy as jnp
from jax import lax
from jax.experimental import pallas as pl
from jax.experimental.pallas import tpu as pltpu

F32 = jnp.float32
BF16 = jnp.bfloat16
I32 = jnp.int32

HEAD_DIM = 128
ROT_DIM = HEAD_DIM // 4
ROPE_THETA = 500000.0
NSA_HEADS = 16
NSA_KV = 4
NSA_REP = NSA_HEADS // NSA_KV
CMP_BLOCK = 32
CMP_STRIDE = 16
SLC_BLOCK = 64
SLC_SHIFT = 6
TOP_BLOCKS = 16
WINDOW = 512
FORCE_SCORE = 1.0e9
CHUNK = 128
GMLP_GROUPS = 16
GMLP_GDIM = 128
W_B = GMLP_GROUPS * GMLP_GDIM
MLSTM_HEADS = 8
MLSTM_DQK = 128
MLSTM_DV = 256
MLSTM_CHUNK = 64
PEER_HEADS = 8
PEER_DQ = 256
N_KEYS = 128
PEER_TOPK = 16
PAGE_SIZE = 128
LOG2E = 1.4426950408889634
NEG_INF = float("-inf")

VMEM_LIMIT_BYTES = 56 * 1024 * 1024


def _cp(*sem):
    return pltpu.CompilerParams(dimension_semantics=sem, vmem_limit_bytes=VMEM_LIMIT_BYTES)


def _gelu(x):
    return 0.5 * x * (1.0 + jnp.tanh(0.7978845608028654 * (x + 0.044715 * (x * x * x))))


def _sigmoid(x):
    return 1.0 / (1.0 + jnp.exp(-x))


def _log_sigmoid(x):
    return jnp.minimum(x, 0.0) - jnp.log(1.0 + jnp.exp(-jnp.abs(x)))


def _dot(a, b):
    return jnp.dot(a, b, preferred_element_type=F32)


def _dot_nt(a, b):
    return lax.dot_general(a, b, (((1,), (1,)), ((), ())), preferred_element_type=F32)


def _dot_tn(a, b):
    return lax.dot_general(a, b, (((0,), (0,)), ((), ())), preferred_element_type=F32)


def _masked_softmax(lg, mask):
    lg = jnp.where(mask, lg, NEG_INF)
    mx = jnp.max(lg, -1, keepdims=True)
    mx = jnp.where(mx > NEG_INF, mx, 0.0)
    e = jnp.exp(lg - mx)
    return e / jnp.maximum(jnp.sum(e, -1, keepdims=True), 1e-30)


def _split3_dot(p, m_bf16):
    hi = p.astype(BF16)
    r1 = p - hi.astype(F32)
    mid = r1.astype(BF16)
    lo = (r1 - mid.astype(F32)).astype(BF16)
    return _dot(hi, m_bf16) + _dot(mid, m_bf16) + _dot(lo, m_bf16)


def _rope(x, cos, sin):
    lane = lax.broadcasted_iota(I32, x.shape, 1)
    up = pltpu.roll(x, HEAD_DIM - ROT_DIM // 2, 1)
    dn = pltpu.roll(x, ROT_DIM // 2, 1)
    return x * cos + jnp.where(lane < ROT_DIM // 2, up, dn) * sin


def _rope_tables(pos):
    inv = ROPE_THETA ** (-jnp.arange(0, ROT_DIM, 2, dtype=F32) / ROT_DIM)
    ang = pos.astype(F32)[:, None] * inv[None, :]
    c, s = jnp.cos(ang), jnp.sin(ang)
    n = pos.shape[0]
    cos = jnp.concatenate([c, c, jnp.ones((n, HEAD_DIM - ROT_DIM), F32)], -1)
    sin = jnp.concatenate([-s, s, jnp.zeros((n, HEAD_DIM - ROT_DIM), F32)], -1)
    return cos, sin


def _mm_kernel(x_ref, w_ref, o_ref, acc_ref):
    k = pl.program_id(2)

    @pl.when(k == 0)
    def _():
        acc_ref[...] = jnp.zeros_like(acc_ref)

    acc_ref[...] += _dot(x_ref[...], w_ref[...])

    @pl.when(k == pl.num_programs(2) - 1)
    def _():
        o_ref[...] = acc_ref[...].astype(o_ref.dtype)


def _mm(x, w, out_dtype=F32, tm=1024, tn=512, tk=4096):
    M, K = x.shape
    N = w.shape[1]
    tm, tn, tk = min(tm, M), min(tn, N), min(tk, K)
    assert M % tm == 0 and N % tn == 0 and K % tk == 0, (x.shape, w.shape)
    return pl.pallas_call(
        _mm_kernel,
        out_shape=jax.ShapeDtypeStruct((M, N), out_dtype),
        grid=(M // tm, N // tn, K // tk),
        in_specs=[pl.BlockSpec((tm, tk), lambda i, j, k: (i, k)),
                  pl.BlockSpec((tk, tn), lambda i, j, k: (k, j))],
        out_specs=pl.BlockSpec((tm, tn), lambda i, j, k: (i, j)),
        scratch_shapes=[pltpu.VMEM((tm, tn), F32)],
        compiler_params=_cp("parallel", "parallel", "arbitrary"),
        name="mm",
    )(x, w)


def _res_ln_kernel(x_ref, y_ref, g_ref, b_ref, o_ref, ob_ref, *, alpha):
    v = alpha * x_ref[...] + y_ref[...]
    mu = jnp.mean(v, -1, keepdims=True)
    d = v - mu
    var = jnp.mean(d * d, -1, keepdims=True)
    o = d * lax.rsqrt(var + 1e-5) * g_ref[...] + b_ref[...]
    o_ref[...] = o
    ob_ref[...] = o.astype(BF16)


def _res_ln(x, y, g, b, alpha, tm=256):
    M, D = x.shape
    tm = min(tm, M)
    row = pl.BlockSpec((tm, D), lambda i: (i, 0))
    vec = pl.BlockSpec((1, D), lambda i: (0, 0))
    return pl.pallas_call(
        functools.partial(_res_ln_kernel, alpha=alpha),
        out_shape=(jax.ShapeDtypeStruct((M, D), F32), jax.ShapeDtypeStruct((M, D), BF16)),
        grid=(M // tm,),
        in_specs=[row, row, vec, vec],
        out_specs=(row, row),
        compiler_params=_cp("parallel"),
        name="res_ln",
    )(x, y, g.reshape(1, D), b.reshape(1, D))


def _kvprep_kernel(z_ref, cos_ref, sin_ref, kv_ref, win_ref):
    cos, sin = cos_ref[...], sin_ref[...]
    gw = NSA_KV * HEAD_DIM
    kv_ref[:, 0:2 * gw] = z_ref[:, 0:2 * gw]
    kv_ref[:, 3 * gw:4 * gw] = z_ref[:, 3 * gw:4 * gw]
    win_ref[:, gw:2 * gw] = z_ref[:, 5 * gw:6 * gw]
    for g in range(NSA_KV):
        a = 2 * gw + g * HEAD_DIM
        kv_ref[:, a:a + HEAD_DIM] = _rope(z_ref[:, a:a + HEAD_DIM], cos, sin)
        a = 4 * gw + g * HEAD_DIM
        win_ref[:, g * HEAD_DIM:(g + 1) * HEAD_DIM] = _rope(z_ref[:, a:a + HEAD_DIM], cos, sin)


def _kvprep(zkv, cos, sin, T, tm=256):
    M = zkv.shape[0]
    tm = min(tm, T)
    nt = T // tm
    gw = NSA_KV * HEAD_DIM
    return pl.pallas_call(
        _kvprep_kernel,
        out_shape=(jax.ShapeDtypeStruct((M, 4 * gw), F32), jax.ShapeDtypeStruct((M, 2 * gw), F32)),
        grid=(M // tm,),
        in_specs=[pl.BlockSpec((tm, 6 * gw), lambda i: (i, 0)),
                  pl.BlockSpec((tm, HEAD_DIM), lambda i: (i % nt, 0)),
                  pl.BlockSpec((tm, HEAD_DIM), lambda i: (i % nt, 0))],
        out_specs=(pl.BlockSpec((tm, 4 * gw), lambda i: (i, 0)),
                   pl.BlockSpec((tm, 2 * gw), lambda i: (i, 0))),
        compiler_params=_cp("parallel"),
        name="kvprep",
    )(zkv, cos, sin)


CMP_PAGES = 8


def _cmp_bias_kernel(pe_ref, w1f_ref, b1_ref, o_ref):
    for kv in range(2):
        b = b1_ref[kv:kv + 1, :]
        for half in range(2):
            b = b + _dot(pe_ref[kv, half].astype(BF16), w1f_ref[kv, half])[0:1, :]
        o_ref[:, kv * HEAD_DIM:(kv + 1) * HEAD_DIM] = jnp.broadcast_to(b, (8, HEAD_DIM))


def _cmp_kernel(pt_ref, *refs, P):
    page_refs = refs[:P + 1]
    wa_ref, wb_ref, bias_ref, w2_ref, o_ref, res_s = refs[P + 1:]
    hp = PAGE_SIZE // CMP_STRIDE
    sg = 2 * NSA_KV
    rows = P * hp * sg
    acc_a = jnp.zeros((rows, 2 * HEAD_DIM), F32)
    acc_b = jnp.zeros((rows, 2 * HEAD_DIM), F32)
    for j in range(CMP_STRIDE):
        la, lb = [], []
        for p in range(P):
            la.append(page_refs[p][pl.ds(j, hp, stride=CMP_STRIDE), 0:sg, :].reshape(hp * sg, HEAD_DIM))
            lb.append(page_refs[p][pl.ds(j + CMP_STRIDE, hp - 1, stride=CMP_STRIDE), 0:sg, :]
                      .reshape((hp - 1) * sg, HEAD_DIM))
            lb.append(page_refs[p + 1][j, 0:sg, :])
        acc_a = acc_a + _dot(jnp.concatenate(la, axis=0).astype(BF16), wa_ref[j])
        acc_b = acc_b + _dot(jnp.concatenate(lb, axis=0).astype(BF16), wb_ref[j])
    h = _gelu(acc_a + acc_b + bias_ref[0:1, :])
    ok = _dot(h[:, :HEAD_DIM].astype(BF16), w2_ref[0].astype(BF16))
    ov = _dot(h[:, HEAD_DIM:].astype(BF16), w2_ref[1].astype(BF16))
    is_k = (lax.broadcasted_iota(I32, (rows, 1), 0) & (sg - 1)) < NSA_KV
    res_s[...] = jnp.where(is_k, ok, ov)
    for s in range(sg):
        o_ref[0, s] = res_s[pl.ds(s, P * hp, stride=sg), :]


def _compress(src3, page_table, page_off, cmp_pe, cmp_w1, cmp_b1, cmp_w2):
    B, n_pages = page_table.shape
    P = math.gcd(CMP_PAGES, n_pages)
    hp = PAGE_SIZE // CMP_STRIDE
    SG = src3.shape[1]
    sg = 2 * NSA_KV
    pe_flat = jnp.broadcast_to(cmp_pe.reshape(2, 2, 1, CMP_STRIDE * HEAD_DIM), (2, 2, 8, CMP_STRIDE * HEAD_DIM))
    w1_flat = cmp_w1.reshape(2, 2, CMP_STRIDE * HEAD_DIM, HEAD_DIM).astype(BF16)
    bias = pl.pallas_call(
        _cmp_bias_kernel,
        out_shape=jax.ShapeDtypeStruct((8, 2 * HEAD_DIM), F32),
        name="cmp_bias",
    )(pe_flat, w1_flat, cmp_b1)
    wa = jnp.concatenate([cmp_w1[0, :CMP_STRIDE], cmp_w1[1, :CMP_STRIDE]], -1).astype(BF16)
    wb = jnp.concatenate([cmp_w1[0, CMP_STRIDE:], cmp_w1[1, CMP_STRIDE:]], -1).astype(BF16)

    def page_spec(p):
        return pl.BlockSpec((PAGE_SIZE, SG, HEAD_DIM),
                            lambda b, c, pt: (page_off + pt[b, jnp.minimum(c * P + p, n_pages - 1)], 0, 0))

    full = lambda shape: pl.BlockSpec(shape, lambda b, c, pt: (0,) * len(shape))
    return pl.pallas_call(
        functools.partial(_cmp_kernel, P=P),
        out_shape=jax.ShapeDtypeStruct((B, sg, n_pages * hp, HEAD_DIM), F32),
        grid_spec=pltpu.PrefetchScalarGridSpec(
            num_scalar_prefetch=1,
            grid=(B, n_pages // P),
            in_specs=[page_spec(p) for p in range(P + 1)]
            + [full(wa.shape), full(wb.shape), full(bias.shape), full(cmp_w2.shape)],
            out_specs=pl.BlockSpec((1, sg, P * hp, HEAD_DIM), lambda b, c, pt: (b, 0, c, 0)),
            scratch_shapes=[pltpu.VMEM((P * hp * sg, HEAD_DIM), F32)],
        ),
        compiler_params=_cp("parallel", "parallel"),
        name="cmp",
    )(page_table, *([src3] * (P + 1)), wa, wb, bias, cmp_w2)


def _cmp_to_slc_mask(ncp, nsp, n_cmp, n_slc):
    ci = lax.broadcasted_iota(I32, (ncp, nsp), 0)
    si = lax.broadcasted_iota(I32, (ncp, nsp), 1)
    ov = ((ci * CMP_STRIDE < (si + 1) * SLC_BLOCK) & (ci * CMP_STRIDE + CMP_BLOCK > si * SLC_BLOCK)
          & (ci < n_cmp) & (si < n_slc))
    return jnp.where(ov, 1.0, 0.0).astype(BF16)


def _nsa_prefill_kernel(q_ref, cos_ref, sin_ref, kc_ref, vc_ref, ks_ref, vs_ref, kw_ref, vw_ref, gt_ref, o_ref,
                        *, tq, T, n_cmp, n_slc, wlen):
    i = pl.program_id(2)
    t0 = i * tq
    scale = HEAD_DIM ** -0.5
    rep = NSA_REP
    q4 = q_ref[...]
    Q = jnp.concatenate([q4[:, r * HEAD_DIM:(r + 1) * HEAD_DIM] for r in range(rep)], axis=0)
    cos = jnp.concatenate([cos_ref[...]] * rep, axis=0)
    sin = jnp.concatenate([sin_ref[...]] * rep, axis=0)
    Qb = Q.astype(BF16)
    Qrb = _rope(Q, cos, sin).astype(BF16)
    qpos1 = t0 + lax.broadcasted_iota(I32, (tq, 1), 0)
    qpos = jnp.concatenate([qpos1] * rep, axis=0)
    ncp = kc_ref.shape[2]
    kc = kc_ref[0, 0].astype(BF16)
    vc = vc_ref[0, 0].astype(BF16)
    lg = _dot_nt(Qb, kc) * scale
    cidx = lax.broadcasted_iota(I32, (1, ncp), 1)
    cmask = (cidx * CMP_STRIDE + CMP_BLOCK - 1 <= qpos) & (cidx < n_cmp)
    p = _masked_softmax(lg, cmask)
    o_cmp = _dot(p.astype(BF16), vc)
    psum = p[0:tq]
    for r in range(1, rep):
        psum = psum + p[r * tq:(r + 1) * tq]
    nsp = 128
    imp = _split3_dot(psum, _cmp_to_slc_mask(ncp, nsp, n_cmp, n_slc))
    blk = lax.broadcasted_iota(I32, (1, nsp), 1)
    forced = (blk == 0) | (blk == jnp.right_shift(qpos1, SLC_SHIFT))
    future = blk * SLC_BLOCK > qpos1
    imp = jnp.where(forced, FORCE_SCORE, jnp.where(future, NEG_INF, imp))
    imp = jnp.where(blk < n_slc, imp, NEG_INF)
    rank = jnp.zeros((tq, nsp), F32)
    for s2 in range(n_slc):
        col = imp[:, s2:s2 + 1]
        beats = (col > imp) | ((col == imp) & (blk > s2))
        rank = rank + jnp.where(beats, 1.0, 0.0)
    sel = (rank < float(min(TOP_BLOCKS, n_slc))) & (blk < n_slc)
    srow = lax.broadcasted_iota(I32, (nsp, T), 0)
    kcol = lax.broadcasted_iota(I32, (nsp, T), 1)
    expand = jnp.where(jnp.right_shift(kcol, SLC_SHIFT) == srow, 1.0, 0.0).astype(BF16)
    selk1 = _dot(jnp.where(sel, 1.0, 0.0).astype(BF16), expand)
    selk = jnp.concatenate([selk1] * rep, axis=0)
    kpos = lax.broadcasted_iota(I32, (1, T), 1)
    smask = (selk > 0.5) & (kpos <= qpos)
    ls = _dot_nt(Qrb, ks_ref[...].astype(BF16)) * scale
    ps = _masked_softmax(ls, smask)
    o_s = _dot(ps.astype(BF16), vs_ref[...].astype(BF16))
    start = pl.multiple_of(jnp.clip(t0 - WINDOW, 0, T - wlen), 128)
    kw = kw_ref[pl.ds(start, wlen), :].astype(BF16)
    vw = vw_ref[pl.ds(start, wlen), :].astype(BF16)
    dpos = qpos - (start + lax.broadcasted_iota(I32, (1, wlen), 1))
    lw = _dot_nt(Qrb, kw) * scale
    pw = _masked_softmax(lw, (dpos >= 0) & (dpos <= WINDOW))
    o_w = _dot(pw.astype(BF16), vw)
    gt = _sigmoid(gt_ref[0])
    for r in range(rep):
        rows = slice(r * tq, (r + 1) * tq)
        o = (gt[:, 3 * r:3 * r + 1] * o_cmp[rows] + gt[:, 3 * r + 1:3 * r + 2] * o_s[rows]
             + gt[:, 3 * r + 2:3 * r + 3] * o_w[rows])
        o_ref[:, r * HEAD_DIM:(r + 1) * HEAD_DIM] = o.astype(o_ref.dtype)


def _nsa_prefill(zq, new_kv, win, kvc, gates4, cos, sin, B, T, tq=128):
    tq = min(tq, T)
    nt = T // tq
    n_cmp = (T - CMP_BLOCK) // CMP_STRIDE + 1
    n_slc = -(-T // SLC_BLOCK)
    assert n_slc <= 128 and T % tq == 0
    wlen = min(T, WINDOW + tq)
    ncp = kvc.shape[2]
    gw = NSA_KV
    kern = functools.partial(_nsa_prefill_kernel, tq=tq, T=T, n_cmp=n_cmp, n_slc=n_slc, wlen=wlen)
    return pl.pallas_call(
        kern,
        out_shape=jax.ShapeDtypeStruct((B * T, NSA_HEADS * HEAD_DIM), BF16),
        grid=(B, NSA_KV, nt),
        in_specs=[pl.BlockSpec((tq, NSA_REP * HEAD_DIM), lambda b, g, i: (b * nt + i, g)),
                  pl.BlockSpec((tq, HEAD_DIM), lambda b, g, i: (i, 0)),
                  pl.BlockSpec((tq, HEAD_DIM), lambda b, g, i: (i, 0)),
                  pl.BlockSpec((1, 1, ncp, HEAD_DIM), lambda b, g, i: (b, g, 0, 0)),
                  pl.BlockSpec((1, 1, ncp, HEAD_DIM), lambda b, g, i: (b, gw + g, 0, 0)),
                  pl.BlockSpec((T, HEAD_DIM), lambda b, g, i: (b, 2 * gw + g)),
                  pl.BlockSpec((T, HEAD_DIM), lambda b, g, i: (b, 3 * gw + g)),
                  pl.BlockSpec((T, HEAD_DIM), lambda b, g, i: (b, g)),
                  pl.BlockSpec((T, HEAD_DIM), lambda b, g, i: (b, gw + g)),
                  pl.BlockSpec((1, tq, 128), lambda b, g, i: (g, b * nt + i, 0))],
        out_specs=pl.BlockSpec((tq, NSA_REP * HEAD_DIM), lambda b, g, i: (b * nt + i, g)),
        compiler_params=_cp("parallel", "parallel", "parallel"),
        name="nsa_prefill",
    )(zq, cos, sin, kvc, kvc, new_kv, new_kv, win, win, gates4)


def _nsa_dec_sel_kernel(q_ref, kc_ref, vc_ref, ocmp_ref, sel_ref, *, tp, n_cmp, n_slc, q_start):
    scale = HEAD_DIM ** -0.5
    rep = NSA_REP
    ncp = kc_ref.shape[2]
    nsp = -(-n_slc // 128) * 128
    qpos1 = q_start + lax.broadcasted_iota(I32, (tp, 1), 0)
    qpos = jnp.concatenate([qpos1] * rep, axis=0)
    cidx = lax.broadcasted_iota(I32, (1, ncp), 1)
    cmask = (cidx * CMP_STRIDE + CMP_BLOCK - 1 <= qpos) & (cidx < n_cmp)
    mm = _cmp_to_slc_mask(ncp, nsp, n_cmp, n_slc)
    blk = lax.broadcasted_iota(I32, (1, nsp), 1)
    lane16 = lax.broadcasted_iota(I32, (tp, 128), 1)
    for g in range(NSA_KV):
        c0 = g * rep * HEAD_DIM
        Q = jnp.concatenate([q_ref[0, :, c0 + r * HEAD_DIM:c0 + (r + 1) * HEAD_DIM] for r in range(rep)], axis=0)
        lg = _dot_nt(Q.astype(BF16), kc_ref[0, g].astype(BF16)) * scale
        p = _masked_softmax(lg, cmask)
        ocmp_ref[0, g] = _dot(p.astype(BF16), vc_ref[0, g].astype(BF16))
        psum = p[0:tp]
        for r in range(1, rep):
            psum = psum + p[r * tp:(r + 1) * tp]
        imp = _split3_dot(psum, mm)
        forced = (blk == 0) | (blk == jnp.right_shift(qpos1, SLC_SHIFT))
        future = blk * SLC_BLOCK > qpos1
        imp = jnp.where(forced, FORCE_SCORE, jnp.where(future, NEG_INF, imp))
        taken = jnp.broadcast_to(blk >= n_slc, (tp, nsp))
        sel = jnp.zeros((tp, 128), I32)
        for it in range(min(TOP_BLOCKS, n_slc)):
            cand = jnp.where(taken, NEG_INF, imp)
            mx = jnp.max(cand, -1, keepdims=True)
            hit = (cand == mx) & jnp.logical_not(taken)
            idx = jnp.min(jnp.where(hit, blk, nsp), -1, keepdims=True)
            taken = taken | (blk == idx)
            sel = jnp.where(lane16 == it, idx, sel)
        sel_ref[0, g] = sel


def _nsa_dec_sel(zq3, kvc, n_cmp, n_slc, q_start):
    B, tp, _ = zq3.shape
    ncp = kvc.shape[2]
    kern = functools.partial(_nsa_dec_sel_kernel, tp=tp, n_cmp=n_cmp, n_slc=n_slc, q_start=q_start)
    return pl.pallas_call(
        kern,
        out_shape=(jax.ShapeDtypeStruct((B, NSA_KV, NSA_REP * tp, HEAD_DIM), F32),
                   jax.ShapeDtypeStruct((B, NSA_KV, tp, 128), I32)),
        grid=(B,),
        in_specs=[pl.BlockSpec((1, tp, NSA_HEADS * HEAD_DIM), lambda b: (b, 0, 0)),
                  pl.BlockSpec((1, NSA_KV, ncp, HEAD_DIM), lambda b: (b, 0, 0, 0)),
                  pl.BlockSpec((1, NSA_KV, ncp, HEAD_DIM), lambda b: (b, 1, 0, 0))],
        out_specs=(pl.BlockSpec((1, NSA_KV, NSA_REP * tp, HEAD_DIM), lambda b: (b, 0, 0, 0)),
                   pl.BlockSpec((1, NSA_KV, tp, 128), lambda b: (b, 0, 0, 0))),
        compiler_params=_cp("parallel"),
        name="nsa_dec_sel",
    )(zq3, kvc, kvc)


def _nsa_dec_attn_kernel(pt_ref, sel_ref, q_ref, cos_ref, sin_ref, *refs, tp, tv, n_past_blk, q_start, ktop):
    nb = NSA_KV * tv
    blk_refs = refs[:nb]
    nkv_ref, wst_ref, win_ref, ocmp_ref, gt_ref, o_ref, qr_s, m_s, l_s, acc_s = refs[nb:]
    b, k = pl.program_id(0), pl.program_id(1)
    rep = NSA_REP
    scale = HEAD_DIM ** -0.5
    rows = rep * tp
    gw = NSA_KV * HEAD_DIM
    trow = lax.broadcasted_iota(I32, (rows, 1), 0) & (tp - 1)
    qpos = q_start + trow

    @pl.when(k == 0)
    def _():
        cos = jnp.concatenate([cos_ref[...]] * rep, axis=0)
        sin = jnp.concatenate([sin_ref[...]] * rep, axis=0)
        for g in range(NSA_KV):
            c0 = g * rep * HEAD_DIM
            Q = jnp.concatenate([q_ref[0, :, c0 + r * HEAD_DIM:c0 + (r + 1) * HEAD_DIM] for r in range(rep)], axis=0)
            qr_s[g] = _rope(Q, cos, sin)
        m_s[...] = jnp.full_like(m_s, NEG_INF)
        l_s[...] = jnp.zeros_like(l_s)
        acc_s[...] = jnp.zeros_like(acc_s)

    def online_update(g, lg, mask, v):
        lg = jnp.where(mask, lg, NEG_INF)
        m_old = m_s[g]
        m_new = jnp.maximum(m_old, jnp.max(lg, -1, keepdims=True))
        m_safe = jnp.where(m_new > NEG_INF, m_new, 0.0)
        a = jnp.exp(m_old - m_safe)
        e = jnp.exp(lg - m_safe)
        l_s[g] = a * l_s[g] + jnp.sum(e, -1, keepdims=True)
        acc_s[g] = a * acc_s[g] + _dot(e.astype(BF16), v)
        m_s[g] = m_new

    col = lax.broadcasted_iota(I32, (1, tv * SLC_BLOCK), 1)
    cblk = jnp.right_shift(col, SLC_SHIFT)
    far = 1 << 24
    for g in range(NSA_KV):
        Qrb = qr_s[g].astype(BF16)
        kb = jnp.concatenate([blk_refs[g * tv + t][:, g, :].astype(BF16) for t in range(tv)], axis=0)
        vb = jnp.concatenate([blk_refs[g * tv + t][:, NSA_KV + g, :].astype(BF16) for t in range(tv)], axis=0)
        lg = _dot_nt(Qrb, kb) * scale
        mask = None
        for t in range(tv):
            s = sel_ref[b, g, t, k]
            s = jnp.where(s < n_past_blk, s, far)
            kpos = s * SLC_BLOCK + (col - t * SLC_BLOCK)
            mt = (cblk == t) & (trow == t) & (kpos <= qpos)
            mask = mt if mask is None else (mask | mt)
        online_update(g, lg, mask, vb)

    @pl.when(k == ktop - 1)
    def _():
        jrow = lax.broadcasted_iota(I32, (1, tp), 1)
        P = wst_ref.shape[0]
        jw = lax.broadcasted_iota(I32, (1, P), 1)
        dpos_w = (qpos - (q_start - P)) - jw
        mask_w = (dpos_w >= 0) & (dpos_w <= WINDOW) & (q_start - P + jw >= 0)
        mask_n = (jrow <= trow) & (jrow < tv)
        for g in range(NSA_KV):
            Qrb = qr_s[g].astype(BF16)
            has_new = None
            for t in range(tv):
                f = sel_ref[b, g, t, 0] == n_past_blk
                for kk in range(1, ktop):
                    f = f | (sel_ref[b, g, t, kk] == n_past_blk)
                hn = trow == jnp.where(f, t, -1)
                has_new = hn if has_new is None else (has_new | hn)
            c_k = 2 * gw + g * HEAD_DIM
            c_v = 3 * gw + g * HEAD_DIM
            nk = nkv_ref[0, :, c_k:c_k + HEAD_DIM].astype(BF16)
            nv = nkv_ref[0, :, c_v:c_v + HEAD_DIM].astype(BF16)
            online_update(g, _dot_nt(Qrb, nk) * scale, has_new & mask_n, nv)
            o_s = acc_s[g] / jnp.maximum(l_s[g], 1e-30)
            wk = wst_ref[:, g, :].astype(BF16)
            wv = wst_ref[:, NSA_KV + g, :].astype(BF16)
            nwk = win_ref[0, :, g * HEAD_DIM:(g + 1) * HEAD_DIM].astype(BF16)
            nwv = win_ref[0, :, gw + g * HEAD_DIM:gw + (g + 1) * HEAD_DIM].astype(BF16)
            lw = jnp.where(mask_w, _dot_nt(Qrb, wk) * scale, NEG_INF)
            ln = jnp.where(mask_n, _dot_nt(Qrb, nwk) * scale, NEG_INF)
            mx = jnp.maximum(jnp.max(lw, -1, keepdims=True), jnp.max(ln, -1, keepdims=True))
            mx = jnp.where(mx > NEG_INF, mx, 0.0)
            ew, en = jnp.exp(lw - mx), jnp.exp(ln - mx)
            den = jnp.maximum(jnp.sum(ew, -1, keepdims=True) + jnp.sum(en, -1, keepdims=True), 1e-30)
            o_w = (_dot(ew.astype(BF16), wv) + _dot(en.astype(BF16), nwv)) / den
            gt = _sigmoid(gt_ref[g])
            o_c = ocmp_ref[0, g]
            for r in range(rep):
                rs = slice(r * tp, (r + 1) * tp)
                o = (gt[:, 3 * r:3 * r + 1] * o_c[rs] + gt[:, 3 * r + 1:3 * r + 2] * o_s[rs]
                     + gt[:, 3 * r + 2:3 * r + 3] * o_w[rs])
                c_o = (g * rep + r) * HEAD_DIM
                o_ref[0, :, c_o:c_o + HEAD_DIM] = o.astype(o_ref.dtype)


def _nsa_dec_attn(page_table, sel, zq3, cos, sin, cache4, page_off, new_kv3, wst3, win_off, win3, ocmp, gates4,
                  tv, q_start):
    B, tp, _ = zq3.shape
    bpp = PAGE_SIZE // SLC_BLOCK
    n_past_blk = page_table.shape[1] * bpp
    ktop = sel.shape[-1]
    P = wst3.shape[0] // (win_off[1])
    rows = NSA_REP * tp

    def cache_spec(g, t):
        def imap(b, k, pt, sl):
            s = jnp.minimum(sl[b, g, t, k], n_past_blk - 1)
            return ((page_off + pt[b, s // bpp]) * bpp + s % bpp, 1, 0, 0)
        return pl.BlockSpec((SLC_BLOCK, None, 2 * NSA_KV, HEAD_DIM), imap)

    cache_specs = [cache_spec(g, t) for g in range(NSA_KV) for t in range(tv)]
    kern = functools.partial(_nsa_dec_attn_kernel, tp=tp, tv=tv, n_past_blk=n_past_blk, q_start=q_start, ktop=ktop)
    whole = lambda w: pl.BlockSpec((1, tp, w), lambda b, k, pt, sl: (b, 0, 0))
    return pl.pallas_call(
        kern,
        out_shape=jax.ShapeDtypeStruct((B, tp, NSA_HEADS * HEAD_DIM), BF16),
        grid_spec=pltpu.PrefetchScalarGridSpec(
            num_scalar_prefetch=2,
            grid=(B, ktop),
            in_specs=[whole(NSA_HEADS * HEAD_DIM),
                      pl.BlockSpec((tp, HEAD_DIM), lambda b, k, pt, sl: (0, 0)),
                      pl.BlockSpec((tp, HEAD_DIM), lambda b, k, pt, sl: (0, 0))]
            + cache_specs
            + [whole(4 * NSA_KV * HEAD_DIM),
               pl.BlockSpec((P, 2 * NSA_KV, HEAD_DIM), lambda b, k, pt, sl: (win_off[0] + b, 0, 0)),
               whole(2 * NSA_KV * HEAD_DIM),
               pl.BlockSpec((1, NSA_KV, rows, HEAD_DIM), lambda b, k, pt, sl: (b, 0, 0, 0)),
               pl.BlockSpec((NSA_KV, tp, 128), lambda b, k, pt, sl: (0, b, 0))],
            out_specs=whole(NSA_HEADS * HEAD_DIM),
            scratch_shapes=[pltpu.VMEM((NSA_KV, rows, HEAD_DIM), F32), pltpu.VMEM((NSA_KV, rows, 1), F32),
                            pltpu.VMEM((NSA_KV, rows, 1), F32), pltpu.VMEM((NSA_KV, rows, HEAD_DIM), F32)],
        ),
        compiler_params=_cp("parallel", "arbitrary"),
        name="nsa_dec_attn",
    )(page_table, sel, zq3, cos, sin, *([cache4] * len(cache_specs)), new_kv3, wst3, win3, ocmp, gates4)


def _gmlp_kernel(z_ref, g_ref, b_ref, ws_ref, bst_ref, o_ref, v_ref, *, lc):
    z = _gelu(z_ref[...])
    u, v = z[:, :W_B], z[:, W_B:]
    mu = jnp.mean(v, -1, keepdims=True)
    d = v - mu
    var = jnp.mean(d * d, -1, keepdims=True)
    vn = d * lax.rsqrt(var + 1e-5) * g_ref[...] + b_ref[...]
    v_ref[...] = vn
    vb = vn.astype(BF16)
    ti = lax.broadcasted_iota(I32, (lc, lc), 0)
    si = lax.broadcasted_iota(I32, (lc, lc), 1)
    for g in range(GMLP_GROUPS):
        cs = slice(g * GMLP_GDIM, (g + 1) * GMLP_GDIM)
        w = jnp.where(si <= ti, ws_ref[g], 0.0).astype(BF16)
        mixed = _dot(w, vb[:, cs]) + bst_ref[:, g:g + 1]
        o_ref[:, cs] = (u[:, cs] * mixed).astype(o_ref.dtype)


def _gmlp(zuv, ln_g, ln_b, ws, bs, lc):
    M = zuv.shape[0]
    ws = ws[:, :lc, :lc]
    bst = bs[:, :lc].T
    return pl.pallas_call(
        functools.partial(_gmlp_kernel, lc=lc),
        out_shape=(jax.ShapeDtypeStruct((M, W_B), BF16), jax.ShapeDtypeStruct((M, W_B), F32)),
        grid=(M // lc,),
        in_specs=[pl.BlockSpec((lc, 2 * W_B), lambda i: (i, 0)),
                  pl.BlockSpec((1, W_B), lambda i: (0, 0)),
                  pl.BlockSpec((1, W_B), lambda i: (0, 0)),
                  pl.BlockSpec((GMLP_GROUPS, lc, lc), lambda i: (0, 0, 0)),
                  pl.BlockSpec((lc, GMLP_GROUPS), lambda i: (0, 0))],
        out_specs=(pl.BlockSpec((lc, W_B), lambda i: (i, 0)), pl.BlockSpec((lc, W_B), lambda i: (i, 0))),
        compiler_params=_cp("parallel"),
        name="gmlp",
    )(zuv, ln_g.reshape(1, W_B), ln_b.reshape(1, W_B), ws, bst)


def _mlstm_kernel(bi_ref, bf_ref, q_ref, k_ref, v_ref, og_ref, sm_ref, smt_ref, ng_ref, c0_ref, n0_ref, m0_ref,
                  h_ref, c_ref, n_ref, m_ref, c_s, n_s, m_s, *, lc, valid, ci_col, cf_col):
    c = pl.program_id(1)

    @pl.when(c == 0)
    def _():
        c_s[...] = c0_ref[0]
        n_s[...] = n0_ref[0]
        m_s[...] = m0_ref[0]

    ti = lax.broadcasted_iota(I32, (lc, lc), 0)
    si = lax.broadcasted_iota(I32, (lc, lc), 1)
    tril = (si <= ti) & (si < valid)
    rvalid = lax.broadcasted_iota(I32, (lc, 1), 0) < valid
    last = valid - 1
    kscale = MLSTM_DQK ** -0.5
    for h in range(MLSTM_HEADS):
        ig_c = sm_ref[:, ci_col + h:ci_col + h + 1] + bi_ref[h]
        lf_c = _log_sigmoid(sm_ref[:, cf_col + h:cf_col + h + 1] + bf_ref[h])
        ig_r = smt_ref[0, h:h + 1, :] + bi_ref[h]
        lf_r = _log_sigmoid(smt_ref[0, MLSTM_HEADS + h:MLSTM_HEADS + h + 1, :] + bf_ref[h])
        F_c = jnp.sum(jnp.where(si <= ti, lf_r, 0.0), axis=1, keepdims=True)
        F_r = jnp.sum(jnp.where(ti <= si, lf_c, 0.0), axis=0, keepdims=True)
        D = jnp.where(tril, F_c - F_r + ig_r, NEG_INF)
        m_prev = m_s[h:h + 1, 0:1]
        m_inter = F_c + m_prev
        m_t = jnp.maximum(m_inter, jnp.max(D, axis=1, keepdims=True))
        qh = q_ref[:, h * MLSTM_DQK:(h + 1) * MLSTM_DQK]
        ks = k_ref[:, h * MLSTM_DQK:(h + 1) * MLSTM_DQK] * kscale
        vh = v_ref[:, h * MLSTM_DV:(h + 1) * MLSTM_DV]
        qb, kb, vb = qh.astype(BF16), ks.astype(BF16), vh.astype(BF16)
        S = _dot_nt(qb, kb) * jnp.exp(D - m_t)
        dec = jnp.exp(m_inter - m_t)
        C = c_s[h]
        n_row = n_s[h:h + 1, :]
        num = _dot(S.astype(BF16), vb) + dec * _dot_nt(qb, C.astype(BF16))
        den = jnp.sum(S, axis=1, keepdims=True) + dec * jnp.sum(qh * n_row, axis=1, keepdims=True)
        hh = num / jnp.maximum(jnp.abs(den), 1.0)
        m_new = m_t[last:last + 1, :]
        F_last = F_c[last:last + 1, :]
        wl_c = jnp.where(rvalid, jnp.exp(F_last - F_c + ig_c - m_new), 0.0)
        dl = jnp.exp(F_last + m_prev - m_new)
        c_s[h] = dl * C + _dot_tn((vh * wl_c).astype(BF16), kb)
        n_s[h:h + 1, :] = dl * n_row + jnp.sum(wl_c * ks, axis=0, keepdims=True)
        m_s[h:h + 1, :] = jnp.broadcast_to(m_new, (1, 128))
        mu = jnp.mean(hh, -1, keepdims=True)
        d = hh - mu
        var = jnp.mean(d * d, -1, keepdims=True)
        hn = d * lax.rsqrt(var + 1e-5) * ng_ref[h:h + 1, :]
        vs = slice(h * MLSTM_DV, (h + 1) * MLSTM_DV)
        h_ref[:, vs] = (_sigmoid(og_ref[:, vs]) * hn).astype(h_ref.dtype)

    @pl.when(c == pl.num_programs(1) - 1)
    def _():
        c_ref[0] = c_s[...]
        n_ref[0] = n_s[...]
        m_ref[0] = m_s[...]


def _mlstm(zcq, zck, zcv, zco, zsm, b_i, b_f, norm_g, C0, n0, m0, B, T, lc, valid, ci_col, cf_col):
    nc = T // lc
    H, DQ, DV = MLSTM_HEADS, MLSTM_DQK, MLSTM_DV
    smt = jnp.concatenate([zsm[:, ci_col:ci_col + H], zsm[:, cf_col:cf_col + H]], -1)
    smt = smt.reshape(B * nc, lc, 2 * H).transpose(0, 2, 1)
    m0b = jnp.broadcast_to(m0[..., None], (B, H, 128))
    row = lambda w: pl.BlockSpec((lc, w), lambda b, c: (b * nc + c, 0))
    smem = pl.BlockSpec(memory_space=pltpu.SMEM)
    kern = functools.partial(_mlstm_kernel, lc=lc, valid=valid, ci_col=ci_col, cf_col=cf_col)
    c_spec = pl.BlockSpec((1, H, DV, DQ), lambda b, c: (b, 0, 0, 0))
    n_spec = pl.BlockSpec((1, H, DQ), lambda b, c: (b, 0, 0))
    h, C, n, m = pl.pallas_call(
        kern,
        out_shape=(jax.ShapeDtypeStruct((B * T, H * DV), BF16), jax.ShapeDtypeStruct((B, H, DV, DQ), F32),
                   jax.ShapeDtypeStruct((B, H, DQ), F32), jax.ShapeDtypeStruct((B, H, 128), F32)),
        grid=(B, nc),
        in_specs=[smem, smem, row(H * DQ), row(H * DQ), row(H * DV), row(H * DV), row(128),
                  pl.BlockSpec((1, 2 * H, lc), lambda b, c: (b * nc + c, 0, 0)),
                  pl.BlockSpec((H, DV), lambda b, c: (0, 0)), c_spec, n_spec, n_spec],
        out_specs=(row(H * DV), c_spec, n_spec, n_spec),
        scratch_shapes=[pltpu.VMEM((H, DV, DQ), F32), pltpu.VMEM((H, DQ), F32), pltpu.VMEM((H, 128), F32)],
        compiler_params=_cp("parallel", "arbitrary"),
        name="mlstm",
    )(b_i, b_f, zcq, zck, zcv, zco, zsm, smt, norm_g, C0, n0, m0b)
    return h, C, n, m[..., 0]


def _merge_kernel(a_ref, b_ref, c_ref, wa_ref, wb_ref, wc_ref, ga_ref, gb_ref, gc_ref, o_ref):
    o = (_sigmoid(ga_ref[...]) * _dot(a_ref[...], wa_ref[...])
         + _sigmoid(gb_ref[...]) * _dot(b_ref[...], wb_ref[...])
         + _sigmoid(gc_ref[...]) * _dot(c_ref[...], wc_ref[...]))
    o_ref[...] = o.astype(o_ref.dtype)


def _merge(oa, ob, oc, wa, wb, wc, zmg, tm=512, tn=512):
    M, K = oa.shape
    D = wa.shape[1]
    tm, tn = min(tm, M), min(tn, D)
    nj = D // tn
    x_spec = pl.BlockSpec((tm, K), lambda i, j: (i, 0))
    w_spec = pl.BlockSpec((K, tn), lambda i, j: (0, j))
    g_spec = lambda o: pl.BlockSpec((tm, tn), lambda i, j: (i, o * nj + j))
    return pl.pallas_call(
        _merge_kernel,
        out_shape=jax.ShapeDtypeStruct((M, D), BF16),
        grid=(M // tm, nj),
        in_specs=[x_spec, x_spec, x_spec, w_spec, w_spec, w_spec, g_spec(0), g_spec(1), g_spec(2)],
        out_specs=pl.BlockSpec((tm, tn), lambda i, j: (i, j)),
        compiler_params=_cp("parallel", "parallel"),
        name="merge",
    )(oa, ob, oc, wa, wb, wc, zmg, zmg, zmg)


def _top16(s, ids, big, track_pos):
    vals, idxs = [], []
    pos = jnp.full(s.shape, -1, I32) if track_pos else None
    for it in range(PEER_TOPK):
        mx = jnp.max(s, axis=0, keepdims=True)
        idx = jnp.min(jnp.where(s == mx, ids, big), axis=0, keepdims=True)
        one = ids == idx
        s = jnp.where(one, NEG_INF, s)
        if track_pos:
            pos = jnp.where(one, it, pos)
        vals.append(mx)
        idxs.append(idx)
    return vals, idxs, pos


def _peer_route_kernel(q_ref, k1_ref, k2_ref, s1_ref, s2_ref, pk_ref):
    tm = q_ref.shape[0]
    half = PEER_DQ // 2
    K = PEER_TOPK
    k1 = k1_ref[...].astype(BF16)
    k2 = k2_ref[...].astype(BF16)
    kid = lax.broadcasted_iota(I32, (N_KEYS, 1), 0)
    i16 = lax.broadcasted_iota(I32, (K, 1), 0)
    i8 = lax.broadcasted_iota(I32, (8, 1), 0)
    cid = jnp.concatenate([i16, K + i8, 2 * K + i8, 3 * K + i8, i8 * K, i8 * K + 1, i8 * K + 2, (8 + i8) * K], axis=0)
    for h in range(PEER_HEADS):
        q1 = q_ref[:, h * PEER_DQ:h * PEER_DQ + half].astype(BF16)
        q2 = q_ref[:, h * PEER_DQ + half:(h + 1) * PEER_DQ].astype(BF16)
        s1 = _dot_nt(k1, q1)
        s2 = _dot_nt(k2, q2)
        v1, _, pos1 = _top16(s1, kid, N_KEYS, True)
        v2, _, pos2 = _top16(s2, kid, N_KEYS, True)
        v1m = jnp.concatenate(v1, axis=0)
        v2m = jnp.concatenate(v2, axis=0)
        pieces = [v1[0] + v2m, v1[1] + v2m[0:8], v1[2] + v2m[0:8], v1[3] + v2m[0:8]]
        for kk in range(3):
            pieces.append(jnp.where(i8 >= 4, v1m[0:8] + v2[kk], NEG_INF))
        pieces.append(v1m[8:16] + v2[0])
        cv, cj, _ = _top16(jnp.concatenate(pieces, axis=0), cid, K * K, False)
        z = jnp.zeros((1, tm), F32)
        wbits = jnp.zeros((K, tm), I32)
        for it in range(K):
            z = z + jnp.exp(cv[it] - cv[0])
            ii = jnp.right_shift(cj[it], 4)
            kk = cj[it] & (K - 1)
            wbits = jnp.where(i16 == ii, wbits | jnp.left_shift(1, kk), wbits)
        in1, in2 = pos1 >= 0, pos2 >= 0
        s1_ref[h] = jnp.where(in1, (s1 - cv[0]) * LOG2E - jnp.log2(z), NEG_INF)
        s2_ref[h] = jnp.where(in2, s2 * LOG2E, NEG_INF)
        rw = jnp.zeros((N_KEYS, tm), I32)
        for i in range(K):
            rw = jnp.where(pos1 == i, wbits[i:i + 1, :], rw)
        b2 = jnp.where(in2, jnp.left_shift(1, jnp.maximum(pos2, 0)), 0)
        pk_ref[h] = b2 | jnp.left_shift(rw, 16)


def _peer_route(q, k1, k2, tm=256):
    M = q.shape[0]
    tm = min(tm, M)
    spec = pl.BlockSpec((PEER_HEADS, N_KEYS, tm), lambda i: (0, 0, i))
    shp = lambda dt: jax.ShapeDtypeStruct((PEER_HEADS, N_KEYS, M), dt)
    return pl.pallas_call(
        _peer_route_kernel,
        out_shape=(shp(F32), shp(F32), shp(I32)),
        grid=(M // tm,),
        in_specs=[pl.BlockSpec((tm, PEER_HEADS * PEER_DQ), lambda i: (i, 0)),
                  pl.BlockSpec((N_KEYS, PEER_DQ // 2), lambda i: (0, 0)),
                  pl.BlockSpec((N_KEYS, PEER_DQ // 2), lambda i: (0, 0))],
        out_specs=(spec, spec, spec),
        compiler_params=_cp("parallel"),
        name="peer_route",
    )(q, k1, k2)


PEER_SUB = 256


def _peer_dense_kernel(x_ref, u_ref, v_ref, s1_ref, s2_ref, pk_ref, o_ref, *, te):
    e = pl.program_id(1)

    @pl.when(e == 0)
    def _():
        o_ref[...] = jnp.zeros_like(o_ref)

    x = x_ref[...]
    hs = [_dot_nt(x, u_ref[c * PEER_SUB:(c + 1) * PEER_SUB, :]) for c in range(te // PEER_SUB)]
    per = te // N_KEYS
    gates = []
    for al in range(per):
        a = e * per + al
        gate = None
        for hd in range(PEER_HEADS):
            wt = jnp.exp2(s1_ref[hd, pl.ds(a, 1), :] + s2_ref[hd])
            arow = pk_ref[hd, pl.ds(a, 1), :]
            abits = lax.shift_right_logical(arow, jnp.full_like(arow, 16))
            g = jnp.where((abits & pk_ref[hd]) != 0, wt, 0.0)
            gate = g if gate is None else gate + g
        gates.append(gate.T)
    gps = PEER_SUB // N_KEYS
    acts = [(_gelu(hs[c]) * jnp.concatenate(gates[c * gps:(c + 1) * gps], axis=1)).astype(BF16)
            for c in range(te // PEER_SUB)]
    o_ref[...] += _dot(jnp.concatenate(acts, axis=1), v_ref[...])


def _peer_dense(xb, u, v, s1l, s2l, pk, tm=512, te=512):
    M, D = xb.shape
    NE = u.shape[0]
    tm = min(tm, M)
    n_e = NE // te
    once = pl.Buffered(buffer_count=1)
    rspec = pl.BlockSpec((PEER_HEADS, N_KEYS, tm), lambda i, e: (0, 0, i), pipeline_mode=once)
    return pl.pallas_call(
        functools.partial(_peer_dense_kernel, te=te),
        out_shape=jax.ShapeDtypeStruct((M, D), F32),
        grid=(M // tm, n_e),
        in_specs=[pl.BlockSpec((tm, D), lambda i, e: (i, 0), pipeline_mode=once),
                  pl.BlockSpec((te, D), lambda i, e: (e, 0)),
                  pl.BlockSpec((te, D), lambda i, e: (e, 0)),
                  rspec, rspec, rspec],
        out_specs=pl.BlockSpec((tm, D), lambda i, e: (i, 0)),
        compiler_params=_cp("parallel", "arbitrary"),
        name="peer_dense",
    )(xb, u, v, s1l, s2l, pk)


def _peer(x1, x1b, lw, alpha):
    M, D = x1.shape
    mp = -(-M // 128) * 128
    xb = x1b if mp == M else jnp.pad(x1b, ((0, mp - M), (0, 0)))
    q = _mm(xb, lw["peer_wq"])
    s1l, s2l, pk = _peer_route(q, lw["peer_k1"], lw["peer_k2"])
    y = _peer_dense(xb, lw["peer_u"], lw["peer_v"], s1l, s2l, pk)[:M]
    return _res_ln(x1, y, lw["ln2_g"], lw["ln2_b"], alpha)


Z_SEGS = ("a_q", "a_kv", "a_g", "b_uv", "c_q", "c_k", "c_v", "c_i", "c_f", "c_o", "mg")
GATE_COL = 0
CI_COL = 3 * NSA_HEADS
CF_COL = CI_COL + MLSTM_HEADS


def _prep_layer_weights(l, w):
    D = w["w_in"].shape[1]
    sizes = (NSA_HEADS * HEAD_DIM, 6 * NSA_KV * HEAD_DIM, 3 * NSA_HEADS, 2 * W_B,
             MLSTM_HEADS * MLSTM_DQK, MLSTM_HEADS * MLSTM_DQK, MLSTM_HEADS * MLSTM_DV, MLSTM_HEADS, MLSTM_HEADS,
             MLSTM_HEADS * MLSTM_DV, 3 * D)
    offs = np.concatenate([[0], np.cumsum(sizes)])
    w_in = w["w_in"][l]
    seg = {n: (int(offs[i]), int(offs[i + 1])) for i, n in enumerate(Z_SEGS)}
    cut = lambda n: w_in[:, seg[n][0]:seg[n][1]].astype(BF16)
    lw = {n: cut(n) for n in ("a_q", "a_kv", "b_uv", "c_q", "c_k", "c_v", "c_o", "mg")}
    small = jnp.concatenate([w_in[:, seg[n][0]:seg[n][1]] for n in ("a_g", "c_i", "c_f")], -1)
    lw["small"] = jnp.pad(small, ((0, 0), (0, 128 - small.shape[1]))).astype(BF16)
    for n in ("w_br_a", "w_br_b", "w_br_c", "w_out", "peer_wq", "peer_u", "peer_v"):
        lw[n] = w[n][l].astype(BF16)
    for n in ("cmp_pe", "cmp_w1", "cmp_b1", "cmp_w2", "gmlp_ln_g", "gmlp_ln_b", "gmlp_ws", "gmlp_bs",
              "mlstm_b_i", "mlstm_b_f", "mlstm_norm_g", "ln1_g", "ln1_b", "peer_k1", "peer_k2", "ln2_g", "ln2_b"):
        lw[n] = w[n][l]
    return lw


def _gates_by_group(zsm):
    M = zsm.shape[0]
    gt = zsm[:, GATE_COL:GATE_COL + 3 * NSA_HEADS].reshape(M, NSA_KV, 3 * NSA_REP).transpose(1, 0, 2)
    return jnp.pad(gt, ((0, 0), (0, 0), (0, 128 - 3 * NSA_REP)))


def _layer(x, xb, B, T, tv, lw, alpha, dec):
    M, D = x.shape
    z = {n: _mm(xb, lw[n]) for n in ("a_q", "a_kv", "b_uv", "c_q", "c_k", "c_v", "c_o", "mg", "small")}
    zsm = z["small"]
    q_start = 0 if dec is None else dec["q_start"]
    cos, sin = _rope_tables(q_start + jnp.arange(T))
    new_kv, win = _kvprep(z["a_kv"], cos, sin, T)
    gates4 = _gates_by_group(zsm)
    cmp_w = (lw["cmp_pe"], lw["cmp_w1"], lw["cmp_b1"], lw["cmp_w2"])
    gw = NSA_KV * HEAD_DIM
    if dec is None:
        n_pages = T // PAGE_SIZE
        ident = jnp.arange(B * n_pages, dtype=I32).reshape(B, n_pages)
        src3 = z["a_kv"][:, :2 * gw].reshape(M, 2 * NSA_KV, HEAD_DIM)
        kvc = _compress(src3, ident, 0, *cmp_w)
        o_a = _nsa_prefill(z["a_q"], new_kv, win, kvc, gates4, cos, sin, B, T)
        lc_g, lc_m = CHUNK, math.gcd(T, MLSTM_CHUNK)
        C0 = jnp.zeros((B, MLSTM_HEADS, MLSTM_DV, MLSTM_DQK), F32)
        n0 = jnp.zeros((B, MLSTM_HEADS, MLSTM_DQK), F32)
        m0 = jnp.zeros((B, MLSTM_HEADS), F32)
        valid = lc_m
    else:
        page_table = dec["page_table"]
        kvc = _compress(dec["cache3"], page_table, dec["page_off"], *cmp_w)
        L = q_start + tv
        n_cmp = (L - CMP_BLOCK) // CMP_STRIDE + 1
        n_slc = -(-L // SLC_BLOCK)
        assert n_cmp <= q_start // CMP_STRIDE - 1 and n_slc == q_start // SLC_BLOCK + 1
        assert q_start % SLC_BLOCK == 0 and tv <= T
        zq3 = z["a_q"].reshape(B, T, -1)
        ocmp, sel = _nsa_dec_sel(zq3, kvc, n_cmp, n_slc, q_start)
        sel = sel[:, :, :tv, :min(TOP_BLOCKS, n_slc)]
        o_a = _nsa_dec_attn(page_table, sel, zq3, cos, sin, dec["cache4"], dec["page_off"], new_kv.reshape(B, T, -1),
                            dec["wst3"], dec["win_off"], win.reshape(B, T, -1), ocmp, gates4, tv, q_start)
        o_a = o_a.reshape(M, -1)
        lc_g = lc_m = T
        C0, n0, m0 = dec["C"], dec["n"], dec["m"]
        valid = tv
    o_b, v_rows = _gmlp(z["b_uv"], lw["gmlp_ln_g"], lw["gmlp_ln_b"], lw["gmlp_ws"], lw["gmlp_bs"], lc_g)
    o_c, C, n, m = _mlstm(z["c_q"], z["c_k"], z["c_v"], z["c_o"], zsm, lw["mlstm_b_i"], lw["mlstm_b_f"],
                          lw["mlstm_norm_g"], C0, n0, m0, B, T, lc_m, valid, CI_COL, CF_COL)
    merged = _merge(o_a, o_b, o_c, lw["w_br_a"], lw["w_br_b"], lw["w_br_c"], z["mg"])
    x1, x1b = _res_ln(x, _mm(merged, lw["w_out"]), lw["ln1_g"], lw["ln1_b"], alpha)
    y, yb = _peer(x1, x1b, lw, alpha)
    return y, yb, new_kv, win, v_rows, C, n, m


def kernel(x_prompt, x_sample, cache_nsa_kv, state_nsa_win, state_mlstm_C, state_mlstm_n, state_mlstm_m,
           page_table, w_in, cmp_pe, cmp_w1, cmp_b1, cmp_w2, gmlp_ln_g, gmlp_ln_b, gmlp_ws, gmlp_bs,
           mlstm_b_i, mlstm_b_f, mlstm_norm_g, w_br_a, w_br_b, w_br_c, w_out, ln1_g, ln1_b,
           peer_wq, peer_k1, peer_k2, peer_u, peer_v, ln2_g, ln2_b):
    w = dict(w_in=w_in, cmp_pe=cmp_pe, cmp_w1=cmp_w1, cmp_b1=cmp_b1, cmp_w2=cmp_w2, gmlp_ln_g=gmlp_ln_g,
             gmlp_ln_b=gmlp_ln_b, gmlp_ws=gmlp_ws, gmlp_bs=gmlp_bs, mlstm_b_i=mlstm_b_i, mlstm_b_f=mlstm_b_f,
             mlstm_norm_g=mlstm_norm_g, w_br_a=w_br_a, w_br_b=w_br_b, w_br_c=w_br_c, w_out=w_out, ln1_g=ln1_g,
             ln1_b=ln1_b, peer_wq=peer_wq, peer_k1=peer_k1, peer_k2=peer_k2, peer_u=peer_u, peer_v=peer_v,
             ln2_g=ln2_g, ln2_b=ln2_b)
    depth = w_in.shape[0]
    alpha = (2 * depth) ** 0.25
    Bp, Tp, D = x_prompt.shape
    Bs, Ts, _ = x_sample.shape
    Tpad = 8
    past_len = page_table.shape[1] * PAGE_SIZE
    n_pool = cache_nsa_kv.shape[1]
    P = state_nsa_win.shape[2]
    cache3 = cache_nsa_kv.reshape(depth * n_pool * PAGE_SIZE, 4 * NSA_KV, HEAD_DIM)
    cache4 = cache_nsa_kv.reshape(depth * n_pool * PAGE_SIZE, 2, 2 * NSA_KV, HEAD_DIM)
    wst3 = state_nsa_win.reshape(depth * Bs * P, 2 * NSA_KV, HEAD_DIM)
    yp = x_prompt.reshape(Bp * Tp, D)
    ys = jnp.pad(x_sample, ((0, 0), (0, Tpad - Ts), (0, 0))).reshape(Bs * Tpad, D)
    ypb, ysb = yp.astype(BF16), ys.astype(BF16)
    outs = [[] for _ in range(11)]
    wkeep_p = min(WINDOW, Tp)
    for l in range(depth):
        lw = _prep_layer_weights(l, w)
        yp, ypb, kv, win, _, C, n, m = _layer(yp, ypb, Bp, Tp, Tp, lw, alpha, None)
        outs[0].append(kv.reshape(Bp, Tp, 4, NSA_KV, HEAD_DIM))
        outs[2].append(win.reshape(Bp, Tp, 2, NSA_KV, HEAD_DIM)[:, Tp - wkeep_p:])
        outs[5].append(C); outs[6].append(n); outs[7].append(m)
        dec = dict(q_start=past_len, cache3=cache3, cache4=cache4, page_off=l * n_pool, page_table=page_table,
                   wst3=wst3, win_off=(l * Bs, depth * Bs), C=state_mlstm_C[l], n=state_mlstm_n[l], m=state_mlstm_m[l])
        ys, ysb, kv, win, v_rows, C, n, m = _layer(ys, ysb, Bs, Tpad, Ts, lw, alpha, dec)
        outs[1].append(kv.reshape(Bs, Tpad, 4, NSA_KV, HEAD_DIM)[:, :Ts])
        win_new = win.reshape(Bs, Tpad, 2, NSA_KV, HEAD_DIM)[:, :Ts]
        outs[3].append(jnp.concatenate([state_nsa_win[l], win_new], 1)[:, Ts:])
        outs[4].append(v_rows.reshape(Bs, Tpad, W_B)[:, :Ts])
        outs[8].append(C); outs[9].append(n); outs[10].append(m)
    y_prompt = yp.reshape(Bp, Tp, D)
    y_sample = ys.reshape(Bs, Tpad, D)[:, :Ts]
    st = [jnp.stack(o) for o in outs]
    return (y_prompt, y_sample, st[0], st[1], st[2], st[3], st[4], st[5], st[6], st[7], st[8], st[9], st[10])
```

```python
import functools
import math

import numpy as np
import jax
import jax.numpy as jnp
from jax import lax
from jax.experimental import pallas as pl
from jax.experimental.pallas import tpu as pltpu

F32 = jnp.float32
BF16 = jnp.bfloat16
I32 = jnp.int32

HEAD_DIM = 128
ROT_DIM = HEAD_DIM // 4
ROPE_THETA = 500000.0
NSA_HEADS = 16
NSA_KV = 4
NSA_REP = NSA_HEADS // NSA_KV
CMP_BLOCK = 32
CMP_STRIDE = 16
SLC_BLOCK = 64
SLC_SHIFT = 6
TOP_BLOCKS = 16
WINDOW = 512
FORCE_SCORE = 1.0e9
CHUNK = 128
GMLP_GROUPS = 16
GMLP_GDIM = 128
W_B = GMLP_GROUPS * GMLP_GDIM
MLSTM_HEADS = 8
MLSTM_DQK = 128
MLSTM_DV = 256
MLSTM_KERNEL_CHUNK = 128
PEER_HEADS = 8
PEER_DQ = 256
N_KEYS = 128
PEER_TOPK = 16
PAGE_SIZE = 128
LOG2E = 1.4426950408889634
NEG_INF = float("-inf")

VMEM_LIMIT_BYTES = 56 * 1024 * 1024


def _cp(*sem):
    return pltpu.CompilerParams(dimension_semantics=sem, vmem_limit_bytes=VMEM_LIMIT_BYTES)


def _gelu(x):
    return 0.5 * x * (1.0 + jnp.tanh(0.7978845608028654 * (x + 0.044715 * (x * x * x))))


def _sigmoid(x):
    return 1.0 / (1.0 + jnp.exp(-x))


def _log_sigmoid(x):
    return jnp.minimum(x, 0.0) - jnp.log(1.0 + jnp.exp(-jnp.abs(x)))


def _dot(a, b):
    return jnp.dot(a, b, preferred_element_type=F32)


def _dot_nt(a, b):
    return lax.dot_general(a, b, (((1,), (1,)), ((), ())), preferred_element_type=F32)


def _dot_tn(a, b):
    return lax.dot_general(a, b, (((0,), (0,)), ((), ())), preferred_element_type=F32)


def _masked_softmax(lg, mask):
    lg = jnp.where(mask, lg, NEG_INF)
    mx = jnp.max(lg, -1, keepdims=True)
    mx = jnp.where(mx > NEG_INF, mx, 0.0)
    e = jnp.exp(lg - mx)
    return e / jnp.maximum(jnp.sum(e, -1, keepdims=True), 1e-30)


def _split3_dot(p, m_bf16):
    hi = p.astype(BF16)
    r1 = p - hi.astype(F32)
    mid = r1.astype(BF16)
    lo = (r1 - mid.astype(F32)).astype(BF16)
    return _dot(hi, m_bf16) + _dot(mid, m_bf16) + _dot(lo, m_bf16)


def _rope(x, cos, sin):
    lane = lax.broadcasted_iota(I32, x.shape, 1)
    up = pltpu.roll(x, HEAD_DIM - ROT_DIM // 2, 1)
    dn = pltpu.roll(x, ROT_DIM // 2, 1)
    return x * cos + jnp.where(lane < ROT_DIM // 2, up, dn) * sin


def _rope_tables(pos):
    inv = ROPE_THETA ** (-jnp.arange(0, ROT_DIM, 2, dtype=F32) / ROT_DIM)
    ang = pos.astype(F32)[:, None] * inv[None, :]
    c, s = jnp.cos(ang), jnp.sin(ang)
    n = pos.shape[0]
    cos = jnp.concatenate([c, c, jnp.ones((n, HEAD_DIM - ROT_DIM), F32)], -1)
    sin = jnp.concatenate([-s, s, jnp.zeros((n, HEAD_DIM - ROT_DIM), F32)], -1)
    return cos, sin


def _mm_kernel(x_ref, w_ref, o_ref, acc_ref):
    k = pl.program_id(2)

    @pl.when(k == 0)
    def _():
        acc_ref[...] = jnp.zeros_like(acc_ref)

    acc_ref[...] += _dot(x_ref[...], w_ref[...])

    @pl.when(k == pl.num_programs(2) - 1)
    def _():
        o_ref[...] = acc_ref[...].astype(o_ref.dtype)


def _w_spec(w, layer, block, imap):
    if w.ndim == 2:
        return pl.BlockSpec(block, imap)
    return pl.BlockSpec((None,) + block, lambda *a: (layer,) + imap(*a))


def _mm(x, w, layer=None, out_dtype=F32, tm=1024, tn=512, tk=4096):
    M, K = x.shape
    N = w.shape[-1]
    tm, tn, tk = min(tm, M), min(tn, N), min(tk, K)
    assert M % tm == 0 and N % tn == 0 and K % tk == 0, (x.shape, w.shape)
    return pl.pallas_call(
        _mm_kernel,
        out_shape=jax.ShapeDtypeStruct((M, N), out_dtype),
        grid=(M // tm, N // tn, K // tk),
        in_specs=[pl.BlockSpec((tm, tk), lambda i, j, k: (i, k)),
                  _w_spec(w, layer, (tk, tn), lambda i, j, k: (k, j))],
        out_specs=pl.BlockSpec((tm, tn), lambda i, j, k: (i, j)),
        scratch_shapes=[pltpu.VMEM((tm, tn), F32)],
        compiler_params=_cp("parallel", "parallel", "arbitrary"),
        name="mm",
    )(x, w)


def _res_ln_kernel(x_ref, y_ref, g_ref, b_ref, o_ref, ob_ref, *, alpha):
    v = alpha * x_ref[...] + y_ref[...]
    mu = jnp.mean(v, -1, keepdims=True)
    d = v - mu
    var = jnp.mean(d * d, -1, keepdims=True)
    o = d * lax.rsqrt(var + 1e-5) * g_ref[...] + b_ref[...]
    o_ref[...] = o
    ob_ref[...] = o.astype(BF16)


def _res_ln(x, y, g, b, alpha, tm=256):
    M, D = x.shape
    tm = min(tm, M)
    row = pl.BlockSpec((tm, D), lambda i: (i, 0))
    vec = pl.BlockSpec((1, D), lambda i: (0, 0))
    return pl.pallas_call(
        functools.partial(_res_ln_kernel, alpha=alpha),
        out_shape=(jax.ShapeDtypeStruct((M, D), F32), jax.ShapeDtypeStruct((M, D), BF16)),
        grid=(M // tm,),
        in_specs=[row, row, vec, vec],
        out_specs=(row, row),
        compiler_params=_cp("parallel"),
        name="res_ln",
    )(x, y, g.reshape(1, D), b.reshape(1, D))


def _kvprep_kernel(z_ref, cos_ref, sin_ref, kv_ref, win_ref):
    cos, sin = cos_ref[...], sin_ref[...]
    gw = NSA_KV * HEAD_DIM
    kv_ref[:, 0:2 * gw] = z_ref[:, 0:2 * gw]
    kv_ref[:, 3 * gw:4 * gw] = z_ref[:, 3 * gw:4 * gw]
    win_ref[:, gw:2 * gw] = z_ref[:, 5 * gw:6 * gw]
    for g in range(NSA_KV):
        a = 2 * gw + g * HEAD_DIM
        kv_ref[:, a:a + HEAD_DIM] = _rope(z_ref[:, a:a + HEAD_DIM], cos, sin)
        a = 4 * gw + g * HEAD_DIM
        win_ref[:, g * HEAD_DIM:(g + 1) * HEAD_DIM] = _rope(z_ref[:, a:a + HEAD_DIM], cos, sin)


def _kvprep(zkv, cos, sin, T, tm=256):
    M = zkv.shape[0]
    tm = min(tm, T)
    nt = T // tm
    gw = NSA_KV * HEAD_DIM
    return pl.pallas_call(
        _kvprep_kernel,
        out_shape=(jax.ShapeDtypeStruct((M, 4 * gw), F32), jax.ShapeDtypeStruct((M, 2 * gw), F32)),
        grid=(M // tm,),
        in_specs=[pl.BlockSpec((tm, 6 * gw), lambda i: (i, 0)),
                  pl.BlockSpec((tm, HEAD_DIM), lambda i: (i % nt, 0)),
                  pl.BlockSpec((tm, HEAD_DIM), lambda i: (i % nt, 0))],
        out_specs=(pl.BlockSpec((tm, 4 * gw), lambda i: (i, 0)),
                   pl.BlockSpec((tm, 2 * gw), lambda i: (i, 0))),
        compiler_params=_cp("parallel"),
        name="kvprep",
    )(zkv, cos, sin)


CMP_PAGES = 8


def _cmp_bias_kernel(pe_ref, w1f_ref, b1_ref, o_ref):
    for kv in range(2):
        b = b1_ref[kv:kv + 1, :]
        for half in range(2):
            b = b + _dot(pe_ref[kv, half].astype(BF16), w1f_ref[kv, half])[0:1, :]
        o_ref[:, kv * HEAD_DIM:(kv + 1) * HEAD_DIM] = jnp.broadcast_to(b, (8, HEAD_DIM))


def _cmp_kernel(pt_ref, *refs, P):
    page_refs = refs[:P + 1]
    wa_ref, wb_ref, bias_ref, w2_ref, o_ref, res_s = refs[P + 1:]
    hp = PAGE_SIZE // CMP_STRIDE
    sg = 2 * NSA_KV
    rows = P * hp * sg
    acc_a = jnp.zeros((rows, 2 * HEAD_DIM), F32)
    acc_b = jnp.zeros((rows, 2 * HEAD_DIM), F32)
    for j in range(CMP_STRIDE):
        la, lb = [], []
        for p in range(P):
            la.append(page_refs[p][pl.ds(j, hp, stride=CMP_STRIDE), 0:sg, :].reshape(hp * sg, HEAD_DIM))
            lb.append(page_refs[p][pl.ds(j + CMP_STRIDE, hp - 1, stride=CMP_STRIDE), 0:sg, :]
                      .reshape((hp - 1) * sg, HEAD_DIM))
            lb.append(page_refs[p + 1][j, 0:sg, :])
        acc_a = acc_a + _dot(jnp.concatenate(la, axis=0).astype(BF16), wa_ref[j])
        acc_b = acc_b + _dot(jnp.concatenate(lb, axis=0).astype(BF16), wb_ref[j])
    h = _gelu(acc_a + acc_b + bias_ref[0:1, :])
    ok = _dot(h[:, :HEAD_DIM].astype(BF16), w2_ref[0].astype(BF16))
    ov = _dot(h[:, HEAD_DIM:].astype(BF16), w2_ref[1].astype(BF16))
    is_k = (lax.broadcasted_iota(I32, (rows, 1), 0) & (sg - 1)) < NSA_KV
    res_s[...] = jnp.where(is_k, ok, ov)
    for s in range(sg):
        o_ref[0, s] = res_s[pl.ds(s, P * hp, stride=sg), :]


def _compress(src3, page_table, page_off, cmp_pe, cmp_w1, cmp_b1, cmp_w2):
    B, n_pages = page_table.shape
    P = math.gcd(CMP_PAGES, n_pages)
    hp = PAGE_SIZE // CMP_STRIDE
    sg = 2 * NSA_KV
    src4 = src3.reshape(src3.shape[0], src3.shape[1] // sg, sg, HEAD_DIM)
    pe_flat = jnp.broadcast_to(cmp_pe.reshape(2, 2, 1, CMP_STRIDE * HEAD_DIM), (2, 2, 8, CMP_STRIDE * HEAD_DIM))
    w1_flat = cmp_w1.reshape(2, 2, CMP_STRIDE * HEAD_DIM, HEAD_DIM).astype(BF16)
    bias = pl.pallas_call(
        _cmp_bias_kernel,
        out_shape=jax.ShapeDtypeStruct((8, 2 * HEAD_DIM), F32),
        name="cmp_bias",
    )(pe_flat, w1_flat, cmp_b1)
    wa = jnp.concatenate([cmp_w1[0, :CMP_STRIDE], cmp_w1[1, :CMP_STRIDE]], -1).astype(BF16)
    wb = jnp.concatenate([cmp_w1[0, CMP_STRIDE:], cmp_w1[1, CMP_STRIDE:]], -1).astype(BF16)

    def page_spec(p):
        return pl.BlockSpec((PAGE_SIZE, None, sg, HEAD_DIM),
                            lambda b, c, pt: (page_off + pt[b, jnp.minimum(c * P + p, n_pages - 1)], 0, 0, 0))

    full = lambda shape: pl.BlockSpec(shape, lambda b, c, pt: (0,) * len(shape))
    return pl.pallas_call(
        functools.partial(_cmp_kernel, P=P),
        out_shape=jax.ShapeDtypeStruct((B, sg, n_pages * hp, HEAD_DIM), F32),
        grid_spec=pltpu.PrefetchScalarGridSpec(
            num_scalar_prefetch=1,
            grid=(B, n_pages // P),
            in_specs=[page_spec(p) for p in range(P + 1)]
            + [full(wa.shape), full(wb.shape), full(bias.shape), full(cmp_w2.shape)],
            out_specs=pl.BlockSpec((1, sg, P * hp, HEAD_DIM), lambda b, c, pt: (b, 0, c, 0)),
            scratch_shapes=[pltpu.VMEM((P * hp * sg, HEAD_DIM), F32)],
        ),
        compiler_params=_cp("parallel", "parallel"),
        name="cmp",
    )(page_table, *([src4] * (P + 1)), wa, wb, bias, cmp_w2)


def _cmp_to_slc_mask(ncp, nsp, n_cmp, n_slc):
    ci = lax.broadcasted_iota(I32, (ncp, nsp), 0)
    si = lax.broadcasted_iota(I32, (ncp, nsp), 1)
    ov = ((ci * CMP_STRIDE < (si + 1) * SLC_BLOCK) & (ci * CMP_STRIDE + CMP_BLOCK > si * SLC_BLOCK)
          & (ci < n_cmp) & (si < n_slc))
    return jnp.where(ov, 1.0, 0.0).astype(BF16)


def _nsa_prefill_kernel(q_ref, cos_ref, sin_ref, kc_ref, vc_ref, ks_ref, vs_ref, kw_ref, vw_ref, gt_ref, o_ref,
                        m_s, l_s, acc_s, *, tq, T, n_cmp, n_slc, wlen, kch):
    i = pl.program_id(2)
    t0 = i * tq
    scale = HEAD_DIM ** -0.5
    rep = NSA_REP
    q4 = q_ref[...]
    Q = jnp.concatenate([q4[:, r * HEAD_DIM:(r + 1) * HEAD_DIM] for r in range(rep)], axis=0)
    cos = jnp.concatenate([cos_ref[...]] * rep, axis=0)
    sin = jnp.concatenate([sin_ref[...]] * rep, axis=0)
    Qb = Q.astype(BF16)
    Qrb = _rope(Q, cos, sin).astype(BF16)
    qpos1 = t0 + lax.broadcasted_iota(I32, (tq, 1), 0)
    qpos = jnp.concatenate([qpos1] * rep, axis=0)
    ncp = kc_ref.shape[2]
    kc = kc_ref[0, 0].astype(BF16)
    vc = vc_ref[0, 0].astype(BF16)
    lg = _dot_nt(Qb, kc) * scale
    cidx = lax.broadcasted_iota(I32, (1, ncp), 1)
    cmask = (cidx * CMP_STRIDE + CMP_BLOCK - 1 <= qpos) & (cidx < n_cmp)
    p = _masked_softmax(lg, cmask)
    o_cmp = _dot(p.astype(BF16), vc)
    psum = p[0:tq]
    for r in range(1, rep):
        psum = psum + p[r * tq:(r + 1) * tq]
    nsp = 128
    imp = _split3_dot(psum, _cmp_to_slc_mask(ncp, nsp, n_cmp, n_slc))
    blk = lax.broadcasted_iota(I32, (1, nsp), 1)
    forced = (blk == 0) | (blk == jnp.right_shift(qpos1, SLC_SHIFT))
    future = blk * SLC_BLOCK > qpos1
    imp = jnp.where(forced, FORCE_SCORE, jnp.where(future, NEG_INF, imp))
    imp = jnp.where(blk < n_slc, imp, NEG_INF)
    rank = jnp.zeros((tq, nsp), F32)
    for s2 in range(n_slc):
        col = imp[:, s2:s2 + 1]
        beats = (col > imp) | ((col == imp) & (blk > s2))
        rank = rank + jnp.where(beats, 1.0, 0.0)
    sel = (rank < float(min(TOP_BLOCKS, n_slc))) & (blk < n_slc)
    selb = jnp.where(sel, 1.0, 0.0).astype(BF16)
    m_s[...] = jnp.full_like(m_s, NEG_INF)
    l_s[...] = jnp.zeros_like(l_s)
    acc_s[...] = jnp.zeros_like(acc_s)

    def key_chunk(c):
        srow = lax.broadcasted_iota(I32, (nsp, kch), 0)
        kcol = c * kch + lax.broadcasted_iota(I32, (nsp, kch), 1)
        expand = jnp.where(jnp.right_shift(kcol, SLC_SHIFT) == srow, 1.0, 0.0).astype(BF16)
        selk = jnp.concatenate([_dot(selb, expand)] * rep, axis=0)
        kpos = c * kch + lax.broadcasted_iota(I32, (1, kch), 1)
        smask = (selk > 0.5) & (kpos <= qpos)
        ks = ks_ref[c * kch:(c + 1) * kch, :].astype(BF16)
        vs = vs_ref[c * kch:(c + 1) * kch, :].astype(BF16)
        lg = jnp.where(smask, _dot_nt(Qrb, ks) * scale, NEG_INF)
        m_old = m_s[...]
        m_new = jnp.maximum(m_old, jnp.max(lg, -1, keepdims=True))
        m_safe = jnp.where(m_new > NEG_INF, m_new, 0.0)
        a = jnp.exp(m_old - m_safe)
        e = jnp.exp(lg - m_safe)
        l_s[...] = a * l_s[...] + jnp.sum(e, -1, keepdims=True)
        acc_s[...] = a * acc_s[...] + _dot(e.astype(BF16), vs)
        m_s[...] = m_new

    key_chunk(0)
    for c in range(1, T // kch):
        pl.when(c * kch < t0 + tq)(functools.partial(key_chunk, c))
    o_s = acc_s[...] / jnp.maximum(l_s[...], 1e-30)
    start = pl.multiple_of(jnp.clip(t0 - WINDOW, 0, T - wlen), 128)
    kw = kw_ref[pl.ds(start, wlen), :].astype(BF16)
    vw = vw_ref[pl.ds(start, wlen), :].astype(BF16)
    dpos = qpos - (start + lax.broadcasted_iota(I32, (1, wlen), 1))
    lw = _dot_nt(Qrb, kw) * scale
    pw = _masked_softmax(lw, (dpos >= 0) & (dpos <= WINDOW))
    o_w = _dot(pw.astype(BF16), vw)
    gt = _sigmoid(gt_ref[0])
    for r in range(rep):
        rows = slice(r * tq, (r + 1) * tq)
        o = (gt[:, 3 * r:3 * r + 1] * o_cmp[rows] + gt[:, 3 * r + 1:3 * r + 2] * o_s[rows]
             + gt[:, 3 * r + 2:3 * r + 3] * o_w[rows])
        o_ref[:, r * HEAD_DIM:(r + 1) * HEAD_DIM] = o.astype(o_ref.dtype)


def _nsa_prefill(zq, new_kv, win, kvc, gates4, cos, sin, B, T, tq=128):
    tq = min(tq, T)
    nt = T // tq
    n_cmp = (T - CMP_BLOCK) // CMP_STRIDE + 1
    n_slc = -(-T // SLC_BLOCK)
    assert n_slc <= 128 and T % tq == 0
    wlen = min(T, WINDOW + tq)
    ncp = kvc.shape[2]
    gw = NSA_KV
    kch = math.gcd(T, 512)
    rows = NSA_REP * tq
    kern = functools.partial(_nsa_prefill_kernel, tq=tq, T=T, n_cmp=n_cmp, n_slc=n_slc, wlen=wlen, kch=kch)
    return pl.pallas_call(
        kern,
        out_shape=jax.ShapeDtypeStruct((B * T, NSA_HEADS * HEAD_DIM), BF16),
        grid=(B, NSA_KV, nt),
        in_specs=[pl.BlockSpec((tq, NSA_REP * HEAD_DIM), lambda b, g, i: (b * nt + i, g)),
                  pl.BlockSpec((tq, HEAD_DIM), lambda b, g, i: (i, 0)),
                  pl.BlockSpec((tq, HEAD_DIM), lambda b, g, i: (i, 0)),
                  pl.BlockSpec((1, 1, ncp, HEAD_DIM), lambda b, g, i: (b, g, 0, 0)),
                  pl.BlockSpec((1, 1, ncp, HEAD_DIM), lambda b, g, i: (b, gw + g, 0, 0)),
                  pl.BlockSpec((T, HEAD_DIM), lambda b, g, i: (b, 2 * gw + g)),
                  pl.BlockSpec((T, HEAD_DIM), lambda b, g, i: (b, 3 * gw + g)),
                  pl.BlockSpec((T, HEAD_DIM), lambda b, g, i: (b, g)),
                  pl.BlockSpec((T, HEAD_DIM), lambda b, g, i: (b, gw + g)),
                  pl.BlockSpec((1, tq, 128), lambda b, g, i: (g, b * nt + i, 0))],
        out_specs=pl.BlockSpec((tq, NSA_REP * HEAD_DIM), lambda b, g, i: (b * nt + i, g)),
        scratch_shapes=[pltpu.VMEM((rows, 1), F32), pltpu.VMEM((rows, 1), F32), pltpu.VMEM((rows, HEAD_DIM), F32)],
        compiler_params=_cp("parallel", "parallel", "parallel"),
        name="nsa_prefill",
    )(zq, cos, sin, kvc, kvc, new_kv, new_kv, win, win, gates4)


def _nsa_dec_sel_kernel(q_ref, kc_ref, vc_ref, ocmp_ref, sel_ref, *, tp, n_cmp, n_slc, q_start):
    scale = HEAD_DIM ** -0.5
    rep = NSA_REP
    ncp = kc_ref.shape[2]
    nsp = -(-n_slc // 128) * 128
    qpos1 = q_start + lax.broadcasted_iota(I32, (tp, 1), 0)
    qpos = jnp.concatenate([qpos1] * rep, axis=0)
    cidx = lax.broadcasted_iota(I32, (1, ncp), 1)
    cmask = (cidx * CMP_STRIDE + CMP_BLOCK - 1 <= qpos) & (cidx < n_cmp)
    mm = _cmp_to_slc_mask(ncp, nsp, n_cmp, n_slc)
    blk = lax.broadcasted_iota(I32, (1, nsp), 1)
    lane16 = lax.broadcasted_iota(I32, (tp, 128), 1)
    for g in range(NSA_KV):
        c0 = g * rep * HEAD_DIM
        Q = jnp.concatenate([q_ref[0, :, c0 + r * HEAD_DIM:c0 + (r + 1) * HEAD_DIM] for r in range(rep)], axis=0)
        lg = _dot_nt(Q.astype(BF16), kc_ref[0, g].astype(BF16)) * scale
        p = _masked_softmax(lg, cmask)
        ocmp_ref[0, g] = _dot(p.astype(BF16), vc_ref[0, g].astype(BF16))
        psum = p[0:tp]
        for r in range(1, rep):
            psum = psum + p[r * tp:(r + 1) * tp]
        imp = _split3_dot(psum, mm)
        forced = (blk == 0) | (blk == jnp.right_shift(qpos1, SLC_SHIFT))
        future = blk * SLC_BLOCK > qpos1
        imp = jnp.where(forced, FORCE_SCORE, jnp.where(future, NEG_INF, imp))
        taken = jnp.broadcast_to(blk >= n_slc, (tp, nsp))
        sel = jnp.zeros((tp, 128), I32)
        for it in range(min(TOP_BLOCKS, n_slc)):
            cand = jnp.where(taken, NEG_INF, imp)
            mx = jnp.max(cand, -1, keepdims=True)
            hit = (cand == mx) & jnp.logical_not(taken)
            idx = jnp.min(jnp.where(hit, blk, nsp), -1, keepdims=True)
            taken = taken | (blk == idx)
            sel = jnp.where(lane16 == it, idx, sel)
        sel_ref[0, g] = sel


def _nsa_dec_sel(zq3, kvc, n_cmp, n_slc, q_start):
    B, tp, _ = zq3.shape
    ncp = kvc.shape[2]
    kern = functools.partial(_nsa_dec_sel_kernel, tp=tp, n_cmp=n_cmp, n_slc=n_slc, q_start=q_start)
    return pl.pallas_call(
        kern,
        out_shape=(jax.ShapeDtypeStruct((B, NSA_KV, NSA_REP * tp, HEAD_DIM), F32),
                   jax.ShapeDtypeStruct((B, NSA_KV, tp, 128), I32)),
        grid=(B,),
        in_specs=[pl.BlockSpec((1, tp, NSA_HEADS * HEAD_DIM), lambda b: (b, 0, 0)),
                  pl.BlockSpec((1, NSA_KV, ncp, HEAD_DIM), lambda b: (b, 0, 0, 0)),
                  pl.BlockSpec((1, NSA_KV, ncp, HEAD_DIM), lambda b: (b, 1, 0, 0))],
        out_specs=(pl.BlockSpec((1, NSA_KV, NSA_REP * tp, HEAD_DIM), lambda b: (b, 0, 0, 0)),
                   pl.BlockSpec((1, NSA_KV, tp, 128), lambda b: (b, 0, 0, 0))),
        compiler_params=_cp("parallel"),
        name="nsa_dec_sel",
    )(zq3, kvc, kvc)


def _nsa_dec_attn_kernel(pt_ref, sel_ref, q_ref, cos_ref, sin_ref, *refs, tp, tv, n_past_blk, q_start, ktop):
    nb = NSA_KV * tv
    blk_refs = refs[:nb]
    nkv_ref, wst_ref, win_ref, ocmp_ref, gt_ref, o_ref, qr_s, m_s, l_s, acc_s = refs[nb:]
    b, k = pl.program_id(0), pl.program_id(1)
    rep = NSA_REP
    scale = HEAD_DIM ** -0.5
    rows = rep * tp
    gw = NSA_KV * HEAD_DIM
    trow = lax.broadcasted_iota(I32, (rows, 1), 0) & (tp - 1)
    qpos = q_start + trow

    @pl.when(k == 0)
    def _():
        cos = jnp.concatenate([cos_ref[...]] * rep, axis=0)
        sin = jnp.concatenate([sin_ref[...]] * rep, axis=0)
        for g in range(NSA_KV):
            c0 = g * rep * HEAD_DIM
            Q = jnp.concatenate([q_ref[0, :, c0 + r * HEAD_DIM:c0 + (r + 1) * HEAD_DIM] for r in range(rep)], axis=0)
            qr_s[g] = _rope(Q, cos, sin)
        m_s[...] = jnp.full_like(m_s, NEG_INF)
        l_s[...] = jnp.zeros_like(l_s)
        acc_s[...] = jnp.zeros_like(acc_s)

    def online_update(g, lg, mask, v):
        lg = jnp.where(mask, lg, NEG_INF)
        m_old = m_s[g]
        m_new = jnp.maximum(m_old, jnp.max(lg, -1, keepdims=True))
        m_safe = jnp.where(m_new > NEG_INF, m_new, 0.0)
        a = jnp.exp(m_old - m_safe)
        e = jnp.exp(lg - m_safe)
        l_s[g] = a * l_s[g] + jnp.sum(e, -1, keepdims=True)
        acc_s[g] = a * acc_s[g] + _dot(e.astype(BF16), v)
        m_s[g] = m_new

    col = lax.broadcasted_iota(I32, (1, tv * SLC_BLOCK), 1)
    cblk = jnp.right_shift(col, SLC_SHIFT)
    far = 1 << 24
    for g in range(NSA_KV):
        Qrb = qr_s[g].astype(BF16)
        kb = jnp.concatenate([blk_refs[g * tv + t][:, g, :].astype(BF16) for t in range(tv)], axis=0)
        vb = jnp.concatenate([blk_refs[g * tv + t][:, NSA_KV + g, :].astype(BF16) for t in range(tv)], axis=0)
        lg = _dot_nt(Qrb, kb) * scale
        mask = None
        for t in range(tv):
            s = sel_ref[b, g, t, k]
            s = jnp.where(s < n_past_blk, s, far)
            kpos = s * SLC_BLOCK + (col - t * SLC_BLOCK)
            mt = (cblk == t) & (trow == t) & (kpos <= qpos)
            mask = mt if mask is None else (mask | mt)
        online_update(g, lg, mask, vb)

    @pl.when(k == ktop - 1)
    def _():
        jrow = lax.broadcasted_iota(I32, (1, tp), 1)
        P = wst_ref.shape[0]
        jw = lax.broadcasted_iota(I32, (1, P), 1)
        dpos_w = (qpos - (q_start - P)) - jw
        mask_w = (dpos_w >= 0) & (dpos_w <= WINDOW) & (q_start - P + jw >= 0)
        mask_n = (jrow <= trow) & (jrow < tv)
        for g in range(NSA_KV):
            Qrb = qr_s[g].astype(BF16)
            has_new = None
            for t in range(tv):
                f = sel_ref[b, g, t, 0] == n_past_blk
                for kk in range(1, ktop):
                    f = f | (sel_ref[b, g, t, kk] == n_past_blk)
                hn = trow == jnp.where(f, t, -1)
                has_new = hn if has_new is None else (has_new | hn)
            c_k = 2 * gw + g * HEAD_DIM
            c_v = 3 * gw + g * HEAD_DIM
            nk = nkv_ref[0, :, c_k:c_k + HEAD_DIM].astype(BF16)
            nv = nkv_ref[0, :, c_v:c_v + HEAD_DIM].astype(BF16)
            online_update(g, _dot_nt(Qrb, nk) * scale, has_new & mask_n, nv)
            o_s = acc_s[g] / jnp.maximum(l_s[g], 1e-30)
            wk = wst_ref[:, g, :].astype(BF16)
            wv = wst_ref[:, NSA_KV + g, :].astype(BF16)
            nwk = win_ref[0, :, g * HEAD_DIM:(g + 1) * HEAD_DIM].astype(BF16)
            nwv = win_ref[0, :, gw + g * HEAD_DIM:gw + (g + 1) * HEAD_DIM].astype(BF16)
            lw = jnp.where(mask_w, _dot_nt(Qrb, wk) * scale, NEG_INF)
            ln = jnp.where(mask_n, _dot_nt(Qrb, nwk) * scale, NEG_INF)
            mx = jnp.maximum(jnp.max(lw, -1, keepdims=True), jnp.max(ln, -1, keepdims=True))
            mx = jnp.where(mx > NEG_INF, mx, 0.0)
            ew, en = jnp.exp(lw - mx), jnp.exp(ln - mx)
            den = jnp.maximum(jnp.sum(ew, -1, keepdims=True) + jnp.sum(en, -1, keepdims=True), 1e-30)
            o_w = (_dot(ew.astype(BF16), wv) + _dot(en.astype(BF16), nwv)) / den
            gt = _sigmoid(gt_ref[g])
            o_c = ocmp_ref[0, g]
            for r in range(rep):
                rs = slice(r * tp, (r + 1) * tp)
                o = (gt[:, 3 * r:3 * r + 1] * o_c[rs] + gt[:, 3 * r + 1:3 * r + 2] * o_s[rs]
                     + gt[:, 3 * r + 2:3 * r + 3] * o_w[rs])
                c_o = (g * rep + r) * HEAD_DIM
                o_ref[0, :, c_o:c_o + HEAD_DIM] = o.astype(o_ref.dtype)


def _nsa_dec_attn(page_table, sel, zq3, cos, sin, cache4, page_off, new_kv3, wst3, win_off, win3, ocmp, gates4,
                  tv, q_start):
    B, tp, _ = zq3.shape
    bpp = PAGE_SIZE // SLC_BLOCK
    n_past_blk = page_table.shape[1] * bpp
    ktop = sel.shape[-1]
    P = wst3.shape[0] // (win_off[1])
    rows = NSA_REP * tp

    def cache_spec(g, t):
        def imap(b, k, pt, sl):
            s = jnp.minimum(sl[b, g, t, k], n_past_blk - 1)
            return ((page_off + pt[b, s // bpp]) * bpp + s % bpp, 1, 0, 0)
        return pl.BlockSpec((SLC_BLOCK, None, 2 * NSA_KV, HEAD_DIM), imap)

    cache_specs = [cache_spec(g, t) for g in range(NSA_KV) for t in range(tv)]
    kern = functools.partial(_nsa_dec_attn_kernel, tp=tp, tv=tv, n_past_blk=n_past_blk, q_start=q_start, ktop=ktop)
    whole = lambda w: pl.BlockSpec((1, tp, w), lambda b, k, pt, sl: (b, 0, 0))
    return pl.pallas_call(
        kern,
        out_shape=jax.ShapeDtypeStruct((B, tp, NSA_HEADS * HEAD_DIM), BF16),
        grid_spec=pltpu.PrefetchScalarGridSpec(
            num_scalar_prefetch=2,
            grid=(B, ktop),
            in_specs=[whole(NSA_HEADS * HEAD_DIM),
                      pl.BlockSpec((tp, HEAD_DIM), lambda b, k, pt, sl: (0, 0)),
                      pl.BlockSpec((tp, HEAD_DIM), lambda b, k, pt, sl: (0, 0))]
            + cache_specs
            + [whole(4 * NSA_KV * HEAD_DIM),
               pl.BlockSpec((P, 2 * NSA_KV, HEAD_DIM), lambda b, k, pt, sl: (win_off[0] + b, 0, 0)),
               whole(2 * NSA_KV * HEAD_DIM),
               pl.BlockSpec((1, NSA_KV, rows, HEAD_DIM), lambda b, k, pt, sl: (b, 0, 0, 0)),
               pl.BlockSpec((NSA_KV, tp, 128), lambda b, k, pt, sl: (0, b, 0))],
            out_specs=whole(NSA_HEADS * HEAD_DIM),
            scratch_shapes=[pltpu.VMEM((NSA_KV, rows, HEAD_DIM), F32), pltpu.VMEM((NSA_KV, rows, 1), F32),
                            pltpu.VMEM((NSA_KV, rows, 1), F32), pltpu.VMEM((NSA_KV, rows, HEAD_DIM), F32)],
        ),
        compiler_params=_cp("parallel", "arbitrary"),
        name="nsa_dec_attn",
    )(page_table, sel, zq3, cos, sin, *([cache4] * len(cache_specs)), new_kv3, wst3, win3, ocmp, gates4)


def _gmlp_kernel(z_ref, g_ref, b_ref, ws_ref, bst_ref, o_ref, v_ref, *, lc):
    z = _gelu(z_ref[...])
    u, v = z[:, :W_B], z[:, W_B:]
    mu = jnp.mean(v, -1, keepdims=True)
    d = v - mu
    var = jnp.mean(d * d, -1, keepdims=True)
    vn = d * lax.rsqrt(var + 1e-5) * g_ref[...] + b_ref[...]
    v_ref[...] = vn
    vb = vn.astype(BF16)
    ti = lax.broadcasted_iota(I32, (lc, lc), 0)
    si = lax.broadcasted_iota(I32, (lc, lc), 1)
    for g in range(GMLP_GROUPS):
        cs = slice(g * GMLP_GDIM, (g + 1) * GMLP_GDIM)
        w = jnp.where(si <= ti, ws_ref[g], 0.0).astype(BF16)
        mixed = _dot(w, vb[:, cs]) + bst_ref[:, g:g + 1]
        o_ref[:, cs] = (u[:, cs] * mixed).astype(o_ref.dtype)


def _gmlp(zuv, ln_g, ln_b, ws, bs, lc):
    M = zuv.shape[0]
    ws = ws[:, :lc, :lc]
    bst = bs[:, :lc].T
    return pl.pallas_call(
        functools.partial(_gmlp_kernel, lc=lc),
        out_shape=(jax.ShapeDtypeStruct((M, W_B), BF16), jax.ShapeDtypeStruct((M, W_B), F32)),
        grid=(M // lc,),
        in_specs=[pl.BlockSpec((lc, 2 * W_B), lambda i: (i, 0)),
                  pl.BlockSpec((1, W_B), lambda i: (0, 0)),
                  pl.BlockSpec((1, W_B), lambda i: (0, 0)),
                  pl.BlockSpec((GMLP_GROUPS, lc, lc), lambda i: (0, 0, 0)),
                  pl.BlockSpec((lc, GMLP_GROUPS), lambda i: (0, 0))],
        out_specs=(pl.BlockSpec((lc, W_B), lambda i: (i, 0)), pl.BlockSpec((lc, W_B), lambda i: (i, 0))),
        compiler_params=_cp("parallel"),
        name="gmlp",
    )(zuv, ln_g.reshape(1, W_B), ln_b.reshape(1, W_B), ws, bst)


def _mlstm_kernel(bi_ref, bf_ref, q_ref, k_ref, v_ref, og_ref, sm_ref, smt_ref, ng_ref, c0_ref, n0_ref, m0_ref,
                  h_ref, c_ref, n_ref, m_ref, c_s, n_s, m_s, *, lc, valid, ci_col, cf_col):
    c = pl.program_id(1)

    @pl.when(c == 0)
    def _():
        c_s[...] = c0_ref[0]
        n_s[...] = n0_ref[0]
        m_s[...] = m0_ref[0]

    ti = lax.broadcasted_iota(I32, (lc, lc), 0)
    si = lax.broadcasted_iota(I32, (lc, lc), 1)
    tril = (si <= ti) & (si < valid)
    rvalid = lax.broadcasted_iota(I32, (lc, 1), 0) < valid
    last = valid - 1
    kscale = MLSTM_DQK ** -0.5
    for h in range(MLSTM_HEADS):
        ig_c = sm_ref[:, ci_col + h:ci_col + h + 1] + bi_ref[h]
        lf_c = _log_sigmoid(sm_ref[:, cf_col + h:cf_col + h + 1] + bf_ref[h])
        ig_r = smt_ref[0, h:h + 1, :] + bi_ref[h]
        lf_r = _log_sigmoid(smt_ref[0, MLSTM_HEADS + h:MLSTM_HEADS + h + 1, :] + bf_ref[h])
        F_c = jnp.sum(jnp.where(si <= ti, lf_r, 0.0), axis=1, keepdims=True)
        F_r = jnp.sum(jnp.where(ti <= si, lf_c, 0.0), axis=0, keepdims=True)
        D = jnp.where(tril, F_c - F_r + ig_r, NEG_INF)
        m_prev = m_s[h:h + 1, 0:1]
        m_inter = F_c + m_prev
        m_t = jnp.maximum(m_inter, jnp.max(D, axis=1, keepdims=True))
        qh = q_ref[:, h * MLSTM_DQK:(h + 1) * MLSTM_DQK]
        ks = k_ref[:, h * MLSTM_DQK:(h + 1) * MLSTM_DQK] * kscale
        vh = v_ref[:, h * MLSTM_DV:(h + 1) * MLSTM_DV]
        qb, kb, vb = qh.astype(BF16), ks.astype(BF16), vh.astype(BF16)
        S = _dot_nt(qb, kb) * jnp.exp(D - m_t)
        dec = jnp.exp(m_inter - m_t)
        C = c_s[h]
        n_row = n_s[h:h + 1, :]
        num = _dot(S.astype(BF16), vb) + dec * _dot_nt(qb, C.astype(BF16))
        den = jnp.sum(S, axis=1, keepdims=True) + dec * jnp.sum(qh * n_row, axis=1, keepdims=True)
        hh = num / jnp.maximum(jnp.abs(den), 1.0)
        m_new = m_t[last:last + 1, :]
        F_last = F_c[last:last + 1, :]
        wl_c = jnp.where(rvalid, jnp.exp(F_last - F_c + ig_c - m_new), 0.0)
        dl = jnp.exp(F_last + m_prev - m_new)
        c_s[h] = dl * C + _dot_tn((vh * wl_c).astype(BF16), kb)
        n_s[h:h + 1, :] = dl * n_row + jnp.sum(wl_c * ks, axis=0, keepdims=True)
        m_s[h:h + 1, :] = jnp.broadcast_to(m_new, (1, 128))
        mu = jnp.mean(hh, -1, keepdims=True)
        d = hh - mu
        var = jnp.mean(d * d, -1, keepdims=True)
        hn = d * lax.rsqrt(var + 1e-5) * ng_ref[h:h + 1, :]
        vs = slice(h * MLSTM_DV, (h + 1) * MLSTM_DV)
        h_ref[:, vs] = (_sigmoid(og_ref[:, vs]) * hn).astype(h_ref.dtype)

    @pl.when(c == pl.num_programs(1) - 1)
    def _():
        c_ref[0] = c_s[...]
        n_ref[0] = n_s[...]
        m_ref[0] = m_s[...]


def _mlstm(zcq, zck, zcv, zco, zsm, b_i, b_f, norm_g, C0, n0, m0, B, T, lc, valid, ci_col, cf_col):
    nc = T // lc
    H, DQ, DV = MLSTM_HEADS, MLSTM_DQK, MLSTM_DV
    smt = jnp.concatenate([zsm[:, ci_col:ci_col + H], zsm[:, cf_col:cf_col + H]], -1)
    smt = smt.reshape(B * nc, lc, 2 * H).transpose(0, 2, 1)
    m0b = jnp.broadcast_to(m0[..., None], (B, H, 128))
    row = lambda w: pl.BlockSpec((lc, w), lambda b, c: (b * nc + c, 0))
    smem = pl.BlockSpec(memory_space=pltpu.SMEM)
    kern = functools.partial(_mlstm_kernel, lc=lc, valid=valid, ci_col=ci_col, cf_col=cf_col)
    c_spec = pl.BlockSpec((1, H, DV, DQ), lambda b, c: (b, 0, 0, 0))
    n_spec = pl.BlockSpec((1, H, DQ), lambda b, c: (b, 0, 0))
    h, C, n, m = pl.pallas_call(
        kern,
        out_shape=(jax.ShapeDtypeStruct((B * T, H * DV), BF16), jax.ShapeDtypeStruct((B, H, DV, DQ), F32),
                   jax.ShapeDtypeStruct((B, H, DQ), F32), jax.ShapeDtypeStruct((B, H, 128), F32)),
        grid=(B, nc),
        in_specs=[smem, smem, row(H * DQ), row(H * DQ), row(H * DV), row(H * DV), row(128),
                  pl.BlockSpec((1, 2 * H, lc), lambda b, c: (b * nc + c, 0, 0)),
                  pl.BlockSpec((H, DV), lambda b, c: (0, 0)), c_spec, n_spec, n_spec],
        out_specs=(row(H * DV), c_spec, n_spec, n_spec),
        scratch_shapes=[pltpu.VMEM((H, DV, DQ), F32), pltpu.VMEM((H, DQ), F32), pltpu.VMEM((H, 128), F32)],
        compiler_params=_cp("parallel", "arbitrary"),
        name="mlstm",
    )(b_i, b_f, zcq, zck, zcv, zco, zsm, smt, norm_g, C0, n0, m0b)
    return h, C, n, m[..., 0]


def _merge_kernel(a_ref, b_ref, c_ref, wa_ref, wb_ref, wc_ref, ga_ref, gb_ref, gc_ref, o_ref):
    o = (_sigmoid(ga_ref[...]) * _dot(a_ref[...], wa_ref[...])
         + _sigmoid(gb_ref[...]) * _dot(b_ref[...], wb_ref[...])
         + _sigmoid(gc_ref[...]) * _dot(c_ref[...], wc_ref[...]))
    o_ref[...] = o.astype(o_ref.dtype)


def _merge(oa, ob, oc, wa, wb, wc, layer, zmg, tm=512, tn=512):
    M, K = oa.shape
    D = wa.shape[-1]
    tm, tn = min(tm, M), min(tn, D)
    nj = D // tn
    x_spec = pl.BlockSpec((tm, K), lambda i, j: (i, 0))
    w_spec = _w_spec(wa, layer, (K, tn), lambda i, j: (0, j))
    g_spec = lambda o: pl.BlockSpec((tm, tn), lambda i, j: (i, o * nj + j))
    return pl.pallas_call(
        _merge_kernel,
        out_shape=jax.ShapeDtypeStruct((M, D), BF16),
        grid=(M // tm, nj),
        in_specs=[x_spec, x_spec, x_spec, w_spec, w_spec, w_spec, g_spec(0), g_spec(1), g_spec(2)],
        out_specs=pl.BlockSpec((tm, tn), lambda i, j: (i, j)),
        compiler_params=_cp("parallel", "parallel"),
        name="merge",
    )(oa, ob, oc, wa, wb, wc, zmg, zmg, zmg)


def _top16(s, ids, big, track_pos):
    vals, idxs = [], []
    pos = jnp.full(s.shape, -1, I32) if track_pos else None
    for it in range(PEER_TOPK):
        mx = jnp.max(s, axis=0, keepdims=True)
        idx = jnp.min(jnp.where(s == mx, ids, big), axis=0, keepdims=True)
        one = ids == idx
        s = jnp.where(one, NEG_INF, s)
        if track_pos:
            pos = jnp.where(one, it, pos)
        vals.append(mx)
        idxs.append(idx)
    return vals, idxs, pos


def _peer_route_kernel(q_ref, k1_ref, k2_ref, s1_ref, s2_ref, pk_ref):
    tm = q_ref.shape[0]
    half = PEER_DQ // 2
    K = PEER_TOPK
    k1 = k1_ref[...].astype(BF16)
    k2 = k2_ref[...].astype(BF16)
    kid = lax.broadcasted_iota(I32, (N_KEYS, 1), 0)
    i16 = lax.broadcasted_iota(I32, (K, 1), 0)
    i8 = lax.broadcasted_iota(I32, (8, 1), 0)
    cid = jnp.concatenate([i16, K + i8, 2 * K + i8, 3 * K + i8, i8 * K, i8 * K + 1, i8 * K + 2, (8 + i8) * K], axis=0)
    for h in range(PEER_HEADS):
        q1 = q_ref[:, h * PEER_DQ:h * PEER_DQ + half].astype(BF16)
        q2 = q_ref[:, h * PEER_DQ + half:(h + 1) * PEER_DQ].astype(BF16)
        s1 = _dot_nt(k1, q1)
        s2 = _dot_nt(k2, q2)
        v1, _, pos1 = _top16(s1, kid, N_KEYS, True)
        v2, _, pos2 = _top16(s2, kid, N_KEYS, True)
        v1m = jnp.concatenate(v1, axis=0)
        v2m = jnp.concatenate(v2, axis=0)
        pieces = [v1[0] + v2m, v1[1] + v2m[0:8], v1[2] + v2m[0:8], v1[3] + v2m[0:8]]
        for kk in range(3):
            pieces.append(jnp.where(i8 >= 4, v1m[0:8] + v2[kk], NEG_INF))
        pieces.append(v1m[8:16] + v2[0])
        cv, cj, _ = _top16(jnp.concatenate(pieces, axis=0), cid, K * K, False)
        z = jnp.zeros((1, tm), F32)
        wbits = jnp.zeros((K, tm), I32)
        for it in range(K):
            z = z + jnp.exp(cv[it] - cv[0])
            ii = jnp.right_shift(cj[it], 4)
            kk = cj[it] & (K - 1)
            wbits = jnp.where(i16 == ii, wbits | jnp.left_shift(1, kk), wbits)
        in1, in2 = pos1 >= 0, pos2 >= 0
        s1_ref[h] = jnp.where(in1, (s1 - cv[0]) * LOG2E - jnp.log2(z), NEG_INF)
        s2_ref[h] = jnp.where(in2, s2 * LOG2E, NEG_INF)
        rw = jnp.zeros((N_KEYS, tm), I32)
        for i in range(K):
            rw = jnp.where(pos1 == i, wbits[i:i + 1, :], rw)
        b2 = jnp.where(in2, jnp.left_shift(1, jnp.maximum(pos2, 0)), 0)
        pk_ref[h] = b2 | jnp.left_shift(rw, 16)


def _peer_route(q, k1, k2, tm=256):
    M = q.shape[0]
    tm = min(tm, M)
    spec = pl.BlockSpec((PEER_HEADS, N_KEYS, tm), lambda i: (0, 0, i))
    shp = lambda dt: jax.ShapeDtypeStruct((PEER_HEADS, N_KEYS, M), dt)
    return pl.pallas_call(
        _peer_route_kernel,
        out_shape=(shp(F32), shp(F32), shp(I32)),
        grid=(M // tm,),
        in_specs=[pl.BlockSpec((tm, PEER_HEADS * PEER_DQ), lambda i: (i, 0)),
                  pl.BlockSpec((N_KEYS, PEER_DQ // 2), lambda i: (0, 0)),
                  pl.BlockSpec((N_KEYS, PEER_DQ // 2), lambda i: (0, 0))],
        out_specs=(spec, spec, spec),
        compiler_params=_cp("parallel"),
        name="peer_route",
    )(q, k1, k2)


PEER_SUB = 256


def _peer_dense_kernel(x_ref, u_ref, v_ref, s1_ref, s2_ref, pk_ref, o_ref, *, te):
    e = pl.program_id(1)

    @pl.when(e == 0)
    def _():
        o_ref[...] = jnp.zeros_like(o_ref)

    x = x_ref[...]
    hs = [_dot_nt(x, u_ref[c * PEER_SUB:(c + 1) * PEER_SUB, :]) for c in range(te // PEER_SUB)]
    per = te // N_KEYS
    gates = []
    for al in range(per):
        a = e * per + al
        gate = None
        for hd in range(PEER_HEADS):
            wt = jnp.exp2(s1_ref[hd, pl.ds(a, 1), :] + s2_ref[hd])
            arow = pk_ref[hd, pl.ds(a, 1), :]
            abits = lax.shift_right_logical(arow, jnp.full_like(arow, 16))
            g = jnp.where((abits & pk_ref[hd]) != 0, wt, 0.0)
            gate = g if gate is None else gate + g
        gates.append(gate.T)
    gps = PEER_SUB // N_KEYS
    acts = [(_gelu(hs[c]) * jnp.concatenate(gates[c * gps:(c + 1) * gps], axis=1)).astype(BF16)
            for c in range(te // PEER_SUB)]
    o_ref[...] += _dot(jnp.concatenate(acts, axis=1), v_ref[...])


def _peer_dense(xb, u, v, layer, s1l, s2l, pk, tm=512, te=512):
    M, D = xb.shape
    NE = u.shape[-2]
    tm = min(tm, M)
    n_e = NE // te
    once = pl.Buffered(buffer_count=1)
    rspec = pl.BlockSpec((PEER_HEADS, N_KEYS, tm), lambda i, e: (0, 0, i), pipeline_mode=once)
    return pl.pallas_call(
        functools.partial(_peer_dense_kernel, te=te),
        out_shape=jax.ShapeDtypeStruct((M, D), F32),
        grid=(M // tm, n_e),
        in_specs=[pl.BlockSpec((tm, D), lambda i, e: (i, 0), pipeline_mode=once),
                  _w_spec(u, layer, (te, D), lambda i, e: (e, 0)),
                  _w_spec(v, layer, (te, D), lambda i, e: (e, 0)),
                  rspec, rspec, rspec],
        out_specs=pl.BlockSpec((tm, D), lambda i, e: (i, 0)),
        compiler_params=_cp("parallel", "arbitrary"),
        name="peer_dense",
    )(xb, u, v, s1l, s2l, pk)


def _peer(x1, x1b, lw, alpha):
    M, D = x1.shape
    mp = -(-M // 128) * 128
    xb = x1b if mp == M else jnp.pad(x1b, ((0, mp - M), (0, 0)))
    l = lw["layer"]
    q = _mm(xb, lw["peer_wq"], l)
    s1l, s2l, pk = _peer_route(q, lw["peer_k1"], lw["peer_k2"])
    y = _peer_dense(xb, lw["peer_u"], lw["peer_v"], l, s1l, s2l, pk)[:M]
    return _res_ln(x1, y, lw["ln2_g"], lw["ln2_b"], alpha)


Z_SEGS = ("a_q", "a_kv", "a_g", "b_uv", "c_q", "c_k", "c_v", "c_i", "c_f", "c_o", "mg")
STACKED_BF16 = ("w_br_a", "w_br_b", "w_br_c", "w_out", "peer_wq", "peer_u", "peer_v")
GATE_COL = 0
CI_COL = 3 * NSA_HEADS
CF_COL = CI_COL + MLSTM_HEADS


def _prep_layer_weights(l, w):
    D = w["w_in"].shape[1]
    sizes = (NSA_HEADS * HEAD_DIM, 6 * NSA_KV * HEAD_DIM, 3 * NSA_HEADS, 2 * W_B,
             MLSTM_HEADS * MLSTM_DQK, MLSTM_HEADS * MLSTM_DQK, MLSTM_HEADS * MLSTM_DV, MLSTM_HEADS, MLSTM_HEADS,
             MLSTM_HEADS * MLSTM_DV, 3 * D)
    offs = np.concatenate([[0], np.cumsum(sizes)])
    w_in = w["w_in"][l]
    seg = {n: (int(offs[i]), int(offs[i + 1])) for i, n in enumerate(Z_SEGS)}
    cut = lambda n: w_in[:, seg[n][0]:seg[n][1]].astype(BF16)
    lw = {n: cut(n) for n in ("a_q", "a_kv", "b_uv", "c_q", "c_k", "c_v", "c_o", "mg")}
    small = jnp.concatenate([w_in[:, seg[n][0]:seg[n][1]] for n in ("a_g", "c_i", "c_f")], -1)
    lw["small"] = jnp.pad(small, ((0, 0), (0, 128 - small.shape[1]))).astype(BF16)
    lw["layer"] = l
    for n in STACKED_BF16:
        lw[n] = w[n + "_bf16"]
    for n in ("cmp_pe", "cmp_w1", "cmp_b1", "cmp_w2", "gmlp_ln_g", "gmlp_ln_b", "gmlp_ws", "gmlp_bs",
              "mlstm_b_i", "mlstm_b_f", "mlstm_norm_g", "ln1_g", "ln1_b", "peer_k1", "peer_k2", "ln2_g", "ln2_b"):
        lw[n] = w[n][l]
    return lw


def _gates_by_group(zsm):
    M = zsm.shape[0]
    gt = zsm[:, GATE_COL:GATE_COL + 3 * NSA_HEADS].reshape(M, NSA_KV, 3 * NSA_REP).transpose(1, 0, 2)
    return jnp.pad(gt, ((0, 0), (0, 0), (0, 128 - 3 * NSA_REP)))


def _layer(x, xb, B, T, tv, lw, alpha, dec):
    M, D = x.shape
    z = {n: _mm(xb, lw[n]) for n in ("a_q", "a_kv", "b_uv", "c_q", "c_k", "c_v", "c_o", "mg", "small")}
    zsm = z["small"]
    q_start = 0 if dec is None else dec["q_start"]
    cos, sin = _rope_tables(q_start + jnp.arange(T))
    new_kv, win = _kvprep(z["a_kv"], cos, sin, T)
    gates4 = _gates_by_group(zsm)
    cmp_w = (lw["cmp_pe"], lw["cmp_w1"], lw["cmp_b1"], lw["cmp_w2"])
    gw = NSA_KV * HEAD_DIM
    if dec is None:
        n_pages = T // PAGE_SIZE
        ident = jnp.arange(B * n_pages, dtype=I32).reshape(B, n_pages)
        src3 = z["a_kv"][:, :2 * gw].reshape(M, 2 * NSA_KV, HEAD_DIM)
        kvc = _compress(src3, ident, 0, *cmp_w)
        o_a = _nsa_prefill(z["a_q"], new_kv, win, kvc, gates4, cos, sin, B, T)
        lc_g, lc_m = CHUNK, math.gcd(T, MLSTM_KERNEL_CHUNK)
        C0 = jnp.zeros((B, MLSTM_HEADS, MLSTM_DV, MLSTM_DQK), F32)
        n0 = jnp.zeros((B, MLSTM_HEADS, MLSTM_DQK), F32)
        m0 = jnp.zeros((B, MLSTM_HEADS), F32)
        valid = lc_m
    else:
        page_table = dec["page_table"]
        kvc = _compress(dec["cache3"], page_table, dec["page_off"], *cmp_w)
        L = q_start + tv
        n_cmp = (L - CMP_BLOCK) // CMP_STRIDE + 1
        n_slc = -(-L // SLC_BLOCK)
        assert n_cmp <= q_start // CMP_STRIDE - 1 and n_slc == q_start // SLC_BLOCK + 1
        assert q_start % SLC_BLOCK == 0 and tv <= T
        zq3 = z["a_q"].reshape(B, T, -1)
        ocmp, sel = _nsa_dec_sel(zq3, kvc, n_cmp, n_slc, q_start)
        sel = sel[:, :, :tv, :min(TOP_BLOCKS, n_slc)]
        o_a = _nsa_dec_attn(page_table, sel, zq3, cos, sin, dec["cache4"], dec["page_off"], new_kv.reshape(B, T, -1),
                            dec["wst3"], dec["win_off"], win.reshape(B, T, -1), ocmp, gates4, tv, q_start)
        o_a = o_a.reshape(M, -1)
        lc_g = lc_m = T
        C0, n0, m0 = dec["C"], dec["n"], dec["m"]
        valid = tv
    o_b, v_rows = _gmlp(z["b_uv"], lw["gmlp_ln_g"], lw["gmlp_ln_b"], lw["gmlp_ws"], lw["gmlp_bs"], lc_g)
    o_c, C, n, m = _mlstm(z["c_q"], z["c_k"], z["c_v"], z["c_o"], zsm, lw["mlstm_b_i"], lw["mlstm_b_f"],
                          lw["mlstm_norm_g"], C0, n0, m0, B, T, lc_m, valid, CI_COL, CF_COL)
    merged = _merge(o_a, o_b, o_c, lw["w_br_a"], lw["w_br_b"], lw["w_br_c"], lw["layer"], z["mg"])
    x1, x1b = _res_ln(x, _mm(merged, lw["w_out"], lw["layer"]), lw["ln1_g"], lw["ln1_b"], alpha)
    y, yb = _peer(x1, x1b, lw, alpha)
    return y, yb, new_kv, win, v_rows, C, n, m


def kernel(x_prompt, x_sample, cache_nsa_kv, state_nsa_win, state_mlstm_C, state_mlstm_n, state_mlstm_m,
           page_table, w_in, cmp_pe, cmp_w1, cmp_b1, cmp_w2, gmlp_ln_g, gmlp_ln_b, gmlp_ws, gmlp_bs,
           mlstm_b_i, mlstm_b_f, mlstm_norm_g, w_br_a, w_br_b, w_br_c, w_out, ln1_g, ln1_b,
           peer_wq, peer_k1, peer_k2, peer_u, peer_v, ln2_g, ln2_b):
    w = dict(w_in=w_in, cmp_pe=cmp_pe, cmp_w1=cmp_w1, cmp_b1=cmp_b1, cmp_w2=cmp_w2, gmlp_ln_g=gmlp_ln_g,
             gmlp_ln_b=gmlp_ln_b, gmlp_ws=gmlp_ws, gmlp_bs=gmlp_bs, mlstm_b_i=mlstm_b_i, mlstm_b_f=mlstm_b_f,
             mlstm_norm_g=mlstm_norm_g, w_br_a=w_br_a, w_br_b=w_br_b, w_br_c=w_br_c, w_out=w_out, ln1_g=ln1_g,
             ln1_b=ln1_b, peer_wq=peer_wq, peer_k1=peer_k1, peer_k2=peer_k2, peer_u=peer_u, peer_v=peer_v,
             ln2_g=ln2_g, ln2_b=ln2_b)
    for n in STACKED_BF16:
        w[n + "_bf16"] = w[n].astype(BF16)
    depth = w_in.shape[0]
    alpha = (2 * depth) ** 0.25
    Bp, Tp, D = x_prompt.shape
    Bs, Ts, _ = x_sample.shape
    Tpad = 8
    past_len = page_table.shape[1] * PAGE_SIZE
    n_pool = cache_nsa_kv.shape[1]
    P = state_nsa_win.shape[2]
    cache3 = cache_nsa_kv.reshape(depth * n_pool * PAGE_SIZE, 4 * NSA_KV, HEAD_DIM)
    cache4 = cache_nsa_kv.reshape(depth * n_pool * PAGE_SIZE, 2, 2 * NSA_KV, HEAD_DIM)
    wst3 = state_nsa_win.reshape(depth * Bs * P, 2 * NSA_KV, HEAD_DIM)
    yp = x_prompt.reshape(Bp * Tp, D)
    ys = jnp.pad(x_sample, ((0, 0), (0, Tpad - Ts), (0, 0))).reshape(Bs * Tpad, D)
    ypb, ysb = yp.astype(BF16), ys.astype(BF16)
    outs = [[] for _ in range(11)]
    wkeep_p = min(WINDOW, Tp)
    for l in range(depth):
        lw = _prep_layer_weights(l, w)
        yp, ypb, kv, win, _, C, n, m = _layer(yp, ypb, Bp, Tp, Tp, lw, alpha, None)
        outs[0].append(kv.reshape(Bp, Tp, 4, NSA_KV, HEAD_DIM))
        outs[2].append(win.reshape(Bp, Tp, 2, NSA_KV, HEAD_DIM)[:, Tp - wkeep_p:])
        outs[5].append(C); outs[6].append(n); outs[7].append(m)
        dec = dict(q_start=past_len, cache3=cache3, cache4=cache4, page_off=l * n_pool, page_table=page_table,
                   wst3=wst3, win_off=(l * Bs, depth * Bs), C=state_mlstm_C[l], n=state_mlstm_n[l], m=state_mlstm_m[l])
        ys, ysb, kv, win, v_rows, C, n, m = _layer(ys, ysb, Bs, Tpad, Ts, lw, alpha, dec)
        outs[1].append(kv.reshape(Bs, Tpad, 4, NSA_KV, HEAD_DIM)[:, :Ts])
        win_new = win.reshape(Bs, Tpad, 2, NSA_KV, HEAD_DIM)[:, :Ts]
        outs[3].append(jnp.concatenate([state_nsa_win[l], win_new], 1)[:, Ts:])
        outs[4].append(v_rows.reshape(Bs, Tpad, W_B)[:, :Ts])
        outs[8].append(C); outs[9].append(n); outs[10].append(m)
    y_prompt = yp.reshape(Bp, Tp, D)
    y_sample = ys.reshape(Bs, Tpad, D)[:, :Ts]
    st = [jnp.stack(o) for o in outs]
    return (y_prompt, y_sample, st[0], st[1], st[2], st[3], st[4], st[5], st[6], st[7], st[8], st[9], st[10])
```

```python
import functools
import math

import numpy as np
import jax
import jax.numpy as jnp
from jax import lax
from jax.experimental import pallas as pl
from jax.experimental.pallas import tpu as pltpu

F32 = jnp.float32
BF16 = jnp.bfloat16
I32 = jnp.int32

HEAD_DIM = 128
ROT_DIM = HEAD_DIM // 4
ROPE_THETA = 500000.0
NSA_HEADS = 16
NSA_KV = 4
NSA_REP = NSA_HEADS // NSA_KV
CMP_BLOCK = 32
CMP_STRIDE = 16
SLC_BLOCK = 64
SLC_SHIFT = 6
TOP_BLOCKS = 16
WINDOW = 512
FORCE_SCORE = 1.0e9
CHUNK = 128
GMLP_GROUPS = 16
GMLP_GDIM = 128
W_B = GMLP_GROUPS * GMLP_GDIM
MLSTM_HEADS = 8
MLSTM_DQK = 128
MLSTM_DV = 256
MLSTM_KERNEL_CHUNK = 128
PEER_HEADS = 8
PEER_DQ = 256
N_KEYS = 128
PEER_TOPK = 16
PAGE_SIZE = 128
LOG2E = 1.4426950408889634
NEG_INF = float("-inf")

VMEM_LIMIT_BYTES = 56 * 1024 * 1024


def _cp(*sem):
    return pltpu.CompilerParams(dimension_semantics=sem, vmem_limit_bytes=VMEM_LIMIT_BYTES)


def _gelu(x):
    return 0.5 * x * (1.0 + jnp.tanh(0.7978845608028654 * (x + 0.044715 * (x * x * x))))


def _sigmoid(x):
    return 1.0 / (1.0 + jnp.exp(-x))


def _log_sigmoid(x):
    return jnp.minimum(x, 0.0) - jnp.log(1.0 + jnp.exp(-jnp.abs(x)))


def _dot(a, b):
    return jnp.dot(a, b, preferred_element_type=F32)


def _dot_nt(a, b):
    return lax.dot_general(a, b, (((1,), (1,)), ((), ())), preferred_element_type=F32)


def _dot_tn(a, b):
    return lax.dot_general(a, b, (((0,), (0,)), ((), ())), preferred_element_type=F32)


def _masked_softmax(lg, mask):
    lg = jnp.where(mask, lg, NEG_INF)
    mx = jnp.max(lg, -1, keepdims=True)
    mx = jnp.where(mx > NEG_INF, mx, 0.0)
    e = jnp.exp(lg - mx)
    return e / jnp.maximum(jnp.sum(e, -1, keepdims=True), 1e-30)


def _split3_dot(p, m_bf16):
    hi = p.astype(BF16)
    r1 = p - hi.astype(F32)
    mid = r1.astype(BF16)
    lo = (r1 - mid.astype(F32)).astype(BF16)
    return _dot(hi, m_bf16) + _dot(mid, m_bf16) + _dot(lo, m_bf16)


def _rope(x, cos, sin):
    lane = lax.broadcasted_iota(I32, x.shape, 1)
    up = pltpu.roll(x, HEAD_DIM - ROT_DIM // 2, 1)
    dn = pltpu.roll(x, ROT_DIM // 2, 1)
    return x * cos + jnp.where(lane < ROT_DIM // 2, up, dn) * sin


def _rope_tables(pos):
    inv = ROPE_THETA ** (-jnp.arange(0, ROT_DIM, 2, dtype=F32) / ROT_DIM)
    ang = pos.astype(F32)[:, None] * inv[None, :]
    c, s = jnp.cos(ang), jnp.sin(ang)
    n = pos.shape[0]
    cos = jnp.concatenate([c, c, jnp.ones((n, HEAD_DIM - ROT_DIM), F32)], -1)
    sin = jnp.concatenate([-s, s, jnp.zeros((n, HEAD_DIM - ROT_DIM), F32)], -1)
    return cos, sin


def _mm_kernel(x_ref, w_ref, o_ref, acc_ref):
    k = pl.program_id(2)

    @pl.when(k == 0)
    def _():
        acc_ref[...] = jnp.zeros_like(acc_ref)

    acc_ref[...] += _dot(x_ref[...], w_ref[...])

    @pl.when(k == pl.num_programs(2) - 1)
    def _():
        o_ref[...] = acc_ref[...].astype(o_ref.dtype)


def _w_spec(w, layer, block, imap):
    if w.ndim == 2:
        return pl.BlockSpec(block, imap)
    return pl.BlockSpec((None,) + block, lambda *a: (layer,) + imap(*a))


def _mm(x, w, layer=None, out_dtype=F32, tm=1024, tn=512, tk=4096):
    M, K = x.shape
    N = w.shape[-1]
    tm, tn, tk = min(tm, M), min(tn, N), min(tk, K)
    assert M % tm == 0 and N % tn == 0 and K % tk == 0, (x.shape, w.shape)
    return pl.pallas_call(
        _mm_kernel,
        out_shape=jax.ShapeDtypeStruct((M, N), out_dtype),
        grid=(M // tm, N // tn, K // tk),
        in_specs=[pl.BlockSpec((tm, tk), lambda i, j, k: (i, k)),
                  _w_spec(w, layer, (tk, tn), lambda i, j, k: (k, j))],
        out_specs=pl.BlockSpec((tm, tn), lambda i, j, k: (i, j)),
        scratch_shapes=[pltpu.VMEM((tm, tn), F32)],
        compiler_params=_cp("parallel", "parallel", "arbitrary"),
        name="mm",
    )(x, w)


def _res_ln_kernel(x_ref, y_ref, g_ref, b_ref, o_ref, ob_ref, *, alpha):
    v = alpha * x_ref[...] + y_ref[...]
    mu = jnp.mean(v, -1, keepdims=True)
    d = v - mu
    var = jnp.mean(d * d, -1, keepdims=True)
    o = d * lax.rsqrt(var + 1e-5) * g_ref[...] + b_ref[...]
    o_ref[...] = o
    ob_ref[...] = o.astype(BF16)


def _res_ln(x, y, g, b, alpha, tm=256):
    M, D = x.shape
    tm = min(tm, M)
    row = pl.BlockSpec((tm, D), lambda i: (i, 0))
    vec = pl.BlockSpec((1, D), lambda i: (0, 0))
    return pl.pallas_call(
        functools.partial(_res_ln_kernel, alpha=alpha),
        out_shape=(jax.ShapeDtypeStruct((M, D), F32), jax.ShapeDtypeStruct((M, D), BF16)),
        grid=(M // tm,),
        in_specs=[row, row, vec, vec],
        out_specs=(row, row),
        compiler_params=_cp("parallel"),
        name="res_ln",
    )(x, y, g.reshape(1, D), b.reshape(1, D))


def _kvprep_kernel(z_ref, cos_ref, sin_ref, kv_ref, win_ref):
    cos, sin = cos_ref[...], sin_ref[...]
    gw = NSA_KV * HEAD_DIM
    kv_ref[:, 0:2 * gw] = z_ref[:, 0:2 * gw]
    kv_ref[:, 3 * gw:4 * gw] = z_ref[:, 3 * gw:4 * gw]
    win_ref[:, gw:2 * gw] = z_ref[:, 5 * gw:6 * gw]
    for g in range(NSA_KV):
        a = 2 * gw + g * HEAD_DIM
        kv_ref[:, a:a + HEAD_DIM] = _rope(z_ref[:, a:a + HEAD_DIM], cos, sin)
        a = 4 * gw + g * HEAD_DIM
        win_ref[:, g * HEAD_DIM:(g + 1) * HEAD_DIM] = _rope(z_ref[:, a:a + HEAD_DIM], cos, sin)


def _kvprep(zkv, cos, sin, T, tm=256):
    M = zkv.shape[0]
    tm = min(tm, T)
    nt = T // tm
    gw = NSA_KV * HEAD_DIM
    return pl.pallas_call(
        _kvprep_kernel,
        out_shape=(jax.ShapeDtypeStruct((M, 4 * gw), F32), jax.ShapeDtypeStruct((M, 2 * gw), F32)),
        grid=(M // tm,),
        in_specs=[pl.BlockSpec((tm, 6 * gw), lambda i: (i, 0)),
                  pl.BlockSpec((tm, HEAD_DIM), lambda i: (i % nt, 0)),
                  pl.BlockSpec((tm, HEAD_DIM), lambda i: (i % nt, 0))],
        out_specs=(pl.BlockSpec((tm, 4 * gw), lambda i: (i, 0)),
                   pl.BlockSpec((tm, 2 * gw), lambda i: (i, 0))),
        compiler_params=_cp("parallel"),
        name="kvprep",
    )(zkv, cos, sin)


CMP_PAGES = 8
PREFILL_SPAN = 512


def _cmp_bias_kernel(pe_ref, w1f_ref, b1_ref, o_ref):
    for kv in range(2):
        b = b1_ref[kv:kv + 1, :]
        for half in range(2):
            b = b + _dot(pe_ref[kv, half].astype(BF16), w1f_ref[kv, half])[0:1, :]
        o_ref[:, kv * HEAD_DIM:(kv + 1) * HEAD_DIM] = jnp.broadcast_to(b, (8, HEAD_DIM))


def _cmp_kernel(pt_ref, *refs, P):
    page_refs = refs[:P + 1]
    wa_ref, wb_ref, bias_ref, w2_ref, o_ref, res_s = refs[P + 1:]
    hp = PAGE_SIZE // CMP_STRIDE
    sg = 2 * NSA_KV
    rows = P * hp * sg
    acc_a = jnp.zeros((rows, 2 * HEAD_DIM), F32)
    acc_b = jnp.zeros((rows, 2 * HEAD_DIM), F32)
    for j in range(CMP_STRIDE):
        la, lb = [], []
        for p in range(P):
            la.append(page_refs[p][pl.ds(j, hp, stride=CMP_STRIDE), 0:sg, :].reshape(hp * sg, HEAD_DIM))
            lb.append(page_refs[p][pl.ds(j + CMP_STRIDE, hp - 1, stride=CMP_STRIDE), 0:sg, :]
                      .reshape((hp - 1) * sg, HEAD_DIM))
            lb.append(page_refs[p + 1][j, 0:sg, :])
        acc_a = acc_a + _dot(jnp.concatenate(la, axis=0).astype(BF16), wa_ref[j])
        acc_b = acc_b + _dot(jnp.concatenate(lb, axis=0).astype(BF16), wb_ref[j])
    h = _gelu(acc_a + acc_b + bias_ref[0:1, :])
    ok = _dot(h[:, :HEAD_DIM].astype(BF16), w2_ref[0].astype(BF16))
    ov = _dot(h[:, HEAD_DIM:].astype(BF16), w2_ref[1].astype(BF16))
    is_k = (lax.broadcasted_iota(I32, (rows, 1), 0) & (sg - 1)) < NSA_KV
    res_s[...] = jnp.where(is_k, ok, ov)
    for s in range(sg):
        o_ref[0, s] = res_s[pl.ds(s, P * hp, stride=sg), :]


def _compress(src3, page_table, page_off, cmp_pe, cmp_w1, cmp_b1, cmp_w2):
    B, n_pages = page_table.shape
    P = math.gcd(CMP_PAGES, n_pages)
    hp = PAGE_SIZE // CMP_STRIDE
    sg = 2 * NSA_KV
    src4 = src3.reshape(src3.shape[0], src3.shape[1] // sg, sg, HEAD_DIM)
    pe_flat = jnp.broadcast_to(cmp_pe.reshape(2, 2, 1, CMP_STRIDE * HEAD_DIM), (2, 2, 8, CMP_STRIDE * HEAD_DIM))
    w1_flat = cmp_w1.reshape(2, 2, CMP_STRIDE * HEAD_DIM, HEAD_DIM).astype(BF16)
    bias = pl.pallas_call(
        _cmp_bias_kernel,
        out_shape=jax.ShapeDtypeStruct((8, 2 * HEAD_DIM), F32),
        name="cmp_bias",
    )(pe_flat, w1_flat, cmp_b1)
    wa = jnp.concatenate([cmp_w1[0, :CMP_STRIDE], cmp_w1[1, :CMP_STRIDE]], -1).astype(BF16)
    wb = jnp.concatenate([cmp_w1[0, CMP_STRIDE:], cmp_w1[1, CMP_STRIDE:]], -1).astype(BF16)

    def page_spec(p):
        return pl.BlockSpec((PAGE_SIZE, None, sg, HEAD_DIM),
                            lambda b, c, pt: (page_off + pt[b, jnp.minimum(c * P + p, n_pages - 1)], 0, 0, 0))

    full = lambda shape: pl.BlockSpec(shape, lambda b, c, pt: (0,) * len(shape))
    return pl.pallas_call(
        functools.partial(_cmp_kernel, P=P),
        out_shape=jax.ShapeDtypeStruct((B, sg, n_pages * hp, HEAD_DIM), F32),
        grid_spec=pltpu.PrefetchScalarGridSpec(
            num_scalar_prefetch=1,
            grid=(B, n_pages // P),
            in_specs=[page_spec(p) for p in range(P + 1)]
            + [full(wa.shape), full(wb.shape), full(bias.shape), full(cmp_w2.shape)],
            out_specs=pl.BlockSpec((1, sg, P * hp, HEAD_DIM), lambda b, c, pt: (b, 0, c, 0)),
            scratch_shapes=[pltpu.VMEM((P * hp * sg, HEAD_DIM), F32)],
        ),
        compiler_params=_cp("parallel", "parallel"),
        name="cmp",
    )(page_table, *([src4] * (P + 1)), wa, wb, bias, cmp_w2)


def _cmp_to_slc_mask(ncp, nsp, n_cmp, n_slc):
    ci = lax.broadcasted_iota(I32, (ncp, nsp), 0)
    si = lax.broadcasted_iota(I32, (ncp, nsp), 1)
    ov = ((ci * CMP_STRIDE < (si + 1) * SLC_BLOCK) & (ci * CMP_STRIDE + CMP_BLOCK > si * SLC_BLOCK)
          & (ci < n_cmp) & (si < n_slc))
    return jnp.where(ov, 1.0, 0.0).astype(BF16)


def _nsa_prefill_kernel(q_ref, cos_ref, sin_ref, kc_ref, vc_ref, ks_ref, vs_ref, kw_ref, vw_ref, gt_ref, prev_ref,
                        o_ref, *, tq, T, n_cmp, n_slc, wlen, i0, kext):
    del prev_ref
    i = i0 + pl.program_id(2)
    t0 = i * tq
    scale = HEAD_DIM ** -0.5
    rep = NSA_REP
    q4 = q_ref[...]
    Q = jnp.concatenate([q4[:, r * HEAD_DIM:(r + 1) * HEAD_DIM] for r in range(rep)], axis=0)
    cos = jnp.concatenate([cos_ref[...]] * rep, axis=0)
    sin = jnp.concatenate([sin_ref[...]] * rep, axis=0)
    Qb = Q.astype(BF16)
    Qrb = _rope(Q, cos, sin).astype(BF16)
    qpos1 = t0 + lax.broadcasted_iota(I32, (tq, 1), 0)
    qpos = jnp.concatenate([qpos1] * rep, axis=0)
    ncp = kc_ref.shape[2]
    kc = kc_ref[0, 0].astype(BF16)
    vc = vc_ref[0, 0].astype(BF16)
    lg = _dot_nt(Qb, kc) * scale
    cidx = lax.broadcasted_iota(I32, (1, ncp), 1)
    cmask = (cidx * CMP_STRIDE + CMP_BLOCK - 1 <= qpos) & (cidx < n_cmp)
    p = _masked_softmax(lg, cmask)
    o_cmp = _dot(p.astype(BF16), vc)
    psum = p[0:tq]
    for r in range(1, rep):
        psum = psum + p[r * tq:(r + 1) * tq]
    nsp = 128
    imp = _split3_dot(psum, _cmp_to_slc_mask(ncp, nsp, n_cmp, n_slc))
    blk = lax.broadcasted_iota(I32, (1, nsp), 1)
    forced = (blk == 0) | (blk == jnp.right_shift(qpos1, SLC_SHIFT))
    future = blk * SLC_BLOCK > qpos1
    imp = jnp.where(forced, FORCE_SCORE, jnp.where(future, NEG_INF, imp))
    imp = jnp.where(blk < n_slc, imp, NEG_INF)
    rank = jnp.zeros((tq, nsp), F32)
    for s2 in range(n_slc):
        col = imp[:, s2:s2 + 1]
        beats = (col > imp) | ((col == imp) & (blk > s2))
        rank = rank + jnp.where(beats, 1.0, 0.0)
    sel = (rank < float(min(TOP_BLOCKS, n_slc))) & (blk < n_slc)
    srow = lax.broadcasted_iota(I32, (nsp, kext), 0)
    kcol = lax.broadcasted_iota(I32, (nsp, kext), 1)
    expand = jnp.where(jnp.right_shift(kcol, SLC_SHIFT) == srow, 1.0, 0.0).astype(BF16)
    selk1 = _dot(jnp.where(sel, 1.0, 0.0).astype(BF16), expand)
    selk = jnp.concatenate([selk1] * rep, axis=0)
    kpos = lax.broadcasted_iota(I32, (1, kext), 1)
    smask = (selk > 0.5) & (kpos <= qpos)
    ls = _dot_nt(Qrb, ks_ref[0:kext, :].astype(BF16)) * scale
    ps = _masked_softmax(ls, smask)
    o_s = _dot(ps.astype(BF16), vs_ref[0:kext, :].astype(BF16))
    start = pl.multiple_of(jnp.clip(t0 - WINDOW, 0, T - wlen), 128)
    kw = kw_ref[pl.ds(start, wlen), :].astype(BF16)
    vw = vw_ref[pl.ds(start, wlen), :].astype(BF16)
    dpos = qpos - (start + lax.broadcasted_iota(I32, (1, wlen), 1))
    lw = _dot_nt(Qrb, kw) * scale
    pw = _masked_softmax(lw, (dpos >= 0) & (dpos <= WINDOW))
    o_w = _dot(pw.astype(BF16), vw)
    gt = _sigmoid(gt_ref[0])
    for r in range(rep):
        rows = slice(r * tq, (r + 1) * tq)
        o = (gt[:, 3 * r:3 * r + 1] * o_cmp[rows] + gt[:, 3 * r + 1:3 * r + 2] * o_s[rows]
             + gt[:, 3 * r + 2:3 * r + 3] * o_w[rows])
        o_ref[:, r * HEAD_DIM:(r + 1) * HEAD_DIM] = o.astype(o_ref.dtype)


def _nsa_prefill(zq, new_kv, win, kvc, gates4, cos, sin, B, T, tq=128):
    tq = min(tq, T)
    nt = T // tq
    n_cmp = (T - CMP_BLOCK) // CMP_STRIDE + 1
    n_slc = -(-T // SLC_BLOCK)
    assert n_slc <= 128 and T % tq == 0
    wlen = min(T, WINDOW + tq)
    ncp = kvc.shape[2]
    gw = NSA_KV
    span = math.gcd(T, PREFILL_SPAN) // tq
    out = jnp.zeros((B * T, NSA_HEADS * HEAD_DIM), BF16)
    for i0 in range(0, nt, span):
        kern = functools.partial(_nsa_prefill_kernel, tq=tq, T=T, n_cmp=n_cmp, n_slc=n_slc, wlen=wlen,
                                 i0=i0, kext=(i0 + span) * tq)
        qrow = lambda b, g, i, i0=i0: (b * nt + i0 + i, g)
        out = pl.pallas_call(
            kern,
            out_shape=jax.ShapeDtypeStruct((B * T, NSA_HEADS * HEAD_DIM), BF16),
            grid=(B, NSA_KV, span),
            in_specs=[pl.BlockSpec((tq, NSA_REP * HEAD_DIM), qrow),
                      pl.BlockSpec((tq, HEAD_DIM), lambda b, g, i, i0=i0: (i0 + i, 0)),
                      pl.BlockSpec((tq, HEAD_DIM), lambda b, g, i, i0=i0: (i0 + i, 0)),
                      pl.BlockSpec((1, 1, ncp, HEAD_DIM), lambda b, g, i: (b, g, 0, 0)),
                      pl.BlockSpec((1, 1, ncp, HEAD_DIM), lambda b, g, i: (b, gw + g, 0, 0)),
                      pl.BlockSpec((T, HEAD_DIM), lambda b, g, i: (b, 2 * gw + g)),
                      pl.BlockSpec((T, HEAD_DIM), lambda b, g, i: (b, 3 * gw + g)),
                      pl.BlockSpec((T, HEAD_DIM), lambda b, g, i: (b, g)),
                      pl.BlockSpec((T, HEAD_DIM), lambda b, g, i: (b, gw + g)),
                      pl.BlockSpec((1, tq, 128), lambda b, g, i, i0=i0: (g, b * nt + i0 + i, 0)),
                      pl.BlockSpec(memory_space=pl.ANY)],
            out_specs=pl.BlockSpec((tq, NSA_REP * HEAD_DIM), qrow),
            input_output_aliases={10: 0},
            compiler_params=_cp("parallel", "parallel", "parallel"),
            name="nsa_prefill",
        )(zq, cos, sin, kvc, kvc, new_kv, new_kv, win, win, gates4, out)
    return out


def _nsa_dec_sel_kernel(q_ref, kc_ref, vc_ref, ocmp_ref, sel_ref, *, tp, n_cmp, n_slc, q_start):
    scale = HEAD_DIM ** -0.5
    rep = NSA_REP
    ncp = kc_ref.shape[2]
    nsp = -(-n_slc // 128) * 128
    qpos1 = q_start + lax.broadcasted_iota(I32, (tp, 1), 0)
    qpos = jnp.concatenate([qpos1] * rep, axis=0)
    cidx = lax.broadcasted_iota(I32, (1, ncp), 1)
    cmask = (cidx * CMP_STRIDE + CMP_BLOCK - 1 <= qpos) & (cidx < n_cmp)
    mm = _cmp_to_slc_mask(ncp, nsp, n_cmp, n_slc)
    blk = lax.broadcasted_iota(I32, (1, nsp), 1)
    lane16 = lax.broadcasted_iota(I32, (tp, 128), 1)
    for g in range(NSA_KV):
        c0 = g * rep * HEAD_DIM
        Q = jnp.concatenate([q_ref[0, :, c0 + r * HEAD_DIM:c0 + (r + 1) * HEAD_DIM] for r in range(rep)], axis=0)
        lg = _dot_nt(Q.astype(BF16), kc_ref[0, g].astype(BF16)) * scale
        p = _masked_softmax(lg, cmask)
        ocmp_ref[0, g] = _dot(p.astype(BF16), vc_ref[0, g].astype(BF16))
        psum = p[0:tp]
        for r in range(1, rep):
            psum = psum + p[r * tp:(r + 1) * tp]
        imp = _split3_dot(psum, mm)
        forced = (blk == 0) | (blk == jnp.right_shift(qpos1, SLC_SHIFT))
        future = blk * SLC_BLOCK > qpos1
        imp = jnp.where(forced, FORCE_SCORE, jnp.where(future, NEG_INF, imp))
        taken = jnp.broadcast_to(blk >= n_slc, (tp, nsp))
        sel = jnp.zeros((tp, 128), I32)
        for it in range(min(TOP_BLOCKS, n_slc)):
            cand = jnp.where(taken, NEG_INF, imp)
            mx = jnp.max(cand, -1, keepdims=True)
            hit = (cand == mx) & jnp.logical_not(taken)
            idx = jnp.min(jnp.where(hit, blk, nsp), -1, keepdims=True)
            taken = taken | (blk == idx)
            sel = jnp.where(lane16 == it, idx, sel)
        sel_ref[0, g] = sel


def _nsa_dec_sel(zq3, kvc, n_cmp, n_slc, q_start):
    B, tp, _ = zq3.shape
    ncp = kvc.shape[2]
    kern = functools.partial(_nsa_dec_sel_kernel, tp=tp, n_cmp=n_cmp, n_slc=n_slc, q_start=q_start)
    return pl.pallas_call(
        kern,
        out_shape=(jax.ShapeDtypeStruct((B, NSA_KV, NSA_REP * tp, HEAD_DIM), F32),
                   jax.ShapeDtypeStruct((B, NSA_KV, tp, 128), I32)),
        grid=(B,),
        in_specs=[pl.BlockSpec((1, tp, NSA_HEADS * HEAD_DIM), lambda b: (b, 0, 0)),
                  pl.BlockSpec((1, NSA_KV, ncp, HEAD_DIM), lambda b: (b, 0, 0, 0)),
                  pl.BlockSpec((1, NSA_KV, ncp, HEAD_DIM), lambda b: (b, 1, 0, 0))],
        out_specs=(pl.BlockSpec((1, NSA_KV, NSA_REP * tp, HEAD_DIM), lambda b: (b, 0, 0, 0)),
                   pl.BlockSpec((1, NSA_KV, tp, 128), lambda b: (b, 0, 0, 0))),
        compiler_params=_cp("parallel"),
        name="nsa_dec_sel",
    )(zq3, kvc, kvc)


def _nsa_dec_attn_kernel(pt_ref, sel_ref, q_ref, cos_ref, sin_ref, *refs, tp, tv, n_past_blk, q_start, ktop):
    nb = NSA_KV * tv
    blk_refs = refs[:nb]
    nkv_ref, wst_ref, win_ref, ocmp_ref, gt_ref, o_ref, qr_s, m_s, l_s, acc_s = refs[nb:]
    b, k = pl.program_id(0), pl.program_id(1)
    rep = NSA_REP
    scale = HEAD_DIM ** -0.5
    rows = rep * tp
    gw = NSA_KV * HEAD_DIM
    trow = lax.broadcasted_iota(I32, (rows, 1), 0) & (tp - 1)
    qpos = q_start + trow

    @pl.when(k == 0)
    def _():
        cos = jnp.concatenate([cos_ref[...]] * rep, axis=0)
        sin = jnp.concatenate([sin_ref[...]] * rep, axis=0)
        for g in range(NSA_KV):
            c0 = g * rep * HEAD_DIM
            Q = jnp.concatenate([q_ref[0, :, c0 + r * HEAD_DIM:c0 + (r + 1) * HEAD_DIM] for r in range(rep)], axis=0)
            qr_s[g] = _rope(Q, cos, sin)
        m_s[...] = jnp.full_like(m_s, NEG_INF)
        l_s[...] = jnp.zeros_like(l_s)
        acc_s[...] = jnp.zeros_like(acc_s)

    def online_update(g, lg, mask, v):
        lg = jnp.where(mask, lg, NEG_INF)
        m_old = m_s[g]
        m_new = jnp.maximum(m_old, jnp.max(lg, -1, keepdims=True))
        m_safe = jnp.where(m_new > NEG_INF, m_new, 0.0)
        a = jnp.exp(m_old - m_safe)
        e = jnp.exp(lg - m_safe)
        l_s[g] = a * l_s[g] + jnp.sum(e, -1, keepdims=True)
        acc_s[g] = a * acc_s[g] + _dot(e.astype(BF16), v)
        m_s[g] = m_new

    col = lax.broadcasted_iota(I32, (1, tv * SLC_BLOCK), 1)
    cblk = jnp.right_shift(col, SLC_SHIFT)
    far = 1 << 24
    for g in range(NSA_KV):
        Qrb = qr_s[g].astype(BF16)
        kb = jnp.concatenate([blk_refs[g * tv + t][:, g, :].astype(BF16) for t in range(tv)], axis=0)
        vb = jnp.concatenate([blk_refs[g * tv + t][:, NSA_KV + g, :].astype(BF16) for t in range(tv)], axis=0)
        lg = _dot_nt(Qrb, kb) * scale
        mask = None
        for t in range(tv):
            s = sel_ref[b, g, t, k]
            s = jnp.where(s < n_past_blk, s, far)
            kpos = s * SLC_BLOCK + (col - t * SLC_BLOCK)
            mt = (cblk == t) & (trow == t) & (kpos <= qpos)
            mask = mt if mask is None else (mask | mt)
        online_update(g, lg, mask, vb)

    @pl.when(k == ktop - 1)
    def _():
        jrow = lax.broadcasted_iota(I32, (1, tp), 1)
        P = wst_ref.shape[0]
        jw = lax.broadcasted_iota(I32, (1, P), 1)
        dpos_w = (qpos - (q_start - P)) - jw
        mask_w = (dpos_w >= 0) & (dpos_w <= WINDOW) & (q_start - P + jw >= 0)
        mask_n = (jrow <= trow) & (jrow < tv)
        for g in range(NSA_KV):
            Qrb = qr_s[g].astype(BF16)
            has_new = None
            for t in range(tv):
                f = sel_ref[b, g, t, 0] == n_past_blk
                for kk in range(1, ktop):
                    f = f | (sel_ref[b, g, t, kk] == n_past_blk)
                hn = trow == jnp.where(f, t, -1)
                has_new = hn if has_new is None else (has_new | hn)
            c_k = 2 * gw + g * HEAD_DIM
            c_v = 3 * gw + g * HEAD_DIM
            nk = nkv_ref[0, :, c_k:c_k + HEAD_DIM].astype(BF16)
            nv = nkv_ref[0, :, c_v:c_v + HEAD_DIM].astype(BF16)
            online_update(g, _dot_nt(Qrb, nk) * scale, has_new & mask_n, nv)
            o_s = acc_s[g] / jnp.maximum(l_s[g], 1e-30)
            wk = wst_ref[:, g, :].astype(BF16)
            wv = wst_ref[:, NSA_KV + g, :].astype(BF16)
            nwk = win_ref[0, :, g * HEAD_DIM:(g + 1) * HEAD_DIM].astype(BF16)
            nwv = win_ref[0, :, gw + g * HEAD_DIM:gw + (g + 1) * HEAD_DIM].astype(BF16)
            lw = jnp.where(mask_w, _dot_nt(Qrb, wk) * scale, NEG_INF)
            ln = jnp.where(mask_n, _dot_nt(Qrb, nwk) * scale, NEG_INF)
            mx = jnp.maximum(jnp.max(lw, -1, keepdims=True), jnp.max(ln, -1, keepdims=True))
            mx = jnp.where(mx > NEG_INF, mx, 0.0)
            ew, en = jnp.exp(lw - mx), jnp.exp(ln - mx)
            den = jnp.maximum(jnp.sum(ew, -1, keepdims=True) + jnp.sum(en, -1, keepdims=True), 1e-30)
            o_w = (_dot(ew.astype(BF16), wv) + _dot(en.astype(BF16), nwv)) / den
            gt = _sigmoid(gt_ref[g])
            o_c = ocmp_ref[0, g]
            for r in range(rep):
                rs = slice(r * tp, (r + 1) * tp)
                o = (gt[:, 3 * r:3 * r + 1] * o_c[rs] + gt[:, 3 * r + 1:3 * r + 2] * o_s[rs]
                     + gt[:, 3 * r + 2:3 * r + 3] * o_w[rs])
                c_o = (g * rep + r) * HEAD_DIM
                o_ref[0, :, c_o:c_o + HEAD_DIM] = o.astype(o_ref.dtype)


def _nsa_dec_attn(page_table, sel, zq3, cos, sin, cache4, page_off, new_kv3, wst3, win_off, win3, ocmp, gates4,
                  tv, q_start):
    B, tp, _ = zq3.shape
    bpp = PAGE_SIZE // SLC_BLOCK
    n_past_blk = page_table.shape[1] * bpp
    ktop = sel.shape[-1]
    P = wst3.shape[0] // (win_off[1])
    rows = NSA_REP * tp

    def cache_spec(g, t):
        def imap(b, k, pt, sl):
            s = jnp.minimum(sl[b, g, t, k], n_past_blk - 1)
            return ((page_off + pt[b, s // bpp]) * bpp + s % bpp, 1, 0, 0)
        return pl.BlockSpec((SLC_BLOCK, None, 2 * NSA_KV, HEAD_DIM), imap)

    cache_specs = [cache_spec(g, t) for g in range(NSA_KV) for t in range(tv)]
    kern = functools.partial(_nsa_dec_attn_kernel, tp=tp, tv=tv, n_past_blk=n_past_blk, q_start=q_start, ktop=ktop)
    whole = lambda w: pl.BlockSpec((1, tp, w), lambda b, k, pt, sl: (b, 0, 0))
    return pl.pallas_call(
        kern,
        out_shape=jax.ShapeDtypeStruct((B, tp, NSA_HEADS * HEAD_DIM), BF16),
        grid_spec=pltpu.PrefetchScalarGridSpec(
            num_scalar_prefetch=2,
            grid=(B, ktop),
            in_specs=[whole(NSA_HEADS * HEAD_DIM),
                      pl.BlockSpec((tp, HEAD_DIM), lambda b, k, pt, sl: (0, 0)),
                      pl.BlockSpec((tp, HEAD_DIM), lambda b, k, pt, sl: (0, 0))]
            + cache_specs
            + [whole(4 * NSA_KV * HEAD_DIM),
               pl.BlockSpec((P, 2 * NSA_KV, HEAD_DIM), lambda b, k, pt, sl: (win_off[0] + b, 0, 0)),
               whole(2 * NSA_KV * HEAD_DIM),
               pl.BlockSpec((1, NSA_KV, rows, HEAD_DIM), lambda b, k, pt, sl: (b, 0, 0, 0)),
               pl.BlockSpec((NSA_KV, tp, 128), lambda b, k, pt, sl: (0, b, 0))],
            out_specs=whole(NSA_HEADS * HEAD_DIM),
            scratch_shapes=[pltpu.VMEM((NSA_KV, rows, HEAD_DIM), F32), pltpu.VMEM((NSA_KV, rows, 1), F32),
                            pltpu.VMEM((NSA_KV, rows, 1), F32), pltpu.VMEM((NSA_KV, rows, HEAD_DIM), F32)],
        ),
        compiler_params=_cp("parallel", "arbitrary"),
        name="nsa_dec_attn",
    )(page_table, sel, zq3, cos, sin, *([cache4] * len(cache_specs)), new_kv3, wst3, win3, ocmp, gates4)


def _gmlp_kernel(z_ref, g_ref, b_ref, ws_ref, bst_ref, o_ref, v_ref, *, lc):
    z = _gelu(z_ref[...])
    u, v = z[:, :W_B], z[:, W_B:]
    mu = jnp.mean(v, -1, keepdims=True)
    d = v - mu
    var = jnp.mean(d * d, -1, keepdims=True)
    vn = d * lax.rsqrt(var + 1e-5) * g_ref[...] + b_ref[...]
    v_ref[...] = vn
    vb = vn.astype(BF16)
    ti = lax.broadcasted_iota(I32, (lc, lc), 0)
    si = lax.broadcasted_iota(I32, (lc, lc), 1)
    for g in range(GMLP_GROUPS):
        cs = slice(g * GMLP_GDIM, (g + 1) * GMLP_GDIM)
        w = jnp.where(si <= ti, ws_ref[g], 0.0).astype(BF16)
        mixed = _dot(w, vb[:, cs]) + bst_ref[:, g:g + 1]
        o_ref[:, cs] = (u[:, cs] * mixed).astype(o_ref.dtype)


def _gmlp(zuv, ln_g, ln_b, ws, bs, lc):
    M = zuv.shape[0]
    ws = ws[:, :lc, :lc]
    bst = bs[:, :lc].T
    return pl.pallas_call(
        functools.partial(_gmlp_kernel, lc=lc),
        out_shape=(jax.ShapeDtypeStruct((M, W_B), BF16), jax.ShapeDtypeStruct((M, W_B), F32)),
        grid=(M // lc,),
        in_specs=[pl.BlockSpec((lc, 2 * W_B), lambda i: (i, 0)),
                  pl.BlockSpec((1, W_B), lambda i: (0, 0)),
                  pl.BlockSpec((1, W_B), lambda i: (0, 0)),
                  pl.BlockSpec((GMLP_GROUPS, lc, lc), lambda i: (0, 0, 0)),
                  pl.BlockSpec((lc, GMLP_GROUPS), lambda i: (0, 0))],
        out_specs=(pl.BlockSpec((lc, W_B), lambda i: (i, 0)), pl.BlockSpec((lc, W_B), lambda i: (i, 0))),
        compiler_params=_cp("parallel"),
        name="gmlp",
    )(zuv, ln_g.reshape(1, W_B), ln_b.reshape(1, W_B), ws, bst)


def _mlstm_kernel(bi_ref, bf_ref, q_ref, k_ref, v_ref, og_ref, sm_ref, smt_ref, ng_ref, c0_ref, n0_ref, m0_ref,
                  h_ref, c_ref, n_ref, m_ref, c_s, n_s, m_s, *, lc, valid, ci_col, cf_col):
    c = pl.program_id(1)

    @pl.when(c == 0)
    def _():
        c_s[...] = c0_ref[0]
        n_s[...] = n0_ref[0]
        m_s[...] = m0_ref[0]

    ti = lax.broadcasted_iota(I32, (lc, lc), 0)
    si = lax.broadcasted_iota(I32, (lc, lc), 1)
    tril = (si <= ti) & (si < valid)
    rvalid = lax.broadcasted_iota(I32, (lc, 1), 0) < valid
    last = valid - 1
    kscale = MLSTM_DQK ** -0.5
    for h in range(MLSTM_HEADS):
        ig_c = sm_ref[:, ci_col + h:ci_col + h + 1] + bi_ref[h]
        lf_c = _log_sigmoid(sm_ref[:, cf_col + h:cf_col + h + 1] + bf_ref[h])
        ig_r = smt_ref[0, h:h + 1, :] + bi_ref[h]
        lf_r = _log_sigmoid(smt_ref[0, MLSTM_HEADS + h:MLSTM_HEADS + h + 1, :] + bf_ref[h])
        F_c = jnp.sum(jnp.where(si <= ti, lf_r, 0.0), axis=1, keepdims=True)
        F_r = jnp.sum(jnp.where(ti <= si, lf_c, 0.0), axis=0, keepdims=True)
        D = jnp.where(tril, F_c - F_r + ig_r, NEG_INF)
        m_prev = m_s[h:h + 1, 0:1]
        m_inter = F_c + m_prev
        m_t = jnp.maximum(m_inter, jnp.max(D, axis=1, keepdims=True))
        qh = q_ref[:, h * MLSTM_DQK:(h + 1) * MLSTM_DQK]
        ks = k_ref[:, h * MLSTM_DQK:(h + 1) * MLSTM_DQK] * kscale
        vh = v_ref[:, h * MLSTM_DV:(h + 1) * MLSTM_DV]
        qb, kb, vb = qh.astype(BF16), ks.astype(BF16), vh.astype(BF16)
        S = _dot_nt(qb, kb) * jnp.exp(D - m_t)
        dec = jnp.exp(m_inter - m_t)
        C = c_s[h]
        n_row = n_s[h:h + 1, :]
        num = _dot(S.astype(BF16), vb) + dec * _dot_nt(qb, C.astype(BF16))
        den = jnp.sum(S, axis=1, keepdims=True) + dec * jnp.sum(qh * n_row, axis=1, keepdims=True)
        hh = num / jnp.maximum(jnp.abs(den), 1.0)
        m_new = m_t[last:last + 1, :]
        F_last = F_c[last:last + 1, :]
        wl_c = jnp.where(rvalid, jnp.exp(F_last - F_c + ig_c - m_new), 0.0)
        dl = jnp.exp(F_last + m_prev - m_new)
        c_s[h] = dl * C + _dot_tn((vh * wl_c).astype(BF16), kb)
        n_s[h:h + 1, :] = dl * n_row + jnp.sum(wl_c * ks, axis=0, keepdims=True)
        m_s[h:h + 1, :] = jnp.broadcast_to(m_new, (1, 128))
        mu = jnp.mean(hh, -1, keepdims=True)
        d = hh - mu
        var = jnp.mean(d * d, -1, keepdims=True)
        hn = d * lax.rsqrt(var + 1e-5) * ng_ref[h:h + 1, :]
        vs = slice(h * MLSTM_DV, (h + 1) * MLSTM_DV)
        h_ref[:, vs] = (_sigmoid(og_ref[:, vs]) * hn).astype(h_ref.dtype)

    @pl.when(c == pl.num_programs(1) - 1)
    def _():
        c_ref[0] = c_s[...]
        n_ref[0] = n_s[...]
        m_ref[0] = m_s[...]


def _mlstm(zcq, zck, zcv, zco, zsm, b_i, b_f, norm_g, C0, n0, m0, B, T, lc, valid, ci_col, cf_col):
    nc = T // lc
    H, DQ, DV = MLSTM_HEADS, MLSTM_DQK, MLSTM_DV
    smt = jnp.concatenate([zsm[:, ci_col:ci_col + H], zsm[:, cf_col:cf_col + H]], -1)
    smt = smt.reshape(B * nc, lc, 2 * H).transpose(0, 2, 1)
    m0b = jnp.broadcast_to(m0[..., None], (B, H, 128))
    row = lambda w: pl.BlockSpec((lc, w), lambda b, c: (b * nc + c, 0))
    smem = pl.BlockSpec(memory_space=pltpu.SMEM)
    kern = functools.partial(_mlstm_kernel, lc=lc, valid=valid, ci_col=ci_col, cf_col=cf_col)
    c_spec = pl.BlockSpec((1, H, DV, DQ), lambda b, c: (b, 0, 0, 0))
    n_spec = pl.BlockSpec((1, H, DQ), lambda b, c: (b, 0, 0))
    h, C, n, m = pl.pallas_call(
        kern,
        out_shape=(jax.ShapeDtypeStruct((B * T, H * DV), BF16), jax.ShapeDtypeStruct((B, H, DV, DQ), F32),
                   jax.ShapeDtypeStruct((B, H, DQ), F32), jax.ShapeDtypeStruct((B, H, 128), F32)),
        grid=(B, nc),
        in_specs=[smem, smem, row(H * DQ), row(H * DQ), row(H * DV), row(H * DV), row(128),
                  pl.BlockSpec((1, 2 * H, lc), lambda b, c: (b * nc + c, 0, 0)),
                  pl.BlockSpec((H, DV), lambda b, c: (0, 0)), c_spec, n_spec, n_spec],
        out_specs=(row(H * DV), c_spec, n_spec, n_spec),
        scratch_shapes=[pltpu.VMEM((H, DV, DQ), F32), pltpu.VMEM((H, DQ), F32), pltpu.VMEM((H, 128), F32)],
        compiler_params=_cp("parallel", "arbitrary"),
        name="mlstm",
    )(b_i, b_f, zcq, zck, zcv, zco, zsm, smt, norm_g, C0, n0, m0b)
    return h, C, n, m[..., 0]


def _merge_kernel(a_ref, b_ref, c_ref, wa_ref, wb_ref, wc_ref, ga_ref, gb_ref, gc_ref, o_ref):
    o = (_sigmoid(ga_ref[...]) * _dot(a_ref[...], wa_ref[...])
         + _sigmoid(gb_ref[...]) * _dot(b_ref[...], wb_ref[...])
         + _sigmoid(gc_ref[...]) * _dot(c_ref[...], wc_ref[...]))
    o_ref[...] = o.astype(o_ref.dtype)


def _merge(oa, ob, oc, wa, wb, wc, layer, zmg, tm=512, tn=512):
    M, K = oa.shape
    D = wa.shape[-1]
    tm, tn = min(tm, M), min(tn, D)
    nj = D // tn
    x_spec = pl.BlockSpec((tm, K), lambda i, j: (i, 0))
    w_spec = _w_spec(wa, layer, (K, tn), lambda i, j: (0, j))
    g_spec = lambda o: pl.BlockSpec((tm, tn), lambda i, j: (i, o * nj + j))
    return pl.pallas_call(
        _merge_kernel,
        out_shape=jax.ShapeDtypeStruct((M, D), BF16),
        grid=(M // tm, nj),
        in_specs=[x_spec, x_spec, x_spec, w_spec, w_spec, w_spec, g_spec(0), g_spec(1), g_spec(2)],
        out_specs=pl.BlockSpec((tm, tn), lambda i, j: (i, j)),
        compiler_params=_cp("parallel", "parallel"),
        name="merge",
    )(oa, ob, oc, wa, wb, wc, zmg, zmg, zmg)


def _top16(s, ids, big, track_pos):
    vals, idxs = [], []
    pos = jnp.full(s.shape, -1, I32) if track_pos else None
    for it in range(PEER_TOPK):
        mx = jnp.max(s, axis=0, keepdims=True)
        idx = jnp.min(jnp.where(s == mx, ids, big), axis=0, keepdims=True)
        one = ids == idx
        s = jnp.where(one, NEG_INF, s)
        if track_pos:
            pos = jnp.where(one, it, pos)
        vals.append(mx)
        idxs.append(idx)
    return vals, idxs, pos


def _peer_route_kernel(q_ref, k1_ref, k2_ref, s1_ref, s2_ref, pk_ref):
    tm = q_ref.shape[0]
    half = PEER_DQ // 2
    K = PEER_TOPK
    k1 = k1_ref[...].astype(BF16)
    k2 = k2_ref[...].astype(BF16)
    kid = lax.broadcasted_iota(I32, (N_KEYS, 1), 0)
    i16 = lax.broadcasted_iota(I32, (K, 1), 0)
    i8 = lax.broadcasted_iota(I32, (8, 1), 0)
    cid = jnp.concatenate([i16, K + i8, 2 * K + i8, 3 * K + i8, i8 * K, i8 * K + 1, i8 * K + 2, (8 + i8) * K], axis=0)
    for h in range(PEER_HEADS):
        q1 = q_ref[:, h * PEER_DQ:h * PEER_DQ + half].astype(BF16)
        q2 = q_ref[:, h * PEER_DQ + half:(h + 1) * PEER_DQ].astype(BF16)
        s1 = _dot_nt(k1, q1)
        s2 = _dot_nt(k2, q2)
        v1, _, pos1 = _top16(s1, kid, N_KEYS, True)
        v2, _, pos2 = _top16(s2, kid, N_KEYS, True)
        v1m = jnp.concatenate(v1, axis=0)
        v2m = jnp.concatenate(v2, axis=0)
        pieces = [v1[0] + v2m, v1[1] + v2m[0:8], v1[2] + v2m[0:8], v1[3] + v2m[0:8]]
        for kk in range(3):
            pieces.append(jnp.where(i8 >= 4, v1m[0:8] + v2[kk], NEG_INF))
        pieces.append(v1m[8:16] + v2[0])
        cv, cj, _ = _top16(jnp.concatenate(pieces, axis=0), cid, K * K, False)
        z = jnp.zeros((1, tm), F32)
        wbits = jnp.zeros((K, tm), I32)
        for it in range(K):
            z = z + jnp.exp(cv[it] - cv[0])
            ii = jnp.right_shift(cj[it], 4)
            kk = cj[it] & (K - 1)
            wbits = jnp.where(i16 == ii, wbits | jnp.left_shift(1, kk), wbits)
        in1, in2 = pos1 >= 0, pos2 >= 0
        s1_ref[h] = jnp.where(in1, (s1 - cv[0]) * LOG2E - jnp.log2(z), NEG_INF)
        s2_ref[h] = jnp.where(in2, s2 * LOG2E, NEG_INF)
        rw = jnp.zeros((N_KEYS, tm), I32)
        for i in range(K):
            rw = jnp.where(pos1 == i, wbits[i:i + 1, :], rw)
        b2 = jnp.where(in2, jnp.left_shift(1, jnp.maximum(pos2, 0)), 0)
        pk_ref[h] = b2 | jnp.left_shift(rw, 16)


def _peer_route(q, k1, k2, tm=256):
    M = q.shape[0]
    tm = min(tm, M)
    spec = pl.BlockSpec((PEER_HEADS, N_KEYS, tm), lambda i: (0, 0, i))
    shp = lambda dt: jax.ShapeDtypeStruct((PEER_HEADS, N_KEYS, M), dt)
    return pl.pallas_call(
        _peer_route_kernel,
        out_shape=(shp(F32), shp(F32), shp(I32)),
        grid=(M // tm,),
        in_specs=[pl.BlockSpec((tm, PEER_HEADS * PEER_DQ), lambda i: (i, 0)),
                  pl.BlockSpec((N_KEYS, PEER_DQ // 2), lambda i: (0, 0)),
                  pl.BlockSpec((N_KEYS, PEER_DQ // 2), lambda i: (0, 0))],
        out_specs=(spec, spec, spec),
        compiler_params=_cp("parallel"),
        name="peer_route",
    )(q, k1, k2)


PEER_SUB = 256


def _peer_dense_kernel(x_ref, u_ref, v_ref, s1_ref, s2_ref, pk_ref, o_ref, *, te):
    e = pl.program_id(1)

    @pl.when(e == 0)
    def _():
        o_ref[...] = jnp.zeros_like(o_ref)

    x = x_ref[...]
    hs = [_dot_nt(x, u_ref[c * PEER_SUB:(c + 1) * PEER_SUB, :]) for c in range(te // PEER_SUB)]
    per = te // N_KEYS
    gates = []
    for al in range(per):
        a = e * per + al
        gate = None
        for hd in range(PEER_HEADS):
            wt = jnp.exp2(s1_ref[hd, pl.ds(a, 1), :] + s2_ref[hd])
            arow = pk_ref[hd, pl.ds(a, 1), :]
            abits = lax.shift_right_logical(arow, jnp.full_like(arow, 16))
            g = jnp.where((abits & pk_ref[hd]) != 0, wt, 0.0)
            gate = g if gate is None else gate + g
        gates.append(gate.T)
    gps = PEER_SUB // N_KEYS
    acts = [(_gelu(hs[c]) * jnp.concatenate(gates[c * gps:(c + 1) * gps], axis=1)).astype(BF16)
            for c in range(te // PEER_SUB)]
    o_ref[...] += _dot(jnp.concatenate(acts, axis=1), v_ref[...])


def _peer_dense(xb, u, v, layer, s1l, s2l, pk, tm=512, te=512):
    M, D = xb.shape
    NE = u.shape[-2]
    tm = min(tm, M)
    n_e = NE // te
    once = pl.Buffered(buffer_count=1)
    rspec = pl.BlockSpec((PEER_HEADS, N_KEYS, tm), lambda i, e: (0, 0, i), pipeline_mode=once)
    return pl.pallas_call(
        functools.partial(_peer_dense_kernel, te=te),
        out_shape=jax.ShapeDtypeStruct((M, D), F32),
        grid=(M // tm, n_e),
        in_specs=[pl.BlockSpec((tm, D), lambda i, e: (i, 0), pipeline_mode=once),
                  _w_spec(u, layer, (te, D), lambda i, e: (e, 0)),
                  _w_spec(v, layer, (te, D), lambda i, e: (e, 0)),
                  rspec, rspec, rspec],
        out_specs=pl.BlockSpec((tm, D), lambda i, e: (i, 0)),
        compiler_params=_cp("parallel", "arbitrary"),
        name="peer_dense",
    )(xb, u, v, s1l, s2l, pk)


def _peer(x1, x1b, lw, alpha):
    M, D = x1.shape
    mp = -(-M // 128) * 128
    xb = x1b if mp == M else jnp.pad(x1b, ((0, mp - M), (0, 0)))
    l = lw["layer"]
    q = _mm(xb, lw["peer_wq"], l)
    s1l, s2l, pk = _peer_route(q, lw["peer_k1"], lw["peer_k2"])
    y = _peer_dense(xb, lw["peer_u"], lw["peer_v"], l, s1l, s2l, pk)[:M]
    return _res_ln(x1, y, lw["ln2_g"], lw["ln2_b"], alpha)


Z_SEGS = ("a_q", "a_kv", "a_g", "b_uv", "c_q", "c_k", "c_v", "c_i", "c_f", "c_o", "mg")
STACKED_BF16 = ("w_br_a", "w_br_b", "w_br_c", "w_out", "peer_wq", "peer_u", "peer_v")
GATE_COL = 0
CI_COL = 3 * NSA_HEADS
CF_COL = CI_COL + MLSTM_HEADS


def _prep_layer_weights(l, w):
    D = w["w_in"].shape[1]
    sizes = (NSA_HEADS * HEAD_DIM, 6 * NSA_KV * HEAD_DIM, 3 * NSA_HEADS, 2 * W_B,
             MLSTM_HEADS * MLSTM_DQK, MLSTM_HEADS * MLSTM_DQK, MLSTM_HEADS * MLSTM_DV, MLSTM_HEADS, MLSTM_HEADS,
             MLSTM_HEADS * MLSTM_DV, 3 * D)
    offs = np.concatenate([[0], np.cumsum(sizes)])
    w_in = w["w_in"][l]
    seg = {n: (int(offs[i]), int(offs[i + 1])) for i, n in enumerate(Z_SEGS)}
    cut = lambda n: w_in[:, seg[n][0]:seg[n][1]].astype(BF16)
    lw = {n: cut(n) for n in ("a_q", "a_kv", "b_uv", "c_q", "c_k", "c_v", "c_o", "mg")}
    small = jnp.concatenate([w_in[:, seg[n][0]:seg[n][1]] for n in ("a_g", "c_i", "c_f")], -1)
    lw["small"] = jnp.pad(small, ((0, 0), (0, 128 - small.shape[1]))).astype(BF16)
    lw["layer"] = l
    for n in STACKED_BF16:
        lw[n] = w[n + "_bf16"]
    for n in ("cmp_pe", "cmp_w1", "cmp_b1", "cmp_w2", "gmlp_ln_g", "gmlp_ln_b", "gmlp_ws", "gmlp_bs",
              "mlstm_b_i", "mlstm_b_f", "mlstm_norm_g", "ln1_g", "ln1_b", "peer_k1", "peer_k2", "ln2_g", "ln2_b"):
        lw[n] = w[n][l]
    return lw


def _gates_by_group(zsm):
    M = zsm.shape[0]
    gt = zsm[:, GATE_COL:GATE_COL + 3 * NSA_HEADS].reshape(M, NSA_KV, 3 * NSA_REP).transpose(1, 0, 2)
    return jnp.pad(gt, ((0, 0), (0, 0), (0, 128 - 3 * NSA_REP)))


def _layer(x, xb, B, T, tv, lw, alpha, dec):
    M, D = x.shape
    z = {n: _mm(xb, lw[n]) for n in ("a_q", "a_kv", "b_uv", "c_q", "c_k", "c_v", "c_o", "mg", "small")}
    zsm = z["small"]
    q_start = 0 if dec is None else dec["q_start"]
    cos, sin = _rope_tables(q_start + jnp.arange(T))
    new_kv, win = _kvprep(z["a_kv"], cos, sin, T)
    gates4 = _gates_by_group(zsm)
    cmp_w = (lw["cmp_pe"], lw["cmp_w1"], lw["cmp_b1"], lw["cmp_w2"])
    gw = NSA_KV * HEAD_DIM
    if dec is None:
        n_pages = T // PAGE_SIZE
        ident = jnp.arange(B * n_pages, dtype=I32).reshape(B, n_pages)
        src3 = z["a_kv"][:, :2 * gw].reshape(M, 2 * NSA_KV, HEAD_DIM)
        kvc = _compress(src3, ident, 0, *cmp_w)
        o_a = _nsa_prefill(z["a_q"], new_kv, win, kvc, gates4, cos, sin, B, T)
        lc_g, lc_m = CHUNK, math.gcd(T, MLSTM_KERNEL_CHUNK)
        C0 = jnp.zeros((B, MLSTM_HEADS, MLSTM_DV, MLSTM_DQK), F32)
        n0 = jnp.zeros((B, MLSTM_HEADS, MLSTM_DQK), F32)
        m0 = jnp.zeros((B, MLSTM_HEADS), F32)
        valid = lc_m
    else:
        page_table = dec["page_table"]
        kvc = _compress(dec["cache3"], page_table, dec["page_off"], *cmp_w)
        L = q_start + tv
        n_cmp = (L - CMP_BLOCK) // CMP_STRIDE + 1
        n_slc = -(-L // SLC_BLOCK)
        assert n_cmp <= q_start // CMP_STRIDE - 1 and n_slc == q_start // SLC_BLOCK + 1
        assert q_start % SLC_BLOCK == 0 and tv <= T
        zq3 = z["a_q"].reshape(B, T, -1)
        ocmp, sel = _nsa_dec_sel(zq3, kvc, n_cmp, n_slc, q_start)
        sel = sel[:, :, :tv, :min(TOP_BLOCKS, n_slc)]
        o_a = _nsa_dec_attn(page_table, sel, zq3, cos, sin, dec["cache4"], dec["page_off"], new_kv.reshape(B, T, -1),
                            dec["wst3"], dec["win_off"], win.reshape(B, T, -1), ocmp, gates4, tv, q_start)
        o_a = o_a.reshape(M, -1)
        lc_g = lc_m = T
        C0, n0, m0 = dec["C"], dec["n"], dec["m"]
        valid = tv
    o_b, v_rows = _gmlp(z["b_uv"], lw["gmlp_ln_g"], lw["gmlp_ln_b"], lw["gmlp_ws"], lw["gmlp_bs"], lc_g)
    o_c, C, n, m = _mlstm(z["c_q"], z["c_k"], z["c_v"], z["c_o"], zsm, lw["mlstm_b_i"], lw["mlstm_b_f"],
                          lw["mlstm_norm_g"], C0, n0, m0, B, T, lc_m, valid, CI_COL, CF_COL)
    merged = _merge(o_a, o_b, o_c, lw["w_br_a"], lw["w_br_b"], lw["w_br_c"], lw["layer"], z["mg"])
    x1, x1b = _res_ln(x, _mm(merged, lw["w_out"], lw["layer"]), lw["ln1_g"], lw["ln1_b"], alpha)
    y, yb = _peer(x1, x1b, lw, alpha)
    return y, yb, new_kv, win, v_rows, C, n, m


def kernel(x_prompt, x_sample, cache_nsa_kv, state_nsa_win, state_mlstm_C, state_mlstm_n, state_mlstm_m,
           page_table, w_in, cmp_pe, cmp_w1, cmp_b1, cmp_w2, gmlp_ln_g, gmlp_ln_b, gmlp_ws, gmlp_bs,
           mlstm_b_i, mlstm_b_f, mlstm_norm_g, w_br_a, w_br_b, w_br_c, w_out, ln1_g, ln1_b,
           peer_wq, peer_k1, peer_k2, peer_u, peer_v, ln2_g, ln2_b):
    w = dict(w_in=w_in, cmp_pe=cmp_pe, cmp_w1=cmp_w1, cmp_b1=cmp_b1, cmp_w2=cmp_w2, gmlp_ln_g=gmlp_ln_g,
             gmlp_ln_b=gmlp_ln_b, gmlp_ws=gmlp_ws, gmlp_bs=gmlp_bs, mlstm_b_i=mlstm_b_i, mlstm_b_f=mlstm_b_f,
             mlstm_norm_g=mlstm_norm_g, w_br_a=w_br_a, w_br_b=w_br_b, w_br_c=w_br_c, w_out=w_out, ln1_g=ln1_g,
             ln1_b=ln1_b, peer_wq=peer_wq, peer_k1=peer_k1, peer_k2=peer_k2, peer_u=peer_u, peer_v=peer_v,
             ln2_g=ln2_g, ln2_b=ln2_b)
    for n in STACKED_BF16:
        w[n + "_bf16"] = w[n].astype(BF16)
    depth = w_in.shape[0]
    alpha = (2 * depth) ** 0.25
    Bp, Tp, D = x_prompt.shape
    Bs, Ts, _ = x_sample.shape
    Tpad = 8
    past_len = page_table.shape[1] * PAGE_SIZE
    n_pool = cache_nsa_kv.shape[1]
    P = state_nsa_win.shape[2]
    cache3 = cache_nsa_kv.reshape(depth * n_pool * PAGE_SIZE, 4 * NSA_KV, HEAD_DIM)
    cache4 = cache_nsa_kv.reshape(depth * n_pool * PAGE_SIZE, 2, 2 * NSA_KV, HEAD_DIM)
    wst3 = state_nsa_win.reshape(depth * Bs * P, 2 * NSA_KV, HEAD_DIM)
    yp = x_prompt.reshape(Bp * Tp, D)
    ys = jnp.pad(x_sample, ((0, 0), (0, Tpad - Ts), (0, 0))).reshape(Bs * Tpad, D)
    ypb, ysb = yp.astype(BF16), ys.astype(BF16)
    outs = [[] for _ in range(11)]
    wkeep_p = min(WINDOW, Tp)
    for l in range(depth):
        lw = _prep_layer_weights(l, w)
        yp, ypb, kv, win, _, C, n, m = _layer(yp, ypb, Bp, Tp, Tp, lw, alpha, None)
        outs[0].append(kv.reshape(Bp, Tp, 4, NSA_KV, HEAD_DIM))
        outs[2].append(win.reshape(Bp, Tp, 2, NSA_KV, HEAD_DIM)[:, Tp - wkeep_p:])
        outs[5].append(C); outs[6].append(n); outs[7].append(m)
        dec = dict(q_start=past_len, cache3=cache3, cache4=cache4, page_off=l * n_pool, page_table=page_table,
                   wst3=wst3, win_off=(l * Bs, depth * Bs), C=state_mlstm_C[l], n=state_mlstm_n[l], m=state_mlstm_m[l])
        ys, ysb, kv, win, v_rows, C, n, m = _layer(ys, ysb, Bs, Tpad, Ts, lw, alpha, dec)
        outs[1].append(kv.reshape(Bs, Tpad, 4, NSA_KV, HEAD_DIM)[:, :Ts])
        win_new = win.reshape(Bs, Tpad, 2, NSA_KV, HEAD_DIM)[:, :Ts]
        outs[3].append(jnp.concatenate([state_nsa_win[l], win_new], 1)[:, Ts:])
        outs[4].append(v_rows.reshape(Bs, Tpad, W_B)[:, :Ts])
        outs[8].append(C); outs[9].append(n); outs[10].append(m)
    y_prompt = yp.reshape(Bp, Tp, D)
    y_sample = ys.reshape(Bs, Tpad, D)[:, :Ts]
    st = [jnp.stack(o) for o in outs]
    return (y_prompt, y_sample, st[0], st[1], st[2], st[3], st[4], st[5], st[6], st[7], st[8], st[9], st[10])
```

```python
import functools
import math

import numpy as np
import jax
import jax.numpy as jnp
from jax import lax
from jax.experimental import pallas as pl
from jax.experimental.pallas import tpu as pltpu

F32 = jnp.float32
BF16 = jnp.bfloat16
I32 = jnp.int32

HEAD_DIM = 128
ROT_DIM = HEAD_DIM // 4
ROPE_THETA = 500000.0
NSA_HEADS = 16
NSA_KV = 4
NSA_REP = NSA_HEADS // NSA_KV
CMP_BLOCK = 32
CMP_STRIDE = 16
SLC_BLOCK = 64
SLC_SHIFT = 6
TOP_BLOCKS = 16
WINDOW = 512
FORCE_SCORE = 1.0e9
CHUNK = 128
GMLP_GROUPS = 16
GMLP_GDIM = 128
W_B = GMLP_GROUPS * GMLP_GDIM
MLSTM_HEADS = 8
MLSTM_DQK = 128
MLSTM_DV = 256
MLSTM_KERNEL_CHUNK = 128
PEER_HEADS = 8
PEER_DQ = 256
N_KEYS = 128
PEER_TOPK = 16
PAGE_SIZE = 128
LOG2E = 1.4426950408889634
NEG_INF = float("-inf")

VMEM_LIMIT_BYTES = 56 * 1024 * 1024


def _cp(*sem):
    return pltpu.CompilerParams(dimension_semantics=sem, vmem_limit_bytes=VMEM_LIMIT_BYTES)


def _gelu(x):
    return 0.5 * x * (1.0 + jnp.tanh(0.7978845608028654 * (x + 0.044715 * (x * x * x))))


def _sigmoid(x):
    return 1.0 / (1.0 + jnp.exp(-x))


def _log_sigmoid(x):
    return jnp.minimum(x, 0.0) - jnp.log(1.0 + jnp.exp(-jnp.abs(x)))


def _dot(a, b):
    return jnp.dot(a, b, preferred_element_type=F32)


def _dot_nt(a, b):
    return lax.dot_general(a, b, (((1,), (1,)), ((), ())), preferred_element_type=F32)


def _dot_tn(a, b):
    return lax.dot_general(a, b, (((0,), (0,)), ((), ())), preferred_element_type=F32)


def _masked_softmax(lg, mask):
    lg = jnp.where(mask, lg, NEG_INF)
    mx = jnp.max(lg, -1, keepdims=True)
    mx = jnp.where(mx > NEG_INF, mx, 0.0)
    e = jnp.exp(lg - mx)
    return e / jnp.maximum(jnp.sum(e, -1, keepdims=True), 1e-30)


def _masked_attend(lg, mask, v):
    lg = jnp.where(mask, lg, NEG_INF)
    mx = jnp.max(lg, -1, keepdims=True)
    mx = jnp.where(mx > NEG_INF, mx, 0.0)
    e = jnp.exp(lg - mx)
    den = jnp.maximum(jnp.sum(e, -1, keepdims=True), 1e-30)
    return _dot(e.astype(BF16), v) / den


def _split3_dot(p, m_bf16):
    hi = p.astype(BF16)
    r1 = p - hi.astype(F32)
    mid = r1.astype(BF16)
    lo = (r1 - mid.astype(F32)).astype(BF16)
    return _dot(hi, m_bf16) + _dot(mid, m_bf16) + _dot(lo, m_bf16)


def _rope(x, cos, sin):
    lane = lax.broadcasted_iota(I32, x.shape, 1)
    up = pltpu.roll(x, HEAD_DIM - ROT_DIM // 2, 1)
    dn = pltpu.roll(x, ROT_DIM // 2, 1)
    return x * cos + jnp.where(lane < ROT_DIM // 2, up, dn) * sin


def _rope_tables(pos):
    inv = ROPE_THETA ** (-jnp.arange(0, ROT_DIM, 2, dtype=F32) / ROT_DIM)
    ang = pos.astype(F32)[:, None] * inv[None, :]
    c, s = jnp.cos(ang), jnp.sin(ang)
    n = pos.shape[0]
    cos = jnp.concatenate([c, c, jnp.ones((n, HEAD_DIM - ROT_DIM), F32)], -1)
    sin = jnp.concatenate([-s, s, jnp.zeros((n, HEAD_DIM - ROT_DIM), F32)], -1)
    return cos, sin


def _mm_kernel(x_ref, w_ref, o_ref, acc_ref):
    k = pl.program_id(2)

    @pl.when(k == 0)
    def _():
        acc_ref[...] = jnp.zeros_like(acc_ref)

    acc_ref[...] += _dot(x_ref[...], w_ref[...])

    @pl.when(k == pl.num_programs(2) - 1)
    def _():
        o_ref[...] = acc_ref[...].astype(o_ref.dtype)


def _w_spec(w, layer, block, imap):
    if w.ndim == 2:
        return pl.BlockSpec(block, imap)
    return pl.BlockSpec((None,) + block, lambda *a: (layer,) + imap(*a))


def _mm(x, w, layer=None, out_dtype=F32, tm=1024, tn=512, tk=4096):
    M, K = x.shape
    N = w.shape[-1]
    tm, tn, tk = min(tm, M), min(tn, N), min(tk, K)
    assert M % tm == 0 and N % tn == 0 and K % tk == 0, (x.shape, w.shape)
    return pl.pallas_call(
        _mm_kernel,
        out_shape=jax.ShapeDtypeStruct((M, N), out_dtype),
        grid=(M // tm, N // tn, K // tk),
        in_specs=[pl.BlockSpec((tm, tk), lambda i, j, k: (i, k)),
                  _w_spec(w, layer, (tk, tn), lambda i, j, k: (k, j))],
        out_specs=pl.BlockSpec((tm, tn), lambda i, j, k: (i, j)),
        scratch_shapes=[pltpu.VMEM((tm, tn), F32)],
        compiler_params=_cp("parallel", "parallel", "arbitrary"),
        name="mm",
    )(x, w)


def _res_ln_kernel(x_ref, y_ref, g_ref, b_ref, o_ref, ob_ref, *, alpha):
    v = alpha * x_ref[...] + y_ref[...]
    mu = jnp.mean(v, -1, keepdims=True)
    d = v - mu
    var = jnp.mean(d * d, -1, keepdims=True)
    o = d * lax.rsqrt(var + 1e-5) * g_ref[...] + b_ref[...]
    o_ref[...] = o
    ob_ref[...] = o.astype(BF16)


def _res_ln(x, y, g, b, alpha, tm=256):
    M, D = x.shape
    tm = min(tm, M)
    row = pl.BlockSpec((tm, D), lambda i: (i, 0))
    vec = pl.BlockSpec((1, D), lambda i: (0, 0))
    return pl.pallas_call(
        functools.partial(_res_ln_kernel, alpha=alpha),
        out_shape=(jax.ShapeDtypeStruct((M, D), F32), jax.ShapeDtypeStruct((M, D), BF16)),
        grid=(M // tm,),
        in_specs=[row, row, vec, vec],
        out_specs=(row, row),
        compiler_params=_cp("parallel"),
        name="res_ln",
    )(x, y, g.reshape(1, D), b.reshape(1, D))


def _kvprep_kernel(z_ref, cos_ref, sin_ref, kv_ref, win_ref):
    cos, sin = cos_ref[...], sin_ref[...]
    gw = NSA_KV * HEAD_DIM
    kv_ref[:, 0:2 * gw] = z_ref[:, 0:2 * gw]
    kv_ref[:, 3 * gw:4 * gw] = z_ref[:, 3 * gw:4 * gw]
    win_ref[:, gw:2 * gw] = z_ref[:, 5 * gw:6 * gw]
    for g in range(NSA_KV):
        a = 2 * gw + g * HEAD_DIM
        kv_ref[:, a:a + HEAD_DIM] = _rope(z_ref[:, a:a + HEAD_DIM], cos, sin)
        a = 4 * gw + g * HEAD_DIM
        win_ref[:, g * HEAD_DIM:(g + 1) * HEAD_DIM] = _rope(z_ref[:, a:a + HEAD_DIM], cos, sin)


def _kvprep(zkv, cos, sin, T, tm=256):
    M = zkv.shape[0]
    tm = min(tm, T)
    nt = T // tm
    gw = NSA_KV * HEAD_DIM
    return pl.pallas_call(
        _kvprep_kernel,
        out_shape=(jax.ShapeDtypeStruct((M, 4 * gw), F32), jax.ShapeDtypeStruct((M, 2 * gw), F32)),
        grid=(M // tm,),
        in_specs=[pl.BlockSpec((tm, 6 * gw), lambda i: (i, 0)),
                  pl.BlockSpec((tm, HEAD_DIM), lambda i: (i % nt, 0)),
                  pl.BlockSpec((tm, HEAD_DIM), lambda i: (i % nt, 0))],
        out_specs=(pl.BlockSpec((tm, 4 * gw), lambda i: (i, 0)),
                   pl.BlockSpec((tm, 2 * gw), lambda i: (i, 0))),
        compiler_params=_cp("parallel"),
        name="kvprep",
    )(zkv, cos, sin)


CMP_PAGES = 8
PREFILL_SPAN = 512


def _cmp_bias_kernel(pe_ref, w1f_ref, b1_ref, o_ref):
    for kv in range(2):
        b = b1_ref[kv:kv + 1, :]
        for half in range(2):
            b = b + _dot(pe_ref[kv, half].astype(BF16), w1f_ref[kv, half])[0:1, :]
        o_ref[:, kv * HEAD_DIM:(kv + 1) * HEAD_DIM] = jnp.broadcast_to(b, (8, HEAD_DIM))


def _cmp_kernel(pt_ref, *refs, P):
    page_refs = refs[:P + 1]
    wa_ref, wb_ref, bias_ref, w2_ref, o_ref, res_s = refs[P + 1:]
    hp = PAGE_SIZE // CMP_STRIDE
    sg = 2 * NSA_KV
    rows = P * hp * sg
    acc_a = jnp.zeros((rows, 2 * HEAD_DIM), F32)
    acc_b = jnp.zeros((rows, 2 * HEAD_DIM), F32)
    for j in range(CMP_STRIDE):
        la, lb = [], []
        for p in range(P):
            la.append(page_refs[p][pl.ds(j, hp, stride=CMP_STRIDE), 0:sg, :].reshape(hp * sg, HEAD_DIM))
            lb.append(page_refs[p][pl.ds(j + CMP_STRIDE, hp - 1, stride=CMP_STRIDE), 0:sg, :]
                      .reshape((hp - 1) * sg, HEAD_DIM))
            lb.append(page_refs[p + 1][j, 0:sg, :])
        acc_a = acc_a + _dot(jnp.concatenate(la, axis=0).astype(BF16), wa_ref[j])
        acc_b = acc_b + _dot(jnp.concatenate(lb, axis=0).astype(BF16), wb_ref[j])
    h = _gelu(acc_a + acc_b + bias_ref[0:1, :])
    ok = _dot(h[:, :HEAD_DIM].astype(BF16), w2_ref[0].astype(BF16))
    ov = _dot(h[:, HEAD_DIM:].astype(BF16), w2_ref[1].astype(BF16))
    is_k = (lax.broadcasted_iota(I32, (rows, 1), 0) & (sg - 1)) < NSA_KV
    res_s[...] = jnp.where(is_k, ok, ov)
    for s in range(sg):
        o_ref[0, s] = res_s[pl.ds(s, P * hp, stride=sg), :]


def _compress(src3, page_table, page_off, cmp_pe, cmp_w1, cmp_b1, cmp_w2):
    B, n_pages = page_table.shape
    P = math.gcd(CMP_PAGES, n_pages)
    hp = PAGE_SIZE // CMP_STRIDE
    sg = 2 * NSA_KV
    src4 = src3.reshape(src3.shape[0], src3.shape[1] // sg, sg, HEAD_DIM)
    pe_flat = jnp.broadcast_to(cmp_pe.reshape(2, 2, 1, CMP_STRIDE * HEAD_DIM), (2, 2, 8, CMP_STRIDE * HEAD_DIM))
    w1_flat = cmp_w1.reshape(2, 2, CMP_STRIDE * HEAD_DIM, HEAD_DIM).astype(BF16)
    bias = pl.pallas_call(
        _cmp_bias_kernel,
        out_shape=jax.ShapeDtypeStruct((8, 2 * HEAD_DIM), F32),
        name="cmp_bias",
    )(pe_flat, w1_flat, cmp_b1)
    wa = jnp.concatenate([cmp_w1[0, :CMP_STRIDE], cmp_w1[1, :CMP_STRIDE]], -1).astype(BF16)
    wb = jnp.concatenate([cmp_w1[0, CMP_STRIDE:], cmp_w1[1, CMP_STRIDE:]], -1).astype(BF16)

    def page_spec(p):
        return pl.BlockSpec((PAGE_SIZE, None, sg, HEAD_DIM),
                            lambda b, c, pt: (page_off + pt[b, jnp.minimum(c * P + p, n_pages - 1)], 0, 0, 0))

    full = lambda shape: pl.BlockSpec(shape, lambda b, c, pt: (0,) * len(shape))
    return pl.pallas_call(
        functools.partial(_cmp_kernel, P=P),
        out_shape=jax.ShapeDtypeStruct((B, sg, n_pages * hp, HEAD_DIM), F32),
        grid_spec=pltpu.PrefetchScalarGridSpec(
            num_scalar_prefetch=1,
            grid=(B, n_pages // P),
            in_specs=[page_spec(p) for p in range(P + 1)]
            + [full(wa.shape), full(wb.shape), full(bias.shape), full(cmp_w2.shape)],
            out_specs=pl.BlockSpec((1, sg, P * hp, HEAD_DIM), lambda b, c, pt: (b, 0, c, 0)),
            scratch_shapes=[pltpu.VMEM((P * hp * sg, HEAD_DIM), F32)],
        ),
        compiler_params=_cp("parallel", "parallel"),
        name="cmp",
    )(page_table, *([src4] * (P + 1)), wa, wb, bias, cmp_w2)


def _cmp_to_slc_mask(ncp, nsp, n_cmp, n_slc):
    ci = lax.broadcasted_iota(I32, (ncp, nsp), 0)
    si = lax.broadcasted_iota(I32, (ncp, nsp), 1)
    ov = ((ci * CMP_STRIDE < (si + 1) * SLC_BLOCK) & (ci * CMP_STRIDE + CMP_BLOCK > si * SLC_BLOCK)
          & (ci < n_cmp) & (si < n_slc))
    return jnp.where(ov, 1.0, 0.0).astype(BF16)


def _nsa_prefill_kernel(q_ref, cos_ref, sin_ref, kc_ref, vc_ref, ks_ref, vs_ref, kw_ref, vw_ref, gt_ref, prev_ref,
                        o_ref, *, tq, T, n_cmp, n_slc, wlen, i0, kext):
    del prev_ref
    i = i0 + pl.program_id(2)
    t0 = i * tq
    scale = HEAD_DIM ** -0.5
    rep = NSA_REP
    q4 = q_ref[...]
    Q = jnp.concatenate([q4[:, r * HEAD_DIM:(r + 1) * HEAD_DIM] for r in range(rep)], axis=0)
    cos = jnp.concatenate([cos_ref[...]] * rep, axis=0)
    sin = jnp.concatenate([sin_ref[...]] * rep, axis=0)
    Qb = (Q * scale).astype(BF16)
    Qrb = (_rope(Q, cos, sin) * scale).astype(BF16)
    qpos1 = t0 + lax.broadcasted_iota(I32, (tq, 1), 0)
    qpos = jnp.concatenate([qpos1] * rep, axis=0)
    ncp = kc_ref.shape[2]
    kc = kc_ref[0, 0].astype(BF16)
    vc = vc_ref[0, 0].astype(BF16)
    lg = _dot_nt(Qb, kc)
    cidx = lax.broadcasted_iota(I32, (1, ncp), 1)
    cmask = (cidx * CMP_STRIDE + CMP_BLOCK - 1 <= qpos) & (cidx < n_cmp)
    p = _masked_softmax(lg, cmask)
    o_cmp = _dot(p.astype(BF16), vc)
    psum = p[0:tq]
    for r in range(1, rep):
        psum = psum + p[r * tq:(r + 1) * tq]
    nsp = 128
    imp = _split3_dot(psum, _cmp_to_slc_mask(ncp, nsp, n_cmp, n_slc))
    blk = lax.broadcasted_iota(I32, (1, nsp), 1)
    forced = (blk == 0) | (blk == jnp.right_shift(qpos1, SLC_SHIFT))
    future = blk * SLC_BLOCK > qpos1
    imp = jnp.where(forced, FORCE_SCORE, jnp.where(future, NEG_INF, imp))
    imp = jnp.where(blk < n_slc, imp, NEG_INF)
    rank = jnp.zeros((tq, nsp), F32)
    for s2 in range(n_slc):
        col = imp[:, s2:s2 + 1]
        beats = (col > imp) | ((col == imp) & (blk > s2))
        rank = rank + jnp.where(beats, 1.0, 0.0)
    sel = (rank < float(min(TOP_BLOCKS, n_slc))) & (blk < n_slc)
    srow = lax.broadcasted_iota(I32, (nsp, kext), 0)
    kcol = lax.broadcasted_iota(I32, (nsp, kext), 1)
    expand = jnp.where(jnp.right_shift(kcol, SLC_SHIFT) == srow, 1.0, 0.0).astype(BF16)
    selk1 = _dot(jnp.where(sel, 1.0, 0.0).astype(BF16), expand)
    selk = jnp.concatenate([selk1] * rep, axis=0)
    kpos = lax.broadcasted_iota(I32, (1, kext), 1)
    smask = (selk > 0.5) & (kpos <= qpos)
    o_s = _masked_attend(_dot_nt(Qrb, ks_ref[0:kext, :].astype(BF16)), smask, vs_ref[0:kext, :].astype(BF16))
    start = pl.multiple_of(jnp.clip(t0 - WINDOW, 0, T - wlen), 128)
    kw = kw_ref[pl.ds(start, wlen), :].astype(BF16)
    vw = vw_ref[pl.ds(start, wlen), :].astype(BF16)
    dpos = qpos - (start + lax.broadcasted_iota(I32, (1, wlen), 1))
    o_w = _masked_attend(_dot_nt(Qrb, kw), (dpos >= 0) & (dpos <= WINDOW), vw)
    gt = _sigmoid(gt_ref[0])
    for r in range(rep):
        rows = slice(r * tq, (r + 1) * tq)
        o = (gt[:, 3 * r:3 * r + 1] * o_cmp[rows] + gt[:, 3 * r + 1:3 * r + 2] * o_s[rows]
             + gt[:, 3 * r + 2:3 * r + 3] * o_w[rows])
        o_ref[:, r * HEAD_DIM:(r + 1) * HEAD_DIM] = o.astype(o_ref.dtype)


def _nsa_prefill(zq, new_kv, win, kvc, gates4, cos, sin, B, T, tq=256):
    tq = min(tq, T)
    nt = T // tq
    n_cmp = (T - CMP_BLOCK) // CMP_STRIDE + 1
    n_slc = -(-T // SLC_BLOCK)
    assert n_slc <= 128 and T % tq == 0
    wlen = min(T, WINDOW + tq)
    ncp = kvc.shape[2]
    gw = NSA_KV
    span = math.gcd(T, PREFILL_SPAN) // tq
    out = jnp.zeros((B * T, NSA_HEADS * HEAD_DIM), BF16)
    for i0 in range(0, nt, span):
        kern = functools.partial(_nsa_prefill_kernel, tq=tq, T=T, n_cmp=n_cmp, n_slc=n_slc, wlen=wlen,
                                 i0=i0, kext=(i0 + span) * tq)
        qrow = lambda b, g, i, i0=i0: (b * nt + i0 + i, g)
        out = pl.pallas_call(
            kern,
            out_shape=jax.ShapeDtypeStruct((B * T, NSA_HEADS * HEAD_DIM), BF16),
            grid=(B, NSA_KV, span),
            in_specs=[pl.BlockSpec((tq, NSA_REP * HEAD_DIM), qrow),
                      pl.BlockSpec((tq, HEAD_DIM), lambda b, g, i, i0=i0: (i0 + i, 0)),
                      pl.BlockSpec((tq, HEAD_DIM), lambda b, g, i, i0=i0: (i0 + i, 0)),
                      pl.BlockSpec((1, 1, ncp, HEAD_DIM), lambda b, g, i: (b, g, 0, 0)),
                      pl.BlockSpec((1, 1, ncp, HEAD_DIM), lambda b, g, i: (b, gw + g, 0, 0)),
                      pl.BlockSpec((T, HEAD_DIM), lambda b, g, i: (b, 2 * gw + g)),
                      pl.BlockSpec((T, HEAD_DIM), lambda b, g, i: (b, 3 * gw + g)),
                      pl.BlockSpec((T, HEAD_DIM), lambda b, g, i: (b, g)),
                      pl.BlockSpec((T, HEAD_DIM), lambda b, g, i: (b, gw + g)),
                      pl.BlockSpec((1, tq, 128), lambda b, g, i, i0=i0: (g, b * nt + i0 + i, 0)),
                      pl.BlockSpec(memory_space=pl.ANY)],
            out_specs=pl.BlockSpec((tq, NSA_REP * HEAD_DIM), qrow),
            input_output_aliases={10: 0},
            compiler_params=_cp("parallel", "parallel", "parallel"),
            name="nsa_prefill",
        )(zq, cos, sin, kvc, kvc, new_kv, new_kv, win, win, gates4, out)
    return out


def _nsa_dec_sel_kernel(q_ref, kc_ref, vc_ref, ocmp_ref, sel_ref, *, tp, n_cmp, n_slc, q_start):
    scale = HEAD_DIM ** -0.5
    rep = NSA_REP
    ncp = kc_ref.shape[2]
    nsp = -(-n_slc // 128) * 128
    qpos1 = q_start + lax.broadcasted_iota(I32, (tp, 1), 0)
    qpos = jnp.concatenate([qpos1] * rep, axis=0)
    cidx = lax.broadcasted_iota(I32, (1, ncp), 1)
    cmask = (cidx * CMP_STRIDE + CMP_BLOCK - 1 <= qpos) & (cidx < n_cmp)
    mm = _cmp_to_slc_mask(ncp, nsp, n_cmp, n_slc)
    blk = lax.broadcasted_iota(I32, (1, nsp), 1)
    lane16 = lax.broadcasted_iota(I32, (tp, 128), 1)
    for g in range(NSA_KV):
        c0 = g * rep * HEAD_DIM
        Q = jnp.concatenate([q_ref[0, :, c0 + r * HEAD_DIM:c0 + (r + 1) * HEAD_DIM] for r in range(rep)], axis=0)
        lg = _dot_nt(Q.astype(BF16), kc_ref[0, g].astype(BF16)) * scale
        p = _masked_softmax(lg, cmask)
        ocmp_ref[0, g] = _dot(p.astype(BF16), vc_ref[0, g].astype(BF16))
        psum = p[0:tp]
        for r in range(1, rep):
            psum = psum + p[r * tp:(r + 1) * tp]
        imp = _split3_dot(psum, mm)
        forced = (blk == 0) | (blk == jnp.right_shift(qpos1, SLC_SHIFT))
        future = blk * SLC_BLOCK > qpos1
        imp = jnp.where(forced, FORCE_SCORE, jnp.where(future, NEG_INF, imp))
        taken = jnp.broadcast_to(blk >= n_slc, (tp, nsp))
        sel = jnp.zeros((tp, 128), I32)
        for it in range(min(TOP_BLOCKS, n_slc)):
            cand = jnp.where(taken, NEG_INF, imp)
            mx = jnp.max(cand, -1, keepdims=True)
            hit = (cand == mx) & jnp.logical_not(taken)
            idx = jnp.min(jnp.where(hit, blk, nsp), -1, keepdims=True)
            taken = taken | (blk == idx)
            sel = jnp.where(lane16 == it, idx, sel)
        sel_ref[0, g] = sel


def _nsa_dec_sel(zq3, kvc, n_cmp, n_slc, q_start):
    B, tp, _ = zq3.shape
    ncp = kvc.shape[2]
    kern = functools.partial(_nsa_dec_sel_kernel, tp=tp, n_cmp=n_cmp, n_slc=n_slc, q_start=q_start)
    return pl.pallas_call(
        kern,
        out_shape=(jax.ShapeDtypeStruct((B, NSA_KV, NSA_REP * tp, HEAD_DIM), F32),
                   jax.ShapeDtypeStruct((B, NSA_KV, tp, 128), I32)),
        grid=(B,),
        in_specs=[pl.BlockSpec((1, tp, NSA_HEADS * HEAD_DIM), lambda b: (b, 0, 0)),
                  pl.BlockSpec((1, NSA_KV, ncp, HEAD_DIM), lambda b: (b, 0, 0, 0)),
                  pl.BlockSpec((1, NSA_KV, ncp, HEAD_DIM), lambda b: (b, 1, 0, 0))],
        out_specs=(pl.BlockSpec((1, NSA_KV, NSA_REP * tp, HEAD_DIM), lambda b: (b, 0, 0, 0)),
                   pl.BlockSpec((1, NSA_KV, tp, 128), lambda b: (b, 0, 0, 0))),
        compiler_params=_cp("parallel"),
        name="nsa_dec_sel",
    )(zq3, kvc, kvc)


def _nsa_dec_attn_kernel(pt_ref, sel_ref, q_ref, cos_ref, sin_ref, *refs, tp, tv, n_past_blk, q_start, ktop):
    nb = NSA_KV * tv
    blk_refs = refs[:nb]
    nkv_ref, wst_ref, win_ref, ocmp_ref, gt_ref, o_ref, qr_s, m_s, l_s, acc_s = refs[nb:]
    b, k = pl.program_id(0), pl.program_id(1)
    rep = NSA_REP
    scale = HEAD_DIM ** -0.5
    rows = rep * tp
    gw = NSA_KV * HEAD_DIM
    trow = lax.broadcasted_iota(I32, (rows, 1), 0) & (tp - 1)
    qpos = q_start + trow

    @pl.when(k == 0)
    def _():
        cos = jnp.concatenate([cos_ref[...]] * rep, axis=0)
        sin = jnp.concatenate([sin_ref[...]] * rep, axis=0)
        for g in range(NSA_KV):
            c0 = g * rep * HEAD_DIM
            Q = jnp.concatenate([q_ref[0, :, c0 + r * HEAD_DIM:c0 + (r + 1) * HEAD_DIM] for r in range(rep)], axis=0)
            qr_s[g] = _rope(Q, cos, sin)
        m_s[...] = jnp.full_like(m_s, NEG_INF)
        l_s[...] = jnp.zeros_like(l_s)
        acc_s[...] = jnp.zeros_like(acc_s)

    def online_update(g, lg, mask, v):
        lg = jnp.where(mask, lg, NEG_INF)
        m_old = m_s[g]
        m_new = jnp.maximum(m_old, jnp.max(lg, -1, keepdims=True))
        m_safe = jnp.where(m_new > NEG_INF, m_new, 0.0)
        a = jnp.exp(m_old - m_safe)
        e = jnp.exp(lg - m_safe)
        l_s[g] = a * l_s[g] + jnp.sum(e, -1, keepdims=True)
        acc_s[g] = a * acc_s[g] + _dot(e.astype(BF16), v)
        m_s[g] = m_new

    col = lax.broadcasted_iota(I32, (1, tv * SLC_BLOCK), 1)
    cblk = jnp.right_shift(col, SLC_SHIFT)
    far = 1 << 24
    for g in range(NSA_KV):
        Qrb = qr_s[g].astype(BF16)
        kb = jnp.concatenate([blk_refs[g * tv + t][:, g, :].astype(BF16) for t in range(tv)], axis=0)
        vb = jnp.concatenate([blk_refs[g * tv + t][:, NSA_KV + g, :].astype(BF16) for t in range(tv)], axis=0)
        lg = _dot_nt(Qrb, kb) * scale
        mask = None
        for t in range(tv):
            s = sel_ref[b, g, t, k]
            s = jnp.where(s < n_past_blk, s, far)
            kpos = s * SLC_BLOCK + (col - t * SLC_BLOCK)
            mt = (cblk == t) & (trow == t) & (kpos <= qpos)
            mask = mt if mask is None else (mask | mt)
        online_update(g, lg, mask, vb)

    @pl.when(k == ktop - 1)
    def _():
        jrow = lax.broadcasted_iota(I32, (1, tp), 1)
        P = wst_ref.shape[0]
        jw = lax.broadcasted_iota(I32, (1, P), 1)
        dpos_w = (qpos - (q_start - P)) - jw
        mask_w = (dpos_w >= 0) & (dpos_w <= WINDOW) & (q_start - P + jw >= 0)
        mask_n = (jrow <= trow) & (jrow < tv)
        for g in range(NSA_KV):
            Qrb = qr_s[g].astype(BF16)
            has_new = None
            for t in range(tv):
                f = sel_ref[b, g, t, 0] == n_past_blk
                for kk in range(1, ktop):
                    f = f | (sel_ref[b, g, t, kk] == n_past_blk)
                hn = trow == jnp.where(f, t, -1)
                has_new = hn if has_new is None else (has_new | hn)
            c_k = 2 * gw + g * HEAD_DIM
            c_v = 3 * gw + g * HEAD_DIM
            nk = nkv_ref[0, :, c_k:c_k + HEAD_DIM].astype(BF16)
            nv = nkv_ref[0, :, c_v:c_v + HEAD_DIM].astype(BF16)
            online_update(g, _dot_nt(Qrb, nk) * scale, has_new & mask_n, nv)
            o_s = acc_s[g] / jnp.maximum(l_s[g], 1e-30)
            wk = wst_ref[:, g, :].astype(BF16)
            wv = wst_ref[:, NSA_KV + g, :].astype(BF16)
            nwk = win_ref[0, :, g * HEAD_DIM:(g + 1) * HEAD_DIM].astype(BF16)
            nwv = win_ref[0, :, gw + g * HEAD_DIM:gw + (g + 1) * HEAD_DIM].astype(BF16)
            lw = jnp.where(mask_w, _dot_nt(Qrb, wk) * scale, NEG_INF)
            ln = jnp.where(mask_n, _dot_nt(Qrb, nwk) * scale, NEG_INF)
            mx = jnp.maximum(jnp.max(lw, -1, keepdims=True), jnp.max(ln, -1, keepdims=True))
            mx = jnp.where(mx > NEG_INF, mx, 0.0)
            ew, en = jnp.exp(lw - mx), jnp.exp(ln - mx)
            den = jnp.maximum(jnp.sum(ew, -1, keepdims=True) + jnp.sum(en, -1, keepdims=True), 1e-30)
            o_w = (_dot(ew.astype(BF16), wv) + _dot(en.astype(BF16), nwv)) / den
            gt = _sigmoid(gt_ref[g])
            o_c = ocmp_ref[0, g]
            for r in range(rep):
                rs = slice(r * tp, (r + 1) * tp)
                o = (gt[:, 3 * r:3 * r + 1] * o_c[rs] + gt[:, 3 * r + 1:3 * r + 2] * o_s[rs]
                     + gt[:, 3 * r + 2:3 * r + 3] * o_w[rs])
                c_o = (g * rep + r) * HEAD_DIM
                o_ref[0, :, c_o:c_o + HEAD_DIM] = o.astype(o_ref.dtype)


def _nsa_dec_attn(page_table, sel, zq3, cos, sin, cache4, page_off, new_kv3, wst3, win_off, win3, ocmp, gates4,
                  tv, q_start):
    B, tp, _ = zq3.shape
    bpp = PAGE_SIZE // SLC_BLOCK
    n_past_blk = page_table.shape[1] * bpp
    ktop = sel.shape[-1]
    P = wst3.shape[0] // (win_off[1])
    rows = NSA_REP * tp

    def cache_spec(g, t):
        def imap(b, k, pt, sl):
            s = jnp.minimum(sl[b, g, t, k], n_past_blk - 1)
            return ((page_off + pt[b, s // bpp]) * bpp + s % bpp, 1, 0, 0)
        return pl.BlockSpec((SLC_BLOCK, None, 2 * NSA_KV, HEAD_DIM), imap)

    cache_specs = [cache_spec(g, t) for g in range(NSA_KV) for t in range(tv)]
    kern = functools.partial(_nsa_dec_attn_kernel, tp=tp, tv=tv, n_past_blk=n_past_blk, q_start=q_start, ktop=ktop)
    whole = lambda w: pl.BlockSpec((1, tp, w), lambda b, k, pt, sl: (b, 0, 0))
    return pl.pallas_call(
        kern,
        out_shape=jax.ShapeDtypeStruct((B, tp, NSA_HEADS * HEAD_DIM), BF16),
        grid_spec=pltpu.PrefetchScalarGridSpec(
            num_scalar_prefetch=2,
            grid=(B, ktop),
            in_specs=[whole(NSA_HEADS * HEAD_DIM),
                      pl.BlockSpec((tp, HEAD_DIM), lambda b, k, pt, sl: (0, 0)),
                      pl.BlockSpec((tp, HEAD_DIM), lambda b, k, pt, sl: (0, 0))]
            + cache_specs
            + [whole(4 * NSA_KV * HEAD_DIM),
               pl.BlockSpec((P, 2 * NSA_KV, HEAD_DIM), lambda b, k, pt, sl: (win_off[0] + b, 0, 0)),
               whole(2 * NSA_KV * HEAD_DIM),
               pl.BlockSpec((1, NSA_KV, rows, HEAD_DIM), lambda b, k, pt, sl: (b, 0, 0, 0)),
               pl.BlockSpec((NSA_KV, tp, 128), lambda b, k, pt, sl: (0, b, 0))],
            out_specs=whole(NSA_HEADS * HEAD_DIM),
            scratch_shapes=[pltpu.VMEM((NSA_KV, rows, HEAD_DIM), F32), pltpu.VMEM((NSA_KV, rows, 1), F32),
                            pltpu.VMEM((NSA_KV, rows, 1), F32), pltpu.VMEM((NSA_KV, rows, HEAD_DIM), F32)],
        ),
        compiler_params=_cp("parallel", "arbitrary"),
        name="nsa_dec_attn",
    )(page_table, sel, zq3, cos, sin, *([cache4] * len(cache_specs)), new_kv3, wst3, win3, ocmp, gates4)


def _gmlp_kernel(z_ref, g_ref, b_ref, ws_ref, bst_ref, o_ref, v_ref, *, lc):
    z = _gelu(z_ref[...])
    u, v = z[:, :W_B], z[:, W_B:]
    mu = jnp.mean(v, -1, keepdims=True)
    d = v - mu
    var = jnp.mean(d * d, -1, keepdims=True)
    vn = d * lax.rsqrt(var + 1e-5) * g_ref[...] + b_ref[...]
    v_ref[...] = vn
    vb = vn.astype(BF16)
    ti = lax.broadcasted_iota(I32, (lc, lc), 0)
    si = lax.broadcasted_iota(I32, (lc, lc), 1)
    for g in range(GMLP_GROUPS):
        cs = slice(g * GMLP_GDIM, (g + 1) * GMLP_GDIM)
        w = jnp.where(si <= ti, ws_ref[g], 0.0).astype(BF16)
        mixed = _dot(w, vb[:, cs]) + bst_ref[:, g:g + 1]
        o_ref[:, cs] = (u[:, cs] * mixed).astype(o_ref.dtype)


def _gmlp(zuv, ln_g, ln_b, ws, bs, lc):
    M = zuv.shape[0]
    ws = ws[:, :lc, :lc]
    bst = bs[:, :lc].T
    return pl.pallas_call(
        functools.partial(_gmlp_kernel, lc=lc),
        out_shape=(jax.ShapeDtypeStruct((M, W_B), BF16), jax.ShapeDtypeStruct((M, W_B), F32)),
        grid=(M // lc,),
        in_specs=[pl.BlockSpec((lc, 2 * W_B), lambda i: (i, 0)),
                  pl.BlockSpec((1, W_B), lambda i: (0, 0)),
                  pl.BlockSpec((1, W_B), lambda i: (0, 0)),
                  pl.BlockSpec((GMLP_GROUPS, lc, lc), lambda i: (0, 0, 0)),
                  pl.BlockSpec((lc, GMLP_GROUPS), lambda i: (0, 0))],
        out_specs=(pl.BlockSpec((lc, W_B), lambda i: (i, 0)), pl.BlockSpec((lc, W_B), lambda i: (i, 0))),
        compiler_params=_cp("parallel"),
        name="gmlp",
    )(zuv, ln_g.reshape(1, W_B), ln_b.reshape(1, W_B), ws, bst)


def _mlstm_kernel(bi_ref, bf_ref, q_ref, k_ref, v_ref, og_ref, sm_ref, smt_ref, ng_ref, c0_ref, n0_ref, m0_ref,
                  h_ref, c_ref, n_ref, m_ref, c_s, n_s, m_s, *, lc, valid, ci_col, cf_col):
    c = pl.program_id(1)

    @pl.when(c == 0)
    def _():
        c_s[...] = c0_ref[0]
        n_s[...] = n0_ref[0]
        m_s[...] = m0_ref[0]

    ti = lax.broadcasted_iota(I32, (lc, lc), 0)
    si = lax.broadcasted_iota(I32, (lc, lc), 1)
    tril = (si <= ti) & (si < valid)
    rvalid = lax.broadcasted_iota(I32, (lc, 1), 0) < valid
    last = valid - 1
    kscale = MLSTM_DQK ** -0.5
    for h in range(MLSTM_HEADS):
        ig_c = sm_ref[:, ci_col + h:ci_col + h + 1] + bi_ref[h]
        lf_c = _log_sigmoid(sm_ref[:, cf_col + h:cf_col + h + 1] + bf_ref[h])
        ig_r = smt_ref[0, h:h + 1, :] + bi_ref[h]
        lf_r = _log_sigmoid(smt_ref[0, MLSTM_HEADS + h:MLSTM_HEADS + h + 1, :] + bf_ref[h])
        F_c = jnp.sum(jnp.where(si <= ti, lf_r, 0.0), axis=1, keepdims=True)
        F_r = jnp.sum(jnp.where(ti <= si, lf_c, 0.0), axis=0, keepdims=True)
        D = jnp.where(tril, F_c - F_r + ig_r, NEG_INF)
        m_prev = m_s[h:h + 1, 0:1]
        m_inter = F_c + m_prev
        m_t = jnp.maximum(m_inter, jnp.max(D, axis=1, keepdims=True))
        qh = q_ref[:, h * MLSTM_DQK:(h + 1) * MLSTM_DQK]
        ks = k_ref[:, h * MLSTM_DQK:(h + 1) * MLSTM_DQK] * kscale
        vh = v_ref[:, h * MLSTM_DV:(h + 1) * MLSTM_DV]
        qb, kb, vb = qh.astype(BF16), ks.astype(BF16), vh.astype(BF16)
        S = _dot_nt(qb, kb) * jnp.exp(D - m_t)
        dec = jnp.exp(m_inter - m_t)
        C = c_s[h]
        n_row = n_s[h:h + 1, :]
        num = _dot(S.astype(BF16), vb) + dec * _dot_nt(qb, C.astype(BF16))
        den = jnp.sum(S, axis=1, keepdims=True) + dec * jnp.sum(qh * n_row, axis=1, keepdims=True)
        hh = num / jnp.maximum(jnp.abs(den), 1.0)
        m_new = m_t[last:last + 1, :]
        F_last = F_c[last:last + 1, :]
        wl_c = jnp.where(rvalid, jnp.exp(F_last - F_c + ig_c - m_new), 0.0)
        dl = jnp.exp(F_last + m_prev - m_new)
        c_s[h] = dl * C + _dot_tn((vh * wl_c).astype(BF16), kb)
        n_s[h:h + 1, :] = dl * n_row + jnp.sum(wl_c * ks, axis=0, keepdims=True)
        m_s[h:h + 1, :] = jnp.broadcast_to(m_new, (1, 128))
        mu = jnp.mean(hh, -1, keepdims=True)
        d = hh - mu
        var = jnp.mean(d * d, -1, keepdims=True)
        hn = d * lax.rsqrt(var + 1e-5) * ng_ref[h:h + 1, :]
        vs = slice(h * MLSTM_DV, (h + 1) * MLSTM_DV)
        h_ref[:, vs] = (_sigmoid(og_ref[:, vs]) * hn).astype(h_ref.dtype)

    @pl.when(c == pl.num_programs(1) - 1)
    def _():
        c_ref[0] = c_s[...]
        n_ref[0] = n_s[...]
        m_ref[0] = m_s[...]


def _mlstm(zcq, zck, zcv, zco, zsm, b_i, b_f, norm_g, C0, n0, m0, B, T, lc, valid, ci_col, cf_col):
    nc = T // lc
    H, DQ, DV = MLSTM_HEADS, MLSTM_DQK, MLSTM_DV
    smt = jnp.concatenate([zsm[:, ci_col:ci_col + H], zsm[:, cf_col:cf_col + H]], -1)
    smt = smt.reshape(B * nc, lc, 2 * H).transpose(0, 2, 1)
    m0b = jnp.broadcast_to(m0[..., None], (B, H, 128))
    row = lambda w: pl.BlockSpec((lc, w), lambda b, c: (b * nc + c, 0))
    smem = pl.BlockSpec(memory_space=pltpu.SMEM)
    kern = functools.partial(_mlstm_kernel, lc=lc, valid=valid, ci_col=ci_col, cf_col=cf_col)
    c_spec = pl.BlockSpec((1, H, DV, DQ), lambda b, c: (b, 0, 0, 0))
    n_spec = pl.BlockSpec((1, H, DQ), lambda b, c: (b, 0, 0))
    h, C, n, m = pl.pallas_call(
        kern,
        out_shape=(jax.ShapeDtypeStruct((B * T, H * DV), BF16), jax.ShapeDtypeStruct((B, H, DV, DQ), F32),
                   jax.ShapeDtypeStruct((B, H, DQ), F32), jax.ShapeDtypeStruct((B, H, 128), F32)),
        grid=(B, nc),
        in_specs=[smem, smem, row(H * DQ), row(H * DQ), row(H * DV), row(H * DV), row(128),
                  pl.BlockSpec((1, 2 * H, lc), lambda b, c: (b * nc + c, 0, 0)),
                  pl.BlockSpec((H, DV), lambda b, c: (0, 0)), c_spec, n_spec, n_spec],
        out_specs=(row(H * DV), c_spec, n_spec, n_spec),
        scratch_shapes=[pltpu.VMEM((H, DV, DQ), F32), pltpu.VMEM((H, DQ), F32), pltpu.VMEM((H, 128), F32)],
        compiler_params=_cp("parallel", "arbitrary"),
        name="mlstm",
    )(b_i, b_f, zcq, zck, zcv, zco, zsm, smt, norm_g, C0, n0, m0b)
    return h, C, n, m[..., 0]


def _merge_kernel(a_ref, b_ref, c_ref, wa_ref, wb_ref, wc_ref, ga_ref, gb_ref, gc_ref, o_ref):
    o = (_sigmoid(ga_ref[...]) * _dot(a_ref[...], wa_ref[...])
         + _sigmoid(gb_ref[...]) * _dot(b_ref[...], wb_ref[...])
         + _sigmoid(gc_ref[...]) * _dot(c_ref[...], wc_ref[...]))
    o_ref[...] = o.astype(o_ref.dtype)


def _merge(oa, ob, oc, wa, wb, wc, layer, zmg, tm=512, tn=512):
    M, K = oa.shape
    D = wa.shape[-1]
    tm, tn = min(tm, M), min(tn, D)
    nj = D // tn
    x_spec = pl.BlockSpec((tm, K), lambda i, j: (i, 0))
    w_spec = _w_spec(wa, layer, (K, tn), lambda i, j: (0, j))
    g_spec = lambda o: pl.BlockSpec((tm, tn), lambda i, j: (i, o * nj + j))
    return pl.pallas_call(
        _merge_kernel,
        out_shape=jax.ShapeDtypeStruct((M, D), BF16),
        grid=(M // tm, nj),
        in_specs=[x_spec, x_spec, x_spec, w_spec, w_spec, w_spec, g_spec(0), g_spec(1), g_spec(2)],
        out_specs=pl.BlockSpec((tm, tn), lambda i, j: (i, j)),
        compiler_params=_cp("parallel", "parallel"),
        name="merge",
    )(oa, ob, oc, wa, wb, wc, zmg, zmg, zmg)


def _top16(s, ids, big, track_pos):
    vals, idxs = [], []
    pos = jnp.full(s.shape, -1, I32) if track_pos else None
    for it in range(PEER_TOPK):
        mx = jnp.max(s, axis=0, keepdims=True)
        idx = jnp.min(jnp.where(s == mx, ids, big), axis=0, keepdims=True)
        one = ids == idx
        s = jnp.where(one, NEG_INF, s)
        if track_pos:
            pos = jnp.where(one, it, pos)
        vals.append(mx)
        idxs.append(idx)
    return vals, idxs, pos


def _peer_route_kernel(q_ref, k1_ref, k2_ref, s1_ref, s2_ref, pk_ref):
    tm = q_ref.shape[0]
    half = PEER_DQ // 2
    K = PEER_TOPK
    k1 = k1_ref[...].astype(BF16)
    k2 = k2_ref[...].astype(BF16)
    kid = lax.broadcasted_iota(I32, (N_KEYS, 1), 0)
    i16 = lax.broadcasted_iota(I32, (K, 1), 0)
    i8 = lax.broadcasted_iota(I32, (8, 1), 0)
    cid = jnp.concatenate([i16, K + i8, 2 * K + i8, 3 * K + i8, i8 * K, i8 * K + 1, i8 * K + 2, (8 + i8) * K], axis=0)
    for h in range(PEER_HEADS):
        q1 = q_ref[:, h * PEER_DQ:h * PEER_DQ + half].astype(BF16)
        q2 = q_ref[:, h * PEER_DQ + half:(h + 1) * PEER_DQ].astype(BF16)
        s1 = _dot_nt(k1, q1)
        s2 = _dot_nt(k2, q2)
        v1, _, pos1 = _top16(s1, kid, N_KEYS, True)
        v2, _, pos2 = _top16(s2, kid, N_KEYS, True)
        v1m = jnp.concatenate(v1, axis=0)
        v2m = jnp.concatenate(v2, axis=0)
        pieces = [v1[0] + v2m, v1[1] + v2m[0:8], v1[2] + v2m[0:8], v1[3] + v2m[0:8]]
        for kk in range(3):
            pieces.append(jnp.where(i8 >= 4, v1m[0:8] + v2[kk], NEG_INF))
        pieces.append(v1m[8:16] + v2[0])
        cv, cj, _ = _top16(jnp.concatenate(pieces, axis=0), cid, K * K, False)
        z = jnp.zeros((1, tm), F32)
        wbits = jnp.zeros((K, tm), I32)
        for it in range(K):
            z = z + jnp.exp(cv[it] - cv[0])
            ii = jnp.right_shift(cj[it], 4)
            kk = cj[it] & (K - 1)
            wbits = jnp.where(i16 == ii, wbits | jnp.left_shift(1, kk), wbits)
        in1, in2 = pos1 >= 0, pos2 >= 0
        s1_ref[h] = jnp.where(in1, (s1 - cv[0]) * LOG2E - jnp.log2(z), NEG_INF)
        s2_ref[h] = jnp.where(in2, s2 * LOG2E, NEG_INF)
        rw = jnp.zeros((N_KEYS, tm), I32)
        for i in range(K):
            rw = jnp.where(pos1 == i, wbits[i:i + 1, :], rw)
        b2 = jnp.where(in2, jnp.left_shift(1, jnp.maximum(pos2, 0)), 0)
        pk_ref[h] = b2 | jnp.left_shift(rw, 16)


def _peer_route(q, k1, k2, tm=256):
    M = q.shape[0]
    tm = min(tm, M)
    spec = pl.BlockSpec((PEER_HEADS, N_KEYS, tm), lambda i: (0, 0, i))
    shp = lambda dt: jax.ShapeDtypeStruct((PEER_HEADS, N_KEYS, M), dt)
    return pl.pallas_call(
        _peer_route_kernel,
        out_shape=(shp(F32), shp(F32), shp(I32)),
        grid=(M // tm,),
        in_specs=[pl.BlockSpec((tm, PEER_HEADS * PEER_DQ), lambda i: (i, 0)),
                  pl.BlockSpec((N_KEYS, PEER_DQ // 2), lambda i: (0, 0)),
                  pl.BlockSpec((N_KEYS, PEER_DQ // 2), lambda i: (0, 0))],
        out_specs=(spec, spec, spec),
        compiler_params=_cp("parallel"),
        name="peer_route",
    )(q, k1, k2)


PEER_SUB = 256


def _peer_dense_kernel(x_ref, u_ref, v_ref, s1_ref, s2_ref, pk_ref, o_ref, *, te):
    e = pl.program_id(1)

    @pl.when(e == 0)
    def _():
        o_ref[...] = jnp.zeros_like(o_ref)

    x = x_ref[...]
    hs = [_dot_nt(x, u_ref[c * PEER_SUB:(c + 1) * PEER_SUB, :]) for c in range(te // PEER_SUB)]
    per = te // N_KEYS
    gates = []
    for al in range(per):
        a = e * per + al
        gate = None
        for hd in range(PEER_HEADS):
            wt = jnp.exp2(s1_ref[hd, pl.ds(a, 1), :] + s2_ref[hd])
            arow = pk_ref[hd, pl.ds(a, 1), :]
            abits = lax.shift_right_logical(arow, jnp.full_like(arow, 16))
            g = jnp.where((abits & pk_ref[hd]) != 0, wt, 0.0)
            gate = g if gate is None else gate + g
        gates.append(gate.T)
    gps = PEER_SUB // N_KEYS
    acts = [(_gelu(hs[c]) * jnp.concatenate(gates[c * gps:(c + 1) * gps], axis=1)).astype(BF16)
            for c in range(te // PEER_SUB)]
    o_ref[...] += _dot(jnp.concatenate(acts, axis=1), v_ref[...])


def _peer_dense(xb, u, v, layer, s1l, s2l, pk, tm=512, te=512):
    M, D = xb.shape
    NE = u.shape[-2]
    tm = min(tm, M)
    n_e = NE // te
    once = pl.Buffered(buffer_count=1)
    rspec = pl.BlockSpec((PEER_HEADS, N_KEYS, tm), lambda i, e: (0, 0, i), pipeline_mode=once)
    return pl.pallas_call(
        functools.partial(_peer_dense_kernel, te=te),
        out_shape=jax.ShapeDtypeStruct((M, D), F32),
        grid=(M // tm, n_e),
        in_specs=[pl.BlockSpec((tm, D), lambda i, e: (i, 0), pipeline_mode=once),
                  _w_spec(u, layer, (te, D), lambda i, e: (e, 0)),
                  _w_spec(v, layer, (te, D), lambda i, e: (e, 0)),
                  rspec, rspec, rspec],
        out_specs=pl.BlockSpec((tm, D), lambda i, e: (i, 0)),
        compiler_params=_cp("parallel", "arbitrary"),
        name="peer_dense",
    )(xb, u, v, s1l, s2l, pk)


def _peer(x1, x1b, lw, alpha):
    M, D = x1.shape
    mp = -(-M // 128) * 128
    xb = x1b if mp == M else jnp.pad(x1b, ((0, mp - M), (0, 0)))
    l = lw["layer"]
    q = _mm(xb, lw["peer_wq"], l)
    s1l, s2l, pk = _peer_route(q, lw["peer_k1"], lw["peer_k2"])
    y = _peer_dense(xb, lw["peer_u"], lw["peer_v"], l, s1l, s2l, pk)[:M]
    return _res_ln(x1, y, lw["ln2_g"], lw["ln2_b"], alpha)


Z_SEGS = ("a_q", "a_kv", "a_g", "b_uv", "c_q", "c_k", "c_v", "c_i", "c_f", "c_o", "mg")
STACKED_BF16 = ("w_br_a", "w_br_b", "w_br_c", "w_out", "peer_wq", "peer_u", "peer_v")
GATE_COL = 0
CI_COL = 3 * NSA_HEADS
CF_COL = CI_COL + MLSTM_HEADS


def _prep_layer_weights(l, w):
    D = w["w_in"].shape[1]
    sizes = (NSA_HEADS * HEAD_DIM, 6 * NSA_KV * HEAD_DIM, 3 * NSA_HEADS, 2 * W_B,
             MLSTM_HEADS * MLSTM_DQK, MLSTM_HEADS * MLSTM_DQK, MLSTM_HEADS * MLSTM_DV, MLSTM_HEADS, MLSTM_HEADS,
             MLSTM_HEADS * MLSTM_DV, 3 * D)
    offs = np.concatenate([[0], np.cumsum(sizes)])
    w_in = w["w_in"][l]
    seg = {n: (int(offs[i]), int(offs[i + 1])) for i, n in enumerate(Z_SEGS)}
    cut = lambda n: w_in[:, seg[n][0]:seg[n][1]].astype(BF16)
    lw = {n: cut(n) for n in ("a_q", "a_kv", "b_uv", "c_q", "c_k", "c_v", "c_o", "mg")}
    small = jnp.concatenate([w_in[:, seg[n][0]:seg[n][1]] for n in ("a_g", "c_i", "c_f")], -1)
    lw["small"] = jnp.pad(small, ((0, 0), (0, 128 - small.shape[1]))).astype(BF16)
    lw["layer"] = l
    for n in STACKED_BF16:
        lw[n] = w[n + "_bf16"]
    for n in ("cmp_pe", "cmp_w1", "cmp_b1", "cmp_w2", "gmlp_ln_g", "gmlp_ln_b", "gmlp_ws", "gmlp_bs",
              "mlstm_b_i", "mlstm_b_f", "mlstm_norm_g", "ln1_g", "ln1_b", "peer_k1", "peer_k2", "ln2_g", "ln2_b"):
        lw[n] = w[n][l]
    return lw


def _gates_by_group(zsm):
    M = zsm.shape[0]
    gt = zsm[:, GATE_COL:GATE_COL + 3 * NSA_HEADS].reshape(M, NSA_KV, 3 * NSA_REP).transpose(1, 0, 2)
    return jnp.pad(gt, ((0, 0), (0, 0), (0, 128 - 3 * NSA_REP)))


def _layer(x, xb, B, T, tv, lw, alpha, dec):
    M, D = x.shape
    z = {n: _mm(xb, lw[n]) for n in ("a_q", "a_kv", "b_uv", "c_q", "c_k", "c_v", "c_o", "mg", "small")}
    zsm = z["small"]
    q_start = 0 if dec is None else dec["q_start"]
    cos, sin = _rope_tables(q_start + jnp.arange(T))
    new_kv, win = _kvprep(z["a_kv"], cos, sin, T)
    gates4 = _gates_by_group(zsm)
    cmp_w = (lw["cmp_pe"], lw["cmp_w1"], lw["cmp_b1"], lw["cmp_w2"])
    gw = NSA_KV * HEAD_DIM
    if dec is None:
        n_pages = T // PAGE_SIZE
        ident = jnp.arange(B * n_pages, dtype=I32).reshape(B, n_pages)
        src3 = z["a_kv"][:, :2 * gw].reshape(M, 2 * NSA_KV, HEAD_DIM)
        kvc = _compress(src3, ident, 0, *cmp_w)
        o_a = _nsa_prefill(z["a_q"], new_kv, win, kvc, gates4, cos, sin, B, T)
        lc_g, lc_m = CHUNK, math.gcd(T, MLSTM_KERNEL_CHUNK)
        C0 = jnp.zeros((B, MLSTM_HEADS, MLSTM_DV, MLSTM_DQK), F32)
        n0 = jnp.zeros((B, MLSTM_HEADS, MLSTM_DQK), F32)
        m0 = jnp.zeros((B, MLSTM_HEADS), F32)
        valid = lc_m
    else:
        page_table = dec["page_table"]
        kvc = _compress(dec["cache3"], page_table, dec["page_off"], *cmp_w)
        L = q_start + tv
        n_cmp = (L - CMP_BLOCK) // CMP_STRIDE + 1
        n_slc = -(-L // SLC_BLOCK)
        assert n_cmp <= q_start // CMP_STRIDE - 1 and n_slc == q_start // SLC_BLOCK + 1
        assert q_start % SLC_BLOCK == 0 and tv <= T
        zq3 = z["a_q"].reshape(B, T, -1)
        ocmp, sel = _nsa_dec_sel(zq3, kvc, n_cmp, n_slc, q_start)
        sel = sel[:, :, :tv, :min(TOP_BLOCKS, n_slc)]
        o_a = _nsa_dec_attn(page_table, sel, zq3, cos, sin, dec["cache4"], dec["page_off"], new_kv.reshape(B, T, -1),
                            dec["wst3"], dec["win_off"], win.reshape(B, T, -1), ocmp, gates4, tv, q_start)
        o_a = o_a.reshape(M, -1)
        lc_g = lc_m = T
        C0, n0, m0 = dec["C"], dec["n"], dec["m"]
        valid = tv
    o_b, v_rows = _gmlp(z["b_uv"], lw["gmlp_ln_g"], lw["gmlp_ln_b"], lw["gmlp_ws"], lw["gmlp_bs"], lc_g)
    o_c, C, n, m = _mlstm(z["c_q"], z["c_k"], z["c_v"], z["c_o"], zsm, lw["mlstm_b_i"], lw["mlstm_b_f"],
                          lw["mlstm_norm_g"], C0, n0, m0, B, T, lc_m, valid, CI_COL, CF_COL)
    merged = _merge(o_a, o_b, o_c, lw["w_br_a"], lw["w_br_b"], lw["w_br_c"], lw["layer"], z["mg"])
    x1, x1b = _res_ln(x, _mm(merged, lw["w_out"], lw["layer"]), lw["ln1_g"], lw["ln1_b"], alpha)
    y, yb = _peer(x1, x1b, lw, alpha)
    return y, yb, new_kv, win, v_rows, C, n, m


def kernel(x_prompt, x_sample, cache_nsa_kv, state_nsa_win, state_mlstm_C, state_mlstm_n, state_mlstm_m,
           page_table, w_in, cmp_pe, cmp_w1, cmp_b1, cmp_w2, gmlp_ln_g, gmlp_ln_b, gmlp_ws, gmlp_bs,
           mlstm_b_i, mlstm_b_f, mlstm_norm_g, w_br_a, w_br_b, w_br_c, w_out, ln1_g, ln1_b,
           peer_wq, peer_k1, peer_k2, peer_u, peer_v, ln2_g, ln2_b):
    w = dict(w_in=w_in, cmp_pe=cmp_pe, cmp_w1=cmp_w1, cmp_b1=cmp_b1, cmp_w2=cmp_w2, gmlp_ln_g=gmlp_ln_g,
             gmlp_ln_b=gmlp_ln_b, gmlp_ws=gmlp_ws, gmlp_bs=gmlp_bs, mlstm_b_i=mlstm_b_i, mlstm_b_f=mlstm_b_f,
             mlstm_norm_g=mlstm_norm_g, w_br_a=w_br_a, w_br_b=w_br_b, w_br_c=w_br_c, w_out=w_out, ln1_g=ln1_g,
             ln1_b=ln1_b, peer_wq=peer_wq, peer_k1=peer_k1, peer_k2=peer_k2, peer_u=peer_u, peer_v=peer_v,
             ln2_g=ln2_g, ln2_b=ln2_b)
    for n in STACKED_BF16:
        w[n + "_bf16"] = w[n].astype(BF16)
    depth = w_in.shape[0]
    alpha = (2 * depth) ** 0.25
    Bp, Tp, D = x_prompt.shape
    Bs, Ts, _ = x_sample.shape
    Tpad = 8
    past_len = page_table.shape[1] * PAGE_SIZE
    n_pool = cache_nsa_kv.shape[1]
    P = state_nsa_win.shape[2]
    cache3 = cache_nsa_kv.reshape(depth * n_pool * PAGE_SIZE, 4 * NSA_KV, HEAD_DIM)
    cache4 = cache_nsa_kv.reshape(depth * n_pool * PAGE_SIZE, 2, 2 * NSA_KV, HEAD_DIM)
    wst3 = state_nsa_win.reshape(depth * Bs * P, 2 * NSA_KV, HEAD_DIM)
    yp = x_prompt.reshape(Bp * Tp, D)
    ys = jnp.pad(x_sample, ((0, 0), (0, Tpad - Ts), (0, 0))).reshape(Bs * Tpad, D)
    ypb, ysb = yp.astype(BF16), ys.astype(BF16)
    outs = [[] for _ in range(11)]
    wkeep_p = min(WINDOW, Tp)
    for l in range(depth):
        lw = _prep_layer_weights(l, w)
        yp, ypb, kv, win, _, C, n, m = _layer(yp, ypb, Bp, Tp, Tp, lw, alpha, None)
        outs[0].append(kv.reshape(Bp, Tp, 4, NSA_KV, HEAD_DIM))
        outs[2].append(win.reshape(Bp, Tp, 2, NSA_KV, HEAD_DIM)[:, Tp - wkeep_p:])
        outs[5].append(C); outs[6].append(n); outs[7].append(m)
        dec = dict(q_start=past_len, cache3=cache3, cache4=cache4, page_off=l * n_pool, page_table=page_table,
                   wst3=wst3, win_off=(l * Bs, depth * Bs), C=state_mlstm_C[l], n=state_mlstm_n[l], m=state_mlstm_m[l])
        ys, ysb, kv, win, v_rows, C, n, m = _layer(ys, ysb, Bs, Tpad, Ts, lw, alpha, dec)
        outs[1].append(kv.reshape(Bs, Tpad, 4, NSA_KV, HEAD_DIM)[:, :Ts])
        win_new = win.reshape(Bs, Tpad, 2, NSA_KV, HEAD_DIM)[:, :Ts]
        outs[3].append(jnp.concatenate([state_nsa_win[l], win_new], 1)[:, Ts:])
        outs[4].append(v_rows.reshape(Bs, Tpad, W_B)[:, :Ts])
        outs[8].append(C); outs[9].append(n); outs[10].append(m)
    y_prompt = yp.reshape(Bp, Tp, D)
    y_sample = ys.reshape(Bs, Tpad, D)[:, :Ts]
    st = [jnp.stack(o) for o in outs]
    return (y_prompt, y_sample, st[0], st[1], st[2], st[3], st[4], st[5], st[6], st[7], st[8], st[9], st[10])
```

```python
import functools
import math

import numpy as np
import jax
import jax.numpy as jnp
from jax import lax
from jax.experimental import pallas as pl
from jax.experimental.pallas import tpu as pltpu

F32 = jnp.float32
BF16 = jnp.bfloat16
I32 = jnp.int32

HEAD_DIM = 128
ROT_DIM = HEAD_DIM // 4
ROPE_THETA = 500000.0
NSA_HEADS = 16
NSA_KV = 4
NSA_REP = NSA_HEADS // NSA_KV
CMP_BLOCK = 32
CMP_STRIDE = 16
SLC_BLOCK = 64
SLC_SHIFT = 6
TOP_BLOCKS = 16
WINDOW = 512
FORCE_SCORE = 1.0e9
CHUNK = 128
GMLP_GROUPS = 16
GMLP_GDIM = 128
W_B = GMLP_GROUPS * GMLP_GDIM
MLSTM_HEADS = 8
MLSTM_DQK = 128
MLSTM_DV = 256
MLSTM_KERNEL_CHUNK = 256
PEER_HEADS = 8
PEER_DQ = 256
N_KEYS = 128
PEER_TOPK = 16
PAGE_SIZE = 128
LOG2E = 1.4426950408889634
NEG_INF = float("-inf")

VMEM_LIMIT_BYTES = 56 * 1024 * 1024


def _cp(*sem):
    return pltpu.CompilerParams(dimension_semantics=sem, vmem_limit_bytes=VMEM_LIMIT_BYTES)


def _gelu(x):
    return 0.5 * x * (1.0 + jnp.tanh(0.7978845608028654 * (x + 0.044715 * (x * x * x))))


def _sigmoid(x):
    return 1.0 / (1.0 + jnp.exp(-x))


def _log_sigmoid(x):
    return jnp.minimum(x, 0.0) - jnp.log(1.0 + jnp.exp(-jnp.abs(x)))


def _dot(a, b):
    return jnp.dot(a, b, preferred_element_type=F32)


def _dot_nt(a, b):
    return lax.dot_general(a, b, (((1,), (1,)), ((), ())), preferred_element_type=F32)


def _dot_tn(a, b):
    return lax.dot_general(a, b, (((0,), (0,)), ((), ())), preferred_element_type=F32)


def _masked_softmax(lg, mask):
    lg = jnp.where(mask, lg, NEG_INF)
    mx = jnp.max(lg, -1, keepdims=True)
    mx = jnp.where(mx > NEG_INF, mx, 0.0)
    e = jnp.exp(lg - mx)
    return e / jnp.maximum(jnp.sum(e, -1, keepdims=True), 1e-30)


def _masked_attend(lg, mask, v):
    lg = jnp.where(mask, lg, NEG_INF)
    mx = jnp.max(lg, -1, keepdims=True)
    mx = jnp.where(mx > NEG_INF, mx, 0.0)
    e = jnp.exp(lg - mx)
    den = jnp.maximum(jnp.sum(e, -1, keepdims=True), 1e-30)
    return _dot(e.astype(BF16), v) / den


def _split3_dot(p, m_bf16):
    hi = p.astype(BF16)
    r1 = p - hi.astype(F32)
    mid = r1.astype(BF16)
    lo = (r1 - mid.astype(F32)).astype(BF16)
    return _dot(hi, m_bf16) + _dot(mid, m_bf16) + _dot(lo, m_bf16)


def _rope(x, cos, sin):
    lane = lax.broadcasted_iota(I32, x.shape, 1)
    up = pltpu.roll(x, HEAD_DIM - ROT_DIM // 2, 1)
    dn = pltpu.roll(x, ROT_DIM // 2, 1)
    return x * cos + jnp.where(lane < ROT_DIM // 2, up, dn) * sin


def _rope_tables(pos):
    inv = ROPE_THETA ** (-jnp.arange(0, ROT_DIM, 2, dtype=F32) / ROT_DIM)
    ang = pos.astype(F32)[:, None] * inv[None, :]
    c, s = jnp.cos(ang), jnp.sin(ang)
    n = pos.shape[0]
    cos = jnp.concatenate([c, c, jnp.ones((n, HEAD_DIM - ROT_DIM), F32)], -1)
    sin = jnp.concatenate([-s, s, jnp.zeros((n, HEAD_DIM - ROT_DIM), F32)], -1)
    return cos, sin


def _mm_kernel(x_ref, w_ref, o_ref, acc_ref):
    k = pl.program_id(2)

    @pl.when(k == 0)
    def _():
        acc_ref[...] = jnp.zeros_like(acc_ref)

    acc_ref[...] += _dot(x_ref[...], w_ref[...])

    @pl.when(k == pl.num_programs(2) - 1)
    def _():
        o_ref[...] = acc_ref[...].astype(o_ref.dtype)


def _w_spec(w, layer, block, imap):
    if w.ndim == 2:
        return pl.BlockSpec(block, imap)
    return pl.BlockSpec((None,) + block, lambda *a: (layer,) + imap(*a))


def _mm(x, w, layer=None, out_dtype=F32, tm=1024, tn=512, tk=4096):
    M, K = x.shape
    N = w.shape[-1]
    tm, tn, tk = min(tm, M), min(tn, N), min(tk, K)
    assert M % tm == 0 and N % tn == 0 and K % tk == 0, (x.shape, w.shape)
    return pl.pallas_call(
        _mm_kernel,
        out_shape=jax.ShapeDtypeStruct((M, N), out_dtype),
        grid=(M // tm, N // tn, K // tk),
        in_specs=[pl.BlockSpec((tm, tk), lambda i, j, k: (i, k)),
                  _w_spec(w, layer, (tk, tn), lambda i, j, k: (k, j))],
        out_specs=pl.BlockSpec((tm, tn), lambda i, j, k: (i, j)),
        scratch_shapes=[pltpu.VMEM((tm, tn), F32)],
        compiler_params=_cp("parallel", "parallel", "arbitrary"),
        name="mm",
    )(x, w)


def _res_ln_kernel(x_ref, y_ref, g_ref, b_ref, o_ref, ob_ref, *, alpha):
    v = alpha * x_ref[...] + y_ref[...]
    mu = jnp.mean(v, -1, keepdims=True)
    d = v - mu
    var = jnp.mean(d * d, -1, keepdims=True)
    o = d * lax.rsqrt(var + 1e-5) * g_ref[...] + b_ref[...]
    o_ref[...] = o
    ob_ref[...] = o.astype(BF16)


def _res_ln(x, y, g, b, alpha, tm=256):
    M, D = x.shape
    tm = min(tm, M)
    row = pl.BlockSpec((tm, D), lambda i: (i, 0))
    vec = pl.BlockSpec((1, D), lambda i: (0, 0))
    return pl.pallas_call(
        functools.partial(_res_ln_kernel, alpha=alpha),
        out_shape=(jax.ShapeDtypeStruct((M, D), F32), jax.ShapeDtypeStruct((M, D), BF16)),
        grid=(M // tm,),
        in_specs=[row, row, vec, vec],
        out_specs=(row, row),
        compiler_params=_cp("parallel"),
        name="res_ln",
    )(x, y, g.reshape(1, D), b.reshape(1, D))


def _kvprep_kernel(z_ref, cos_ref, sin_ref, kv_ref, win_ref):
    cos, sin = cos_ref[...], sin_ref[...]
    gw = NSA_KV * HEAD_DIM
    kv_ref[:, 0:2 * gw] = z_ref[:, 0:2 * gw]
    kv_ref[:, 3 * gw:4 * gw] = z_ref[:, 3 * gw:4 * gw]
    win_ref[:, gw:2 * gw] = z_ref[:, 5 * gw:6 * gw]
    for g in range(NSA_KV):
        a = 2 * gw + g * HEAD_DIM
        kv_ref[:, a:a + HEAD_DIM] = _rope(z_ref[:, a:a + HEAD_DIM], cos, sin)
        a = 4 * gw + g * HEAD_DIM
        win_ref[:, g * HEAD_DIM:(g + 1) * HEAD_DIM] = _rope(z_ref[:, a:a + HEAD_DIM], cos, sin)


def _kvprep(zkv, cos, sin, T, tm=256):
    M = zkv.shape[0]
    tm = min(tm, T)
    nt = T // tm
    gw = NSA_KV * HEAD_DIM
    return pl.pallas_call(
        _kvprep_kernel,
        out_shape=(jax.ShapeDtypeStruct((M, 4 * gw), F32), jax.ShapeDtypeStruct((M, 2 * gw), F32)),
        grid=(M // tm,),
        in_specs=[pl.BlockSpec((tm, 6 * gw), lambda i: (i, 0)),
                  pl.BlockSpec((tm, HEAD_DIM), lambda i: (i % nt, 0)),
                  pl.BlockSpec((tm, HEAD_DIM), lambda i: (i % nt, 0))],
        out_specs=(pl.BlockSpec((tm, 4 * gw), lambda i: (i, 0)),
                   pl.BlockSpec((tm, 2 * gw), lambda i: (i, 0))),
        compiler_params=_cp("parallel"),
        name="kvprep",
    )(zkv, cos, sin)


CMP_PAGES = 8
PREFILL_SPAN = 512


def _cmp_bias_kernel(pe_ref, w1f_ref, b1_ref, o_ref):
    for kv in range(2):
        b = b1_ref[kv:kv + 1, :]
        for half in range(2):
            b = b + _dot(pe_ref[kv, half].astype(BF16), w1f_ref[kv, half])[0:1, :]
        o_ref[:, kv * HEAD_DIM:(kv + 1) * HEAD_DIM] = jnp.broadcast_to(b, (8, HEAD_DIM))


def _cmp_kernel(pt_ref, *refs, P):
    page_refs = refs[:P + 1]
    wa_ref, wb_ref, bias_ref, w2_ref, o_ref, res_s = refs[P + 1:]
    hp = PAGE_SIZE // CMP_STRIDE
    sg = 2 * NSA_KV
    rows = P * hp * sg
    acc_a = jnp.zeros((rows, 2 * HEAD_DIM), F32)
    acc_b = jnp.zeros((rows, 2 * HEAD_DIM), F32)
    for j in range(CMP_STRIDE):
        la, lb = [], []
        for p in range(P):
            la.append(page_refs[p][pl.ds(j, hp, stride=CMP_STRIDE), 0:sg, :].reshape(hp * sg, HEAD_DIM))
            lb.append(page_refs[p][pl.ds(j + CMP_STRIDE, hp - 1, stride=CMP_STRIDE), 0:sg, :]
                      .reshape((hp - 1) * sg, HEAD_DIM))
            lb.append(page_refs[p + 1][j, 0:sg, :])
        acc_a = acc_a + _dot(jnp.concatenate(la, axis=0).astype(BF16), wa_ref[j])
        acc_b = acc_b + _dot(jnp.concatenate(lb, axis=0).astype(BF16), wb_ref[j])
    h = _gelu(acc_a + acc_b + bias_ref[0:1, :])
    ok = _dot(h[:, :HEAD_DIM].astype(BF16), w2_ref[0].astype(BF16))
    ov = _dot(h[:, HEAD_DIM:].astype(BF16), w2_ref[1].astype(BF16))
    is_k = (lax.broadcasted_iota(I32, (rows, 1), 0) & (sg - 1)) < NSA_KV
    res_s[...] = jnp.where(is_k, ok, ov)
    for s in range(sg):
        o_ref[0, s] = res_s[pl.ds(s, P * hp, stride=sg), :]


def _compress(src3, page_table, page_off, cmp_pe, cmp_w1, cmp_b1, cmp_w2):
    B, n_pages = page_table.shape
    P = math.gcd(CMP_PAGES, n_pages)
    hp = PAGE_SIZE // CMP_STRIDE
    sg = 2 * NSA_KV
    src4 = src3.reshape(src3.shape[0], src3.shape[1] // sg, sg, HEAD_DIM)
    pe_flat = jnp.broadcast_to(cmp_pe.reshape(2, 2, 1, CMP_STRIDE * HEAD_DIM), (2, 2, 8, CMP_STRIDE * HEAD_DIM))
    w1_flat = cmp_w1.reshape(2, 2, CMP_STRIDE * HEAD_DIM, HEAD_DIM).astype(BF16)
    bias = pl.pallas_call(
        _cmp_bias_kernel,
        out_shape=jax.ShapeDtypeStruct((8, 2 * HEAD_DIM), F32),
        name="cmp_bias",
    )(pe_flat, w1_flat, cmp_b1)
    wa = jnp.concatenate([cmp_w1[0, :CMP_STRIDE], cmp_w1[1, :CMP_STRIDE]], -1).astype(BF16)
    wb = jnp.concatenate([cmp_w1[0, CMP_STRIDE:], cmp_w1[1, CMP_STRIDE:]], -1).astype(BF16)

    def page_spec(p):
        return pl.BlockSpec((PAGE_SIZE, None, sg, HEAD_DIM),
                            lambda b, c, pt: (page_off + pt[b, jnp.minimum(c * P + p, n_pages - 1)], 0, 0, 0))

    full = lambda shape: pl.BlockSpec(shape, lambda b, c, pt: (0,) * len(shape))
    return pl.pallas_call(
        functools.partial(_cmp_kernel, P=P),
        out_shape=jax.ShapeDtypeStruct((B, sg, n_pages * hp, HEAD_DIM), F32),
        grid_spec=pltpu.PrefetchScalarGridSpec(
            num_scalar_prefetch=1,
            grid=(B, n_pages // P),
            in_specs=[page_spec(p) for p in range(P + 1)]
            + [full(wa.shape), full(wb.shape), full(bias.shape), full(cmp_w2.shape)],
            out_specs=pl.BlockSpec((1, sg, P * hp, HEAD_DIM), lambda b, c, pt: (b, 0, c, 0)),
            scratch_shapes=[pltpu.VMEM((P * hp * sg, HEAD_DIM), F32)],
        ),
        compiler_params=_cp("parallel", "parallel"),
        name="cmp",
    )(page_table, *([src4] * (P + 1)), wa, wb, bias, cmp_w2)


def _cmp_to_slc_mask(ncp, nsp, n_cmp, n_slc):
    ci = lax.broadcasted_iota(I32, (ncp, nsp), 0)
    si = lax.broadcasted_iota(I32, (ncp, nsp), 1)
    ov = ((ci * CMP_STRIDE < (si + 1) * SLC_BLOCK) & (ci * CMP_STRIDE + CMP_BLOCK > si * SLC_BLOCK)
          & (ci < n_cmp) & (si < n_slc))
    return jnp.where(ov, 1.0, 0.0).astype(BF16)


def _nsa_prefill_kernel(q_ref, cos_ref, sin_ref, kc_ref, vc_ref, ks_ref, vs_ref, kw_ref, vw_ref, gt_ref, prev_ref,
                        o_ref, *, tq, T, n_cmp, n_slc, wlen, i0, kext):
    del prev_ref
    i = i0 + pl.program_id(2)
    t0 = i * tq
    scale = HEAD_DIM ** -0.5
    rep = NSA_REP
    q4 = q_ref[...]
    Q = jnp.concatenate([q4[:, r * HEAD_DIM:(r + 1) * HEAD_DIM] for r in range(rep)], axis=0)
    cos = jnp.concatenate([cos_ref[...]] * rep, axis=0)
    sin = jnp.concatenate([sin_ref[...]] * rep, axis=0)
    Qb = (Q * scale).astype(BF16)
    Qrb = (_rope(Q, cos, sin) * scale).astype(BF16)
    qpos1 = t0 + lax.broadcasted_iota(I32, (tq, 1), 0)
    qpos = jnp.concatenate([qpos1] * rep, axis=0)
    ncp = kc_ref.shape[2]
    kc = kc_ref[0, 0].astype(BF16)
    vc = vc_ref[0, 0].astype(BF16)
    lg = _dot_nt(Qb, kc)
    cidx = lax.broadcasted_iota(I32, (1, ncp), 1)
    cmask = (cidx * CMP_STRIDE + CMP_BLOCK - 1 <= qpos) & (cidx < n_cmp)
    p = _masked_softmax(lg, cmask)
    o_cmp = _dot(p.astype(BF16), vc)
    psum = p[0:tq]
    for r in range(1, rep):
        psum = psum + p[r * tq:(r + 1) * tq]
    nsp = 128
    imp = _split3_dot(psum, _cmp_to_slc_mask(ncp, nsp, n_cmp, n_slc))
    blk = lax.broadcasted_iota(I32, (1, nsp), 1)
    forced = (blk == 0) | (blk == jnp.right_shift(qpos1, SLC_SHIFT))
    future = blk * SLC_BLOCK > qpos1
    imp = jnp.where(forced, FORCE_SCORE, jnp.where(future, NEG_INF, imp))
    imp = jnp.where(blk < n_slc, imp, NEG_INF)
    rank = jnp.zeros((tq, nsp), F32)
    for s2 in range(n_slc):
        col = imp[:, s2:s2 + 1]
        beats = (col > imp) | ((col == imp) & (blk > s2))
        rank = rank + jnp.where(beats, 1.0, 0.0)
    sel = (rank < float(min(TOP_BLOCKS, n_slc))) & (blk < n_slc)
    srow = lax.broadcasted_iota(I32, (nsp, kext), 0)
    kcol = lax.broadcasted_iota(I32, (nsp, kext), 1)
    expand = jnp.where(jnp.right_shift(kcol, SLC_SHIFT) == srow, 1.0, 0.0).astype(BF16)
    selk1 = _dot(jnp.where(sel, 1.0, 0.0).astype(BF16), expand)
    selk = jnp.concatenate([selk1] * rep, axis=0)
    kpos = lax.broadcasted_iota(I32, (1, kext), 1)
    smask = (selk > 0.5) & (kpos <= qpos)
    o_s = _masked_attend(_dot_nt(Qrb, ks_ref[0:kext, :].astype(BF16)), smask, vs_ref[0:kext, :].astype(BF16))
    start = pl.multiple_of(jnp.clip(t0 - WINDOW, 0, T - wlen), 128)
    kw = kw_ref[pl.ds(start, wlen), :].astype(BF16)
    vw = vw_ref[pl.ds(start, wlen), :].astype(BF16)
    dpos = qpos - (start + lax.broadcasted_iota(I32, (1, wlen), 1))
    o_w = _masked_attend(_dot_nt(Qrb, kw), (dpos >= 0) & (dpos <= WINDOW), vw)
    gt = _sigmoid(gt_ref[0])
    for r in range(rep):
        rows = slice(r * tq, (r + 1) * tq)
        o = (gt[:, 3 * r:3 * r + 1] * o_cmp[rows] + gt[:, 3 * r + 1:3 * r + 2] * o_s[rows]
             + gt[:, 3 * r + 2:3 * r + 3] * o_w[rows])
        o_ref[:, r * HEAD_DIM:(r + 1) * HEAD_DIM] = o.astype(o_ref.dtype)


def _nsa_prefill(zq, new_kv, win, kvc, gates4, cos, sin, B, T, tq=256):
    tq = min(tq, T)
    nt = T // tq
    n_cmp = (T - CMP_BLOCK) // CMP_STRIDE + 1
    n_slc = -(-T // SLC_BLOCK)
    assert n_slc <= 128 and T % tq == 0
    wlen = min(T, WINDOW + tq)
    ncp = kvc.shape[2]
    gw = NSA_KV
    span = math.gcd(T, PREFILL_SPAN) // tq
    out = jnp.zeros((B * T, NSA_HEADS * HEAD_DIM), BF16)
    for i0 in range(0, nt, span):
        kern = functools.partial(_nsa_prefill_kernel, tq=tq, T=T, n_cmp=n_cmp, n_slc=n_slc, wlen=wlen,
                                 i0=i0, kext=(i0 + span) * tq)
        qrow = lambda b, g, i, i0=i0: (b * nt + i0 + i, g)
        out = pl.pallas_call(
            kern,
            out_shape=jax.ShapeDtypeStruct((B * T, NSA_HEADS * HEAD_DIM), BF16),
            grid=(B, NSA_KV, span),
            in_specs=[pl.BlockSpec((tq, NSA_REP * HEAD_DIM), qrow),
                      pl.BlockSpec((tq, HEAD_DIM), lambda b, g, i, i0=i0: (i0 + i, 0)),
                      pl.BlockSpec((tq, HEAD_DIM), lambda b, g, i, i0=i0: (i0 + i, 0)),
                      pl.BlockSpec((1, 1, ncp, HEAD_DIM), lambda b, g, i: (b, g, 0, 0)),
                      pl.BlockSpec((1, 1, ncp, HEAD_DIM), lambda b, g, i: (b, gw + g, 0, 0)),
                      pl.BlockSpec((T, HEAD_DIM), lambda b, g, i: (b, 2 * gw + g)),
                      pl.BlockSpec((T, HEAD_DIM), lambda b, g, i: (b, 3 * gw + g)),
                      pl.BlockSpec((T, HEAD_DIM), lambda b, g, i: (b, g)),
                      pl.BlockSpec((T, HEAD_DIM), lambda b, g, i: (b, gw + g)),
                      pl.BlockSpec((1, tq, 128), lambda b, g, i, i0=i0: (g, b * nt + i0 + i, 0)),
                      pl.BlockSpec(memory_space=pl.ANY)],
            out_specs=pl.BlockSpec((tq, NSA_REP * HEAD_DIM), qrow),
            input_output_aliases={10: 0},
            compiler_params=_cp("parallel", "parallel", "parallel"),
            name="nsa_prefill",
        )(zq, cos, sin, kvc, kvc, new_kv, new_kv, win, win, gates4, out)
    return out


def _nsa_dec_sel_kernel(q_ref, kc_ref, vc_ref, ocmp_ref, sel_ref, *, tp, n_cmp, n_slc, q_start):
    scale = HEAD_DIM ** -0.5
    rep = NSA_REP
    ncp = kc_ref.shape[2]
    nsp = -(-n_slc // 128) * 128
    qpos1 = q_start + lax.broadcasted_iota(I32, (tp, 1), 0)
    qpos = jnp.concatenate([qpos1] * rep, axis=0)
    cidx = lax.broadcasted_iota(I32, (1, ncp), 1)
    cmask = (cidx * CMP_STRIDE + CMP_BLOCK - 1 <= qpos) & (cidx < n_cmp)
    mm = _cmp_to_slc_mask(ncp, nsp, n_cmp, n_slc)
    blk = lax.broadcasted_iota(I32, (1, nsp), 1)
    lane16 = lax.broadcasted_iota(I32, (tp, 128), 1)
    for g in range(NSA_KV):
        c0 = g * rep * HEAD_DIM
        Q = jnp.concatenate([q_ref[0, :, c0 + r * HEAD_DIM:c0 + (r + 1) * HEAD_DIM] for r in range(rep)], axis=0)
        lg = _dot_nt(Q.astype(BF16), kc_ref[0, g].astype(BF16)) * scale
        p = _masked_softmax(lg, cmask)
        ocmp_ref[0, g] = _dot(p.astype(BF16), vc_ref[0, g].astype(BF16))
        psum = p[0:tp]
        for r in range(1, rep):
            psum = psum + p[r * tp:(r + 1) * tp]
        imp = _split3_dot(psum, mm)
        forced = (blk == 0) | (blk == jnp.right_shift(qpos1, SLC_SHIFT))
        future = blk * SLC_BLOCK > qpos1
        imp = jnp.where(forced, FORCE_SCORE, jnp.where(future, NEG_INF, imp))
        taken = jnp.broadcast_to(blk >= n_slc, (tp, nsp))
        sel = jnp.zeros((tp, 128), I32)
        for it in range(min(TOP_BLOCKS, n_slc)):
            cand = jnp.where(taken, NEG_INF, imp)
            mx = jnp.max(cand, -1, keepdims=True)
            hit = (cand == mx) & jnp.logical_not(taken)
            idx = jnp.min(jnp.where(hit, blk, nsp), -1, keepdims=True)
            taken = taken | (blk == idx)
            sel = jnp.where(lane16 == it, idx, sel)
        sel_ref[0, g] = sel


def _nsa_dec_sel(zq3, kvc, n_cmp, n_slc, q_start):
    B, tp, _ = zq3.shape
    ncp = kvc.shape[2]
    kern = functools.partial(_nsa_dec_sel_kernel, tp=tp, n_cmp=n_cmp, n_slc=n_slc, q_start=q_start)
    return pl.pallas_call(
        kern,
        out_shape=(jax.ShapeDtypeStruct((B, NSA_KV, NSA_REP * tp, HEAD_DIM), F32),
                   jax.ShapeDtypeStruct((B, NSA_KV, tp, 128), I32)),
        grid=(B,),
        in_specs=[pl.BlockSpec((1, tp, NSA_HEADS * HEAD_DIM), lambda b: (b, 0, 0)),
                  pl.BlockSpec((1, NSA_KV, ncp, HEAD_DIM), lambda b: (b, 0, 0, 0)),
                  pl.BlockSpec((1, NSA_KV, ncp, HEAD_DIM), lambda b: (b, 1, 0, 0))],
        out_specs=(pl.BlockSpec((1, NSA_KV, NSA_REP * tp, HEAD_DIM), lambda b: (b, 0, 0, 0)),
                   pl.BlockSpec((1, NSA_KV, tp, 128), lambda b: (b, 0, 0, 0))),
        compiler_params=_cp("parallel"),
        name="nsa_dec_sel",
    )(zq3, kvc, kvc)


def _nsa_dec_attn_kernel(pt_ref, sel_ref, q_ref, cos_ref, sin_ref, *refs, tp, tv, n_past_blk, q_start, ktop):
    nb = NSA_KV * tv
    blk_refs = refs[:nb]
    nkv_ref, wst_ref, win_ref, ocmp_ref, gt_ref, o_ref, qr_s, m_s, l_s, acc_s = refs[nb:]
    b, k = pl.program_id(0), pl.program_id(1)
    rep = NSA_REP
    scale = HEAD_DIM ** -0.5
    rows = rep * tp
    gw = NSA_KV * HEAD_DIM
    trow = lax.broadcasted_iota(I32, (rows, 1), 0) & (tp - 1)
    qpos = q_start + trow

    @pl.when(k == 0)
    def _():
        cos = jnp.concatenate([cos_ref[...]] * rep, axis=0)
        sin = jnp.concatenate([sin_ref[...]] * rep, axis=0)
        for g in range(NSA_KV):
            c0 = g * rep * HEAD_DIM
            Q = jnp.concatenate([q_ref[0, :, c0 + r * HEAD_DIM:c0 + (r + 1) * HEAD_DIM] for r in range(rep)], axis=0)
            qr_s[g] = _rope(Q, cos, sin)
        m_s[...] = jnp.full_like(m_s, NEG_INF)
        l_s[...] = jnp.zeros_like(l_s)
        acc_s[...] = jnp.zeros_like(acc_s)

    def online_update(g, lg, mask, v):
        lg = jnp.where(mask, lg, NEG_INF)
        m_old = m_s[g]
        m_new = jnp.maximum(m_old, jnp.max(lg, -1, keepdims=True))
        m_safe = jnp.where(m_new > NEG_INF, m_new, 0.0)
        a = jnp.exp(m_old - m_safe)
        e = jnp.exp(lg - m_safe)
        l_s[g] = a * l_s[g] + jnp.sum(e, -1, keepdims=True)
        acc_s[g] = a * acc_s[g] + _dot(e.astype(BF16), v)
        m_s[g] = m_new

    col = lax.broadcasted_iota(I32, (1, tv * SLC_BLOCK), 1)
    cblk = jnp.right_shift(col, SLC_SHIFT)
    far = 1 << 24
    for g in range(NSA_KV):
        Qrb = qr_s[g].astype(BF16)
        kb = jnp.concatenate([blk_refs[g * tv + t][:, g, :].astype(BF16) for t in range(tv)], axis=0)
        vb = jnp.concatenate([blk_refs[g * tv + t][:, NSA_KV + g, :].astype(BF16) for t in range(tv)], axis=0)
        lg = _dot_nt(Qrb, kb) * scale
        mask = None
        for t in range(tv):
            s = sel_ref[b, g, t, k]
            s = jnp.where(s < n_past_blk, s, far)
            kpos = s * SLC_BLOCK + (col - t * SLC_BLOCK)
            mt = (cblk == t) & (trow == t) & (kpos <= qpos)
            mask = mt if mask is None else (mask | mt)
        online_update(g, lg, mask, vb)

    @pl.when(k == ktop - 1)
    def _():
        jrow = lax.broadcasted_iota(I32, (1, tp), 1)
        P = wst_ref.shape[0]
        jw = lax.broadcasted_iota(I32, (1, P), 1)
        dpos_w = (qpos - (q_start - P)) - jw
        mask_w = (dpos_w >= 0) & (dpos_w <= WINDOW) & (q_start - P + jw >= 0)
        mask_n = (jrow <= trow) & (jrow < tv)
        for g in range(NSA_KV):
            Qrb = qr_s[g].astype(BF16)
            has_new = None
            for t in range(tv):
                f = sel_ref[b, g, t, 0] == n_past_blk
                for kk in range(1, ktop):
                    f = f | (sel_ref[b, g, t, kk] == n_past_blk)
                hn = trow == jnp.where(f, t, -1)
                has_new = hn if has_new is None else (has_new | hn)
            c_k = 2 * gw + g * HEAD_DIM
            c_v = 3 * gw + g * HEAD_DIM
            nk = nkv_ref[0, :, c_k:c_k + HEAD_DIM].astype(BF16)
            nv = nkv_ref[0, :, c_v:c_v + HEAD_DIM].astype(BF16)
            online_update(g, _dot_nt(Qrb, nk) * scale, has_new & mask_n, nv)
            o_s = acc_s[g] / jnp.maximum(l_s[g], 1e-30)
            wk = wst_ref[:, g, :].astype(BF16)
            wv = wst_ref[:, NSA_KV + g, :].astype(BF16)
            nwk = win_ref[0, :, g * HEAD_DIM:(g + 1) * HEAD_DIM].astype(BF16)
            nwv = win_ref[0, :, gw + g * HEAD_DIM:gw + (g + 1) * HEAD_DIM].astype(BF16)
            lw = jnp.where(mask_w, _dot_nt(Qrb, wk) * scale, NEG_INF)
            ln = jnp.where(mask_n, _dot_nt(Qrb, nwk) * scale, NEG_INF)
            mx = jnp.maximum(jnp.max(lw, -1, keepdims=True), jnp.max(ln, -1, keepdims=True))
            mx = jnp.where(mx > NEG_INF, mx, 0.0)
            ew, en = jnp.exp(lw - mx), jnp.exp(ln - mx)
            den = jnp.maximum(jnp.sum(ew, -1, keepdims=True) + jnp.sum(en, -1, keepdims=True), 1e-30)
            o_w = (_dot(ew.astype(BF16), wv) + _dot(en.astype(BF16), nwv)) / den
            gt = _sigmoid(gt_ref[g])
            o_c = ocmp_ref[0, g]
            for r in range(rep):
                rs = slice(r * tp, (r + 1) * tp)
                o = (gt[:, 3 * r:3 * r + 1] * o_c[rs] + gt[:, 3 * r + 1:3 * r + 2] * o_s[rs]
                     + gt[:, 3 * r + 2:3 * r + 3] * o_w[rs])
                c_o = (g * rep + r) * HEAD_DIM
                o_ref[0, :, c_o:c_o + HEAD_DIM] = o.astype(o_ref.dtype)


def _nsa_dec_attn(page_table, sel, zq3, cos, sin, cache4, page_off, new_kv3, wst3, win_off, win3, ocmp, gates4,
                  tv, q_start):
    B, tp, _ = zq3.shape
    bpp = PAGE_SIZE // SLC_BLOCK
    n_past_blk = page_table.shape[1] * bpp
    ktop = sel.shape[-1]
    P = wst3.shape[0] // (win_off[1])
    rows = NSA_REP * tp

    def cache_spec(g, t):
        def imap(b, k, pt, sl):
            s = jnp.minimum(sl[b, g, t, k], n_past_blk - 1)
            return ((page_off + pt[b, s // bpp]) * bpp + s % bpp, 1, 0, 0)
        return pl.BlockSpec((SLC_BLOCK, None, 2 * NSA_KV, HEAD_DIM), imap)

    cache_specs = [cache_spec(g, t) for g in range(NSA_KV) for t in range(tv)]
    kern = functools.partial(_nsa_dec_attn_kernel, tp=tp, tv=tv, n_past_blk=n_past_blk, q_start=q_start, ktop=ktop)
    whole = lambda w: pl.BlockSpec((1, tp, w), lambda b, k, pt, sl: (b, 0, 0))
    return pl.pallas_call(
        kern,
        out_shape=jax.ShapeDtypeStruct((B, tp, NSA_HEADS * HEAD_DIM), BF16),
        grid_spec=pltpu.PrefetchScalarGridSpec(
            num_scalar_prefetch=2,
            grid=(B, ktop),
            in_specs=[whole(NSA_HEADS * HEAD_DIM),
                      pl.BlockSpec((tp, HEAD_DIM), lambda b, k, pt, sl: (0, 0)),
                      pl.BlockSpec((tp, HEAD_DIM), lambda b, k, pt, sl: (0, 0))]
            + cache_specs
            + [whole(4 * NSA_KV * HEAD_DIM),
               pl.BlockSpec((P, 2 * NSA_KV, HEAD_DIM), lambda b, k, pt, sl: (win_off[0] + b, 0, 0)),
               whole(2 * NSA_KV * HEAD_DIM),
               pl.BlockSpec((1, NSA_KV, rows, HEAD_DIM), lambda b, k, pt, sl: (b, 0, 0, 0)),
               pl.BlockSpec((NSA_KV, tp, 128), lambda b, k, pt, sl: (0, b, 0))],
            out_specs=whole(NSA_HEADS * HEAD_DIM),
            scratch_shapes=[pltpu.VMEM((NSA_KV, rows, HEAD_DIM), F32), pltpu.VMEM((NSA_KV, rows, 1), F32),
                            pltpu.VMEM((NSA_KV, rows, 1), F32), pltpu.VMEM((NSA_KV, rows, HEAD_DIM), F32)],
        ),
        compiler_params=_cp("parallel", "arbitrary"),
        name="nsa_dec_attn",
    )(page_table, sel, zq3, cos, sin, *([cache4] * len(cache_specs)), new_kv3, wst3, win3, ocmp, gates4)


def _gmlp_kernel(z_ref, g_ref, b_ref, ws_ref, bst_ref, o_ref, v_ref, *, lc):
    z = _gelu(z_ref[...])
    u, v = z[:, :W_B], z[:, W_B:]
    mu = jnp.mean(v, -1, keepdims=True)
    d = v - mu
    var = jnp.mean(d * d, -1, keepdims=True)
    vn = d * lax.rsqrt(var + 1e-5) * g_ref[...] + b_ref[...]
    v_ref[...] = vn
    vb = vn.astype(BF16)
    ti = lax.broadcasted_iota(I32, (lc, lc), 0)
    si = lax.broadcasted_iota(I32, (lc, lc), 1)
    for g in range(GMLP_GROUPS):
        cs = slice(g * GMLP_GDIM, (g + 1) * GMLP_GDIM)
        w = jnp.where(si <= ti, ws_ref[g], 0.0).astype(BF16)
        mixed = _dot(w, vb[:, cs]) + bst_ref[:, g:g + 1]
        o_ref[:, cs] = (u[:, cs] * mixed).astype(o_ref.dtype)


def _gmlp(zuv, ln_g, ln_b, ws, bs, lc):
    M = zuv.shape[0]
    ws = ws[:, :lc, :lc]
    bst = bs[:, :lc].T
    return pl.pallas_call(
        functools.partial(_gmlp_kernel, lc=lc),
        out_shape=(jax.ShapeDtypeStruct((M, W_B), BF16), jax.ShapeDtypeStruct((M, W_B), F32)),
        grid=(M // lc,),
        in_specs=[pl.BlockSpec((lc, 2 * W_B), lambda i: (i, 0)),
                  pl.BlockSpec((1, W_B), lambda i: (0, 0)),
                  pl.BlockSpec((1, W_B), lambda i: (0, 0)),
                  pl.BlockSpec((GMLP_GROUPS, lc, lc), lambda i: (0, 0, 0)),
                  pl.BlockSpec((lc, GMLP_GROUPS), lambda i: (0, 0))],
        out_specs=(pl.BlockSpec((lc, W_B), lambda i: (i, 0)), pl.BlockSpec((lc, W_B), lambda i: (i, 0))),
        compiler_params=_cp("parallel"),
        name="gmlp",
    )(zuv, ln_g.reshape(1, W_B), ln_b.reshape(1, W_B), ws, bst)


def _mlstm_kernel(bi_ref, bf_ref, q_ref, k_ref, v_ref, og_ref, sm_ref, smt_ref, ng_ref, c0_ref, n0_ref, m0_ref,
                  h_ref, c_ref, n_ref, m_ref, c_s, n_s, m_s, *, lc, valid, ci_col, cf_col):
    c = pl.program_id(1)

    @pl.when(c == 0)
    def _():
        c_s[...] = c0_ref[0]
        n_s[...] = n0_ref[0]
        m_s[...] = m0_ref[0]

    ti = lax.broadcasted_iota(I32, (lc, lc), 0)
    si = lax.broadcasted_iota(I32, (lc, lc), 1)
    tril = (si <= ti) & (si < valid)
    rvalid = lax.broadcasted_iota(I32, (lc, 1), 0) < valid
    last = valid - 1
    kscale = MLSTM_DQK ** -0.5
    for h in range(MLSTM_HEADS):
        ig_c = sm_ref[:, ci_col + h:ci_col + h + 1] + bi_ref[h]
        lf_c = _log_sigmoid(sm_ref[:, cf_col + h:cf_col + h + 1] + bf_ref[h])
        ig_r = smt_ref[0, h:h + 1, :] + bi_ref[h]
        lf_r = _log_sigmoid(smt_ref[0, MLSTM_HEADS + h:MLSTM_HEADS + h + 1, :] + bf_ref[h])
        F_c = jnp.sum(jnp.where(si <= ti, lf_r, 0.0), axis=1, keepdims=True)
        F_r = jnp.sum(jnp.where(ti <= si, lf_c, 0.0), axis=0, keepdims=True)
        D = jnp.where(tril, F_c - F_r + ig_r, NEG_INF)
        m_prev = m_s[h:h + 1, 0:1]
        m_inter = F_c + m_prev
        m_t = jnp.maximum(m_inter, jnp.max(D, axis=1, keepdims=True))
        qh = q_ref[:, h * MLSTM_DQK:(h + 1) * MLSTM_DQK]
        ks = k_ref[:, h * MLSTM_DQK:(h + 1) * MLSTM_DQK] * kscale
        vh = v_ref[:, h * MLSTM_DV:(h + 1) * MLSTM_DV]
        qb, kb, vb = qh.astype(BF16), ks.astype(BF16), vh.astype(BF16)
        S = _dot_nt(qb, kb) * jnp.exp(D - m_t)
        dec = jnp.exp(m_inter - m_t)
        C = c_s[h]
        n_row = n_s[h:h + 1, :]
        num = _dot(S.astype(BF16), vb) + dec * _dot_nt(qb, C.astype(BF16))
        den = jnp.sum(S, axis=1, keepdims=True) + dec * jnp.sum(qh * n_row, axis=1, keepdims=True)
        hh = num / jnp.maximum(jnp.abs(den), 1.0)
        m_new = m_t[last:last + 1, :]
        F_last = F_c[last:last + 1, :]
        wl_c = jnp.where(rvalid, jnp.exp(F_last - F_c + ig_c - m_new), 0.0)
        dl = jnp.exp(F_last + m_prev - m_new)
        c_s[h] = dl * C + _dot_tn((vh * wl_c).astype(BF16), kb)
        n_s[h:h + 1, :] = dl * n_row + jnp.sum(wl_c * ks, axis=0, keepdims=True)
        m_s[h:h + 1, :] = jnp.broadcast_to(m_new, (1, 128))
        mu = jnp.mean(hh, -1, keepdims=True)
        d = hh - mu
        var = jnp.mean(d * d, -1, keepdims=True)
        hn = d * lax.rsqrt(var + 1e-5) * ng_ref[h:h + 1, :]
        vs = slice(h * MLSTM_DV, (h + 1) * MLSTM_DV)
        h_ref[:, vs] = (_sigmoid(og_ref[:, vs]) * hn).astype(h_ref.dtype)

    @pl.when(c == pl.num_programs(1) - 1)
    def _():
        c_ref[0] = c_s[...]
        n_ref[0] = n_s[...]
        m_ref[0] = m_s[...]


def _mlstm(zcq, zck, zcv, zco, zsm, b_i, b_f, norm_g, C0, n0, m0, B, T, lc, valid, ci_col, cf_col):
    nc = T // lc
    H, DQ, DV = MLSTM_HEADS, MLSTM_DQK, MLSTM_DV
    smt = jnp.concatenate([zsm[:, ci_col:ci_col + H], zsm[:, cf_col:cf_col + H]], -1)
    smt = smt.reshape(B * nc, lc, 2 * H).transpose(0, 2, 1)
    m0b = jnp.broadcast_to(m0[..., None], (B, H, 128))
    row = lambda w: pl.BlockSpec((lc, w), lambda b, c: (b * nc + c, 0))
    smem = pl.BlockSpec(memory_space=pltpu.SMEM)
    kern = functools.partial(_mlstm_kernel, lc=lc, valid=valid, ci_col=ci_col, cf_col=cf_col)
    c_spec = pl.BlockSpec((1, H, DV, DQ), lambda b, c: (b, 0, 0, 0))
    n_spec = pl.BlockSpec((1, H, DQ), lambda b, c: (b, 0, 0))
    h, C, n, m = pl.pallas_call(
        kern,
        out_shape=(jax.ShapeDtypeStruct((B * T, H * DV), BF16), jax.ShapeDtypeStruct((B, H, DV, DQ), F32),
                   jax.ShapeDtypeStruct((B, H, DQ), F32), jax.ShapeDtypeStruct((B, H, 128), F32)),
        grid=(B, nc),
        in_specs=[smem, smem, row(H * DQ), row(H * DQ), row(H * DV), row(H * DV), row(128),
                  pl.BlockSpec((1, 2 * H, lc), lambda b, c: (b * nc + c, 0, 0)),
                  pl.BlockSpec((H, DV), lambda b, c: (0, 0)), c_spec, n_spec, n_spec],
        out_specs=(row(H * DV), c_spec, n_spec, n_spec),
        scratch_shapes=[pltpu.VMEM((H, DV, DQ), F32), pltpu.VMEM((H, DQ), F32), pltpu.VMEM((H, 128), F32)],
        compiler_params=_cp("parallel", "arbitrary"),
        name="mlstm",
    )(b_i, b_f, zcq, zck, zcv, zco, zsm, smt, norm_g, C0, n0, m0b)
    return h, C, n, m[..., 0]


def _merge_kernel(a_ref, b_ref, c_ref, wa_ref, wb_ref, wc_ref, ga_ref, gb_ref, gc_ref, o_ref):
    o = (_sigmoid(ga_ref[...]) * _dot(a_ref[...], wa_ref[...])
         + _sigmoid(gb_ref[...]) * _dot(b_ref[...], wb_ref[...])
         + _sigmoid(gc_ref[...]) * _dot(c_ref[...], wc_ref[...]))
    o_ref[...] = o.astype(o_ref.dtype)


def _merge(oa, ob, oc, wa, wb, wc, layer, zmg, tm=512, tn=512):
    M, K = oa.shape
    D = wa.shape[-1]
    tm, tn = min(tm, M), min(tn, D)
    nj = D // tn
    x_spec = pl.BlockSpec((tm, K), lambda i, j: (i, 0))
    w_spec = _w_spec(wa, layer, (K, tn), lambda i, j: (0, j))
    g_spec = lambda o: pl.BlockSpec((tm, tn), lambda i, j: (i, o * nj + j))
    return pl.pallas_call(
        _merge_kernel,
        out_shape=jax.ShapeDtypeStruct((M, D), BF16),
        grid=(M // tm, nj),
        in_specs=[x_spec, x_spec, x_spec, w_spec, w_spec, w_spec, g_spec(0), g_spec(1), g_spec(2)],
        out_specs=pl.BlockSpec((tm, tn), lambda i, j: (i, j)),
        compiler_params=_cp("parallel", "parallel"),
        name="merge",
    )(oa, ob, oc, wa, wb, wc, zmg, zmg, zmg)


def _top16(s, ids, big, track_pos):
    vals, idxs = [], []
    pos = jnp.full(s.shape, -1, I32) if track_pos else None
    for it in range(PEER_TOPK):
        mx = jnp.max(s, axis=0, keepdims=True)
        idx = jnp.min(jnp.where(s == mx, ids, big), axis=0, keepdims=True)
        one = ids == idx
        s = jnp.where(one, NEG_INF, s)
        if track_pos:
            pos = jnp.where(one, it, pos)
        vals.append(mx)
        idxs.append(idx)
    return vals, idxs, pos


def _peer_route_kernel(q_ref, k1_ref, k2_ref, s1_ref, s2_ref, pk_ref):
    tm = q_ref.shape[0]
    half = PEER_DQ // 2
    K = PEER_TOPK
    k1 = k1_ref[...].astype(BF16)
    k2 = k2_ref[...].astype(BF16)
    kid = lax.broadcasted_iota(I32, (N_KEYS, 1), 0)
    i16 = lax.broadcasted_iota(I32, (K, 1), 0)
    i8 = lax.broadcasted_iota(I32, (8, 1), 0)
    cid = jnp.concatenate([i16, K + i8, 2 * K + i8, 3 * K + i8, i8 * K, i8 * K + 1, i8 * K + 2, (8 + i8) * K], axis=0)
    for h in range(PEER_HEADS):
        q1 = q_ref[:, h * PEER_DQ:h * PEER_DQ + half].astype(BF16)
        q2 = q_ref[:, h * PEER_DQ + half:(h + 1) * PEER_DQ].astype(BF16)
        s1 = _dot_nt(k1, q1)
        s2 = _dot_nt(k2, q2)
        v1, _, pos1 = _top16(s1, kid, N_KEYS, True)
        v2, _, pos2 = _top16(s2, kid, N_KEYS, True)
        v1m = jnp.concatenate(v1, axis=0)
        v2m = jnp.concatenate(v2, axis=0)
        pieces = [v1[0] + v2m, v1[1] + v2m[0:8], v1[2] + v2m[0:8], v1[3] + v2m[0:8]]
        for kk in range(3):
            pieces.append(jnp.where(i8 >= 4, v1m[0:8] + v2[kk], NEG_INF))
        pieces.append(v1m[8:16] + v2[0])
        cv, cj, _ = _top16(jnp.concatenate(pieces, axis=0), cid, K * K, False)
        z = jnp.zeros((1, tm), F32)
        wbits = jnp.zeros((K, tm), I32)
        for it in range(K):
            z = z + jnp.exp(cv[it] - cv[0])
            ii = jnp.right_shift(cj[it], 4)
            kk = cj[it] & (K - 1)
            wbits = jnp.where(i16 == ii, wbits | jnp.left_shift(1, kk), wbits)
        in1, in2 = pos1 >= 0, pos2 >= 0
        s1_ref[h] = jnp.where(in1, (s1 - cv[0]) * LOG2E - jnp.log2(z), NEG_INF)
        s2_ref[h] = jnp.where(in2, s2 * LOG2E, NEG_INF)
        rw = jnp.zeros((N_KEYS, tm), I32)
        for i in range(K):
            rw = jnp.where(pos1 == i, wbits[i:i + 1, :], rw)
        b2 = jnp.where(in2, jnp.left_shift(1, jnp.maximum(pos2, 0)), 0)
        pk_ref[h] = b2 | jnp.left_shift(rw, 16)


def _peer_route(q, k1, k2, tm=256):
    M = q.shape[0]
    tm = min(tm, M)
    spec = pl.BlockSpec((PEER_HEADS, N_KEYS, tm), lambda i: (0, 0, i))
    shp = lambda dt: jax.ShapeDtypeStruct((PEER_HEADS, N_KEYS, M), dt)
    return pl.pallas_call(
        _peer_route_kernel,
        out_shape=(shp(F32), shp(F32), shp(I32)),
        grid=(M // tm,),
        in_specs=[pl.BlockSpec((tm, PEER_HEADS * PEER_DQ), lambda i: (i, 0)),
                  pl.BlockSpec((N_KEYS, PEER_DQ // 2), lambda i: (0, 0)),
                  pl.BlockSpec((N_KEYS, PEER_DQ // 2), lambda i: (0, 0))],
        out_specs=(spec, spec, spec),
        compiler_params=_cp("parallel"),
        name="peer_route",
    )(q, k1, k2)


PEER_SUB = 256


def _peer_dense_kernel(x_ref, u_ref, v_ref, s1_ref, s2_ref, pk_ref, o_ref, *, te):
    e = pl.program_id(1)

    @pl.when(e == 0)
    def _():
        o_ref[...] = jnp.zeros_like(o_ref)

    x = x_ref[...]
    hs = [_dot_nt(x, u_ref[c * PEER_SUB:(c + 1) * PEER_SUB, :]) for c in range(te // PEER_SUB)]
    per = te // N_KEYS
    gates = []
    for al in range(per):
        a = e * per + al
        gate = None
        for hd in range(PEER_HEADS):
            wt = jnp.exp2(s1_ref[hd, pl.ds(a, 1), :] + s2_ref[hd])
            arow = pk_ref[hd, pl.ds(a, 1), :]
            abits = lax.shift_right_logical(arow, jnp.full_like(arow, 16))
            g = jnp.where((abits & pk_ref[hd]) != 0, wt, 0.0)
            gate = g if gate is None else gate + g
        gates.append(gate.T)
    gps = PEER_SUB // N_KEYS
    acts = [(_gelu(hs[c]) * jnp.concatenate(gates[c * gps:(c + 1) * gps], axis=1)).astype(BF16)
            for c in range(te // PEER_SUB)]
    o_ref[...] += _dot(jnp.concatenate(acts, axis=1), v_ref[...])


def _peer_dense(xb, u, v, layer, s1l, s2l, pk, tm=512, te=512):
    M, D = xb.shape
    NE = u.shape[-2]
    tm = min(tm, M)
    n_e = NE // te
    once = pl.Buffered(buffer_count=1)
    rspec = pl.BlockSpec((PEER_HEADS, N_KEYS, tm), lambda i, e: (0, 0, i), pipeline_mode=once)
    return pl.pallas_call(
        functools.partial(_peer_dense_kernel, te=te),
        out_shape=jax.ShapeDtypeStruct((M, D), F32),
        grid=(M // tm, n_e),
        in_specs=[pl.BlockSpec((tm, D), lambda i, e: (i, 0), pipeline_mode=once),
                  _w_spec(u, layer, (te, D), lambda i, e: (e, 0)),
                  _w_spec(v, layer, (te, D), lambda i, e: (e, 0)),
                  rspec, rspec, rspec],
        out_specs=pl.BlockSpec((tm, D), lambda i, e: (i, 0)),
        compiler_params=_cp("parallel", "arbitrary"),
        name="peer_dense",
    )(xb, u, v, s1l, s2l, pk)


def _peer(x1, x1b, lw, alpha):
    M, D = x1.shape
    mp = -(-M // 128) * 128
    xb = x1b if mp == M else jnp.pad(x1b, ((0, mp - M), (0, 0)))
    l = lw["layer"]
    q = _mm(xb, lw["peer_wq"], l)
    s1l, s2l, pk = _peer_route(q, lw["peer_k1"], lw["peer_k2"])
    y = _peer_dense(xb, lw["peer_u"], lw["peer_v"], l, s1l, s2l, pk)[:M]
    return _res_ln(x1, y, lw["ln2_g"], lw["ln2_b"], alpha)


Z_SEGS = ("a_q", "a_kv", "a_g", "b_uv", "c_q", "c_k", "c_v", "c_i", "c_f", "c_o", "mg")
STACKED_BF16 = ("w_br_a", "w_br_b", "w_br_c", "w_out", "peer_wq", "peer_u", "peer_v")
GATE_COL = 0
CI_COL = 3 * NSA_HEADS
CF_COL = CI_COL + MLSTM_HEADS


def _prep_layer_weights(l, w):
    D = w["w_in"].shape[1]
    sizes = (NSA_HEADS * HEAD_DIM, 6 * NSA_KV * HEAD_DIM, 3 * NSA_HEADS, 2 * W_B,
             MLSTM_HEADS * MLSTM_DQK, MLSTM_HEADS * MLSTM_DQK, MLSTM_HEADS * MLSTM_DV, MLSTM_HEADS, MLSTM_HEADS,
             MLSTM_HEADS * MLSTM_DV, 3 * D)
    offs = np.concatenate([[0], np.cumsum(sizes)])
    w_in = w["w_in"][l]
    seg = {n: (int(offs[i]), int(offs[i + 1])) for i, n in enumerate(Z_SEGS)}
    cut = lambda n: w_in[:, seg[n][0]:seg[n][1]].astype(BF16)
    lw = {n: cut(n) for n in ("a_q", "a_kv", "b_uv", "c_q", "c_k", "c_v", "c_o", "mg")}
    small = jnp.concatenate([w_in[:, seg[n][0]:seg[n][1]] for n in ("a_g", "c_i", "c_f")], -1)
    lw["small"] = jnp.pad(small, ((0, 0), (0, 128 - small.shape[1]))).astype(BF16)
    lw["layer"] = l
    for n in STACKED_BF16:
        lw[n] = w[n + "_bf16"]
    for n in ("cmp_pe", "cmp_w1", "cmp_b1", "cmp_w2", "gmlp_ln_g", "gmlp_ln_b", "gmlp_ws", "gmlp_bs",
              "mlstm_b_i", "mlstm_b_f", "mlstm_norm_g", "ln1_g", "ln1_b", "peer_k1", "peer_k2", "ln2_g", "ln2_b"):
        lw[n] = w[n][l]
    return lw


def _gates_by_group(zsm):
    M = zsm.shape[0]
    gt = zsm[:, GATE_COL:GATE_COL + 3 * NSA_HEADS].reshape(M, NSA_KV, 3 * NSA_REP).transpose(1, 0, 2)
    return jnp.pad(gt, ((0, 0), (0, 0), (0, 128 - 3 * NSA_REP)))


def _layer(x, xb, B, T, tv, lw, alpha, dec):
    M, D = x.shape
    z = {n: _mm(xb, lw[n]) for n in ("a_q", "a_kv", "b_uv", "c_q", "c_k", "c_v", "c_o", "mg", "small")}
    zsm = z["small"]
    q_start = 0 if dec is None else dec["q_start"]
    cos, sin = _rope_tables(q_start + jnp.arange(T))
    new_kv, win = _kvprep(z["a_kv"], cos, sin, T)
    gates4 = _gates_by_group(zsm)
    cmp_w = (lw["cmp_pe"], lw["cmp_w1"], lw["cmp_b1"], lw["cmp_w2"])
    gw = NSA_KV * HEAD_DIM
    if dec is None:
        n_pages = T // PAGE_SIZE
        ident = jnp.arange(B * n_pages, dtype=I32).reshape(B, n_pages)
        src3 = z["a_kv"][:, :2 * gw].reshape(M, 2 * NSA_KV, HEAD_DIM)
        kvc = _compress(src3, ident, 0, *cmp_w)
        o_a = _nsa_prefill(z["a_q"], new_kv, win, kvc, gates4, cos, sin, B, T)
        lc_g, lc_m = CHUNK, math.gcd(T, MLSTM_KERNEL_CHUNK)
        C0 = jnp.zeros((B, MLSTM_HEADS, MLSTM_DV, MLSTM_DQK), F32)
        n0 = jnp.zeros((B, MLSTM_HEADS, MLSTM_DQK), F32)
        m0 = jnp.zeros((B, MLSTM_HEADS), F32)
        valid = lc_m
    else:
        page_table = dec["page_table"]
        kvc = _compress(dec["cache3"], page_table, dec["page_off"], *cmp_w)
        L = q_start + tv
        n_cmp = (L - CMP_BLOCK) // CMP_STRIDE + 1
        n_slc = -(-L // SLC_BLOCK)
        assert n_cmp <= q_start // CMP_STRIDE - 1 and n_slc == q_start // SLC_BLOCK + 1
        assert q_start % SLC_BLOCK == 0 and tv <= T
        zq3 = z["a_q"].reshape(B, T, -1)
        ocmp, sel = _nsa_dec_sel(zq3, kvc, n_cmp, n_slc, q_start)
        sel = sel[:, :, :tv, :min(TOP_BLOCKS, n_slc)]
        o_a = _nsa_dec_attn(page_table, sel, zq3, cos, sin, dec["cache4"], dec["page_off"], new_kv.reshape(B, T, -1),
                            dec["wst3"], dec["win_off"], win.reshape(B, T, -1), ocmp, gates4, tv, q_start)
        o_a = o_a.reshape(M, -1)
        lc_g = lc_m = T
        C0, n0, m0 = dec["C"], dec["n"], dec["m"]
        valid = tv
    o_b, v_rows = _gmlp(z["b_uv"], lw["gmlp_ln_g"], lw["gmlp_ln_b"], lw["gmlp_ws"], lw["gmlp_bs"], lc_g)
    o_c, C, n, m = _mlstm(z["c_q"], z["c_k"], z["c_v"], z["c_o"], zsm, lw["mlstm_b_i"], lw["mlstm_b_f"],
                          lw["mlstm_norm_g"], C0, n0, m0, B, T, lc_m, valid, CI_COL, CF_COL)
    merged = _merge(o_a, o_b, o_c, lw["w_br_a"], lw["w_br_b"], lw["w_br_c"], lw["layer"], z["mg"])
    x1, x1b = _res_ln(x, _mm(merged, lw["w_out"], lw["layer"]), lw["ln1_g"], lw["ln1_b"], alpha)
    y, yb = _peer(x1, x1b, lw, alpha)
    return y, yb, new_kv, win, v_rows, C, n, m


def kernel(x_prompt, x_sample, cache_nsa_kv, state_nsa_win, state_mlstm_C, state_mlstm_n, state_mlstm_m,
           page_table, w_in, cmp_pe, cmp_w1, cmp_b1, cmp_w2, gmlp_ln_g, gmlp_ln_b, gmlp_ws, gmlp_bs,
           mlstm_b_i, mlstm_b_f, mlstm_norm_g, w_br_a, w_br_b, w_br_c, w_out, ln1_g, ln1_b,
           peer_wq, peer_k1, peer_k2, peer_u, peer_v, ln2_g, ln2_b):
    w = dict(w_in=w_in, cmp_pe=cmp_pe, cmp_w1=cmp_w1, cmp_b1=cmp_b1, cmp_w2=cmp_w2, gmlp_ln_g=gmlp_ln_g,
             gmlp_ln_b=gmlp_ln_b, gmlp_ws=gmlp_ws, gmlp_bs=gmlp_bs, mlstm_b_i=mlstm_b_i, mlstm_b_f=mlstm_b_f,
             mlstm_norm_g=mlstm_norm_g, w_br_a=w_br_a, w_br_b=w_br_b, w_br_c=w_br_c, w_out=w_out, ln1_g=ln1_g,
             ln1_b=ln1_b, peer_wq=peer_wq, peer_k1=peer_k1, peer_k2=peer_k2, peer_u=peer_u, peer_v=peer_v,
             ln2_g=ln2_g, ln2_b=ln2_b)
    for n in STACKED_BF16:
        w[n + "_bf16"] = w[n].astype(BF16)
    depth = w_in.shape[0]
    alpha = (2 * depth) ** 0.25
    Bp, Tp, D = x_prompt.shape
    Bs, Ts, _ = x_sample.shape
    Tpad = 8
    past_len = page_table.shape[1] * PAGE_SIZE
    n_pool = cache_nsa_kv.shape[1]
    P = state_nsa_win.shape[2]
    cache3 = cache_nsa_kv.reshape(depth * n_pool * PAGE_SIZE, 4 * NSA_KV, HEAD_DIM)
    cache4 = cache_nsa_kv.reshape(depth * n_pool * PAGE_SIZE, 2, 2 * NSA_KV, HEAD_DIM)
    wst3 = state_nsa_win.reshape(depth * Bs * P, 2 * NSA_KV, HEAD_DIM)
    yp = x_prompt.reshape(Bp * Tp, D)
    ys = jnp.pad(x_sample, ((0, 0), (0, Tpad - Ts), (0, 0))).reshape(Bs * Tpad, D)
    ypb, ysb = yp.astype(BF16), ys.astype(BF16)
    outs = [[] for _ in range(11)]
    wkeep_p = min(WINDOW, Tp)
    for l in range(depth):
        lw = _prep_layer_weights(l, w)
        yp, ypb, kv, win, _, C, n, m = _layer(yp, ypb, Bp, Tp, Tp, lw, alpha, None)
        outs[0].append(kv.reshape(Bp, Tp, 4, NSA_KV, HEAD_DIM))
        outs[2].append(win.reshape(Bp, Tp, 2, NSA_KV, HEAD_DIM)[:, Tp - wkeep_p:])
        outs[5].append(C); outs[6].append(n); outs[7].append(m)
        dec = dict(q_start=past_len, cache3=cache3, cache4=cache4, page_off=l * n_pool, page_table=page_table,
                   wst3=wst3, win_off=(l * Bs, depth * Bs), C=state_mlstm_C[l], n=state_mlstm_n[l], m=state_mlstm_m[l])
        ys, ysb, kv, win, v_rows, C, n, m = _layer(ys, ysb, Bs, Tpad, Ts, lw, alpha, dec)
        outs[1].append(kv.reshape(Bs, Tpad, 4, NSA_KV, HEAD_DIM)[:, :Ts])
        win_new = win.reshape(Bs, Tpad, 2, NSA_KV, HEAD_DIM)[:, :Ts]
        outs[3].append(jnp.concatenate([state_nsa_win[l], win_new], 1)[:, Ts:])
        outs[4].append(v_rows.reshape(Bs, Tpad, W_B)[:, :Ts])
        outs[8].append(C); outs[9].append(n); outs[10].append(m)
    y_prompt = yp.reshape(Bp, Tp, D)
    y_sample = ys.reshape(Bs, Tpad, D)[:, :Ts]
    st = [jnp.stack(o) for o in outs]
    return (y_prompt, y_sample, st[0], st[1], st[2], st[3], st[4], st[5], st[6], st[7], st[8], st[9], st[10])
```

```python
import functools
import math

import numpy as np
import jax
import jax.numpy as jnp
from jax import lax
from jax.experimental import pallas as pl
from jax.experimental.pallas import tpu as pltpu

F32 = jnp.float32
BF16 = jnp.bfloat16
I32 = jnp.int32

HEAD_DIM = 128
ROT_DIM = HEAD_DIM // 4
ROPE_THETA = 500000.0
NSA_HEADS = 16
NSA_KV = 4
NSA_REP = NSA_HEADS // NSA_KV
CMP_BLOCK = 32
CMP_STRIDE = 16
SLC_BLOCK = 64
SLC_SHIFT = 6
TOP_BLOCKS = 16
WINDOW = 512
FORCE_SCORE = 1.0e9
CHUNK = 128
GMLP_GROUPS = 16
GMLP_GDIM = 128
W_B = GMLP_GROUPS * GMLP_GDIM
MLSTM_HEADS = 8
MLSTM_DQK = 128
MLSTM_DV = 256
MLSTM_KERNEL_CHUNK = 256
PEER_HEADS = 8
PEER_DQ = 256
N_KEYS = 128
PEER_TOPK = 16
PAGE_SIZE = 128
LOG2E = 1.4426950408889634
NEG_INF = float("-inf")

VMEM_LIMIT_BYTES = 56 * 1024 * 1024


def _cp(*sem):
    return pltpu.CompilerParams(dimension_semantics=sem, vmem_limit_bytes=VMEM_LIMIT_BYTES)


def _gelu(x):
    hx = 0.5 * x
    return hx + hx * jnp.tanh(x * (x * x * (0.7978845608028654 * 0.044715) + 0.7978845608028654))


def _sigmoid(x):
    return 1.0 / (1.0 + jnp.exp(-x))


def _log_sigmoid(x):
    return jnp.minimum(x, 0.0) - jnp.log(1.0 + jnp.exp(-jnp.abs(x)))


def _dot(a, b):
    return jnp.dot(a, b, preferred_element_type=F32)


def _dot_nt(a, b):
    return lax.dot_general(a, b, (((1,), (1,)), ((), ())), preferred_element_type=F32)


def _dot_tn(a, b):
    return lax.dot_general(a, b, (((0,), (0,)), ((), ())), preferred_element_type=F32)


def _masked_softmax(lg, mask):
    lg = jnp.where(mask, lg, NEG_INF)
    mx = jnp.max(lg, -1, keepdims=True)
    mx = jnp.where(mx > NEG_INF, mx, 0.0)
    e = jnp.exp(lg - mx)
    return e / jnp.maximum(jnp.sum(e, -1, keepdims=True), 1e-30)


def _masked_attend(lg, mask, v):
    lg = jnp.where(mask, lg, NEG_INF)
    mx = jnp.max(lg, -1, keepdims=True)
    mx = jnp.where(mx > NEG_INF, mx, 0.0)
    e = jnp.exp(lg - mx)
    den = jnp.maximum(jnp.sum(e, -1, keepdims=True), 1e-30)
    return _dot(e.astype(BF16), v) / den


def _split3_dot(p, m_bf16):
    hi = p.astype(BF16)
    r1 = p - hi.astype(F32)
    mid = r1.astype(BF16)
    lo = (r1 - mid.astype(F32)).astype(BF16)
    return _dot(hi, m_bf16) + _dot(mid, m_bf16) + _dot(lo, m_bf16)


def _rope(x, cos, sin):
    lane = lax.broadcasted_iota(I32, x.shape, 1)
    up = pltpu.roll(x, HEAD_DIM - ROT_DIM // 2, 1)
    dn = pltpu.roll(x, ROT_DIM // 2, 1)
    return x * cos + jnp.where(lane < ROT_DIM // 2, up, dn) * sin


def _rope_tables(pos):
    inv = ROPE_THETA ** (-jnp.arange(0, ROT_DIM, 2, dtype=F32) / ROT_DIM)
    ang = pos.astype(F32)[:, None] * inv[None, :]
    c, s = jnp.cos(ang), jnp.sin(ang)
    n = pos.shape[0]
    cos = jnp.concatenate([c, c, jnp.ones((n, HEAD_DIM - ROT_DIM), F32)], -1)
    sin = jnp.concatenate([-s, s, jnp.zeros((n, HEAD_DIM - ROT_DIM), F32)], -1)
    return cos, sin


def _mm_kernel(x_ref, w_ref, o_ref, acc_ref):
    k = pl.program_id(2)

    @pl.when(k == 0)
    def _():
        acc_ref[...] = jnp.zeros_like(acc_ref)

    acc_ref[...] += _dot(x_ref[...], w_ref[...])

    @pl.when(k == pl.num_programs(2) - 1)
    def _():
        o_ref[...] = acc_ref[...].astype(o_ref.dtype)


def _w_spec(w, layer, block, imap):
    if w.ndim == 2:
        return pl.BlockSpec(block, imap)
    return pl.BlockSpec((None,) + block, lambda *a: (layer,) + imap(*a))


def _mm(x, w, layer=None, out_dtype=F32, tm=1024, tn=512, tk=4096):
    M, K = x.shape
    N = w.shape[-1]
    tm, tn, tk = min(tm, M), min(tn, N), min(tk, K)
    assert M % tm == 0 and N % tn == 0 and K % tk == 0, (x.shape, w.shape)
    return pl.pallas_call(
        _mm_kernel,
        out_shape=jax.ShapeDtypeStruct((M, N), out_dtype),
        grid=(M // tm, N // tn, K // tk),
        in_specs=[pl.BlockSpec((tm, tk), lambda i, j, k: (i, k)),
                  _w_spec(w, layer, (tk, tn), lambda i, j, k: (k, j))],
        out_specs=pl.BlockSpec((tm, tn), lambda i, j, k: (i, j)),
        scratch_shapes=[pltpu.VMEM((tm, tn), F32)],
        compiler_params=_cp("parallel", "parallel", "arbitrary"),
        name="mm",
    )(x, w)


def _res_ln_kernel(x_ref, y_ref, g_ref, b_ref, o_ref, ob_ref, *, alpha):
    v = alpha * x_ref[...] + y_ref[...]
    mu = jnp.mean(v, -1, keepdims=True)
    d = v - mu
    var = jnp.mean(d * d, -1, keepdims=True)
    o = d * lax.rsqrt(var + 1e-5) * g_ref[...] + b_ref[...]
    o_ref[...] = o
    ob_ref[...] = o.astype(BF16)


def _res_ln(x, y, g, b, alpha, tm=256):
    M, D = x.shape
    tm = min(tm, M)
    row = pl.BlockSpec((tm, D), lambda i: (i, 0))
    vec = pl.BlockSpec((1, D), lambda i: (0, 0))
    return pl.pallas_call(
        functools.partial(_res_ln_kernel, alpha=alpha),
        out_shape=(jax.ShapeDtypeStruct((M, D), F32), jax.ShapeDtypeStruct((M, D), BF16)),
        grid=(M // tm,),
        in_specs=[row, row, vec, vec],
        out_specs=(row, row),
        compiler_params=_cp("parallel"),
        name="res_ln",
    )(x, y, g.reshape(1, D), b.reshape(1, D))


def _kvprep_kernel(z_ref, cos_ref, sin_ref, kv_ref, win_ref):
    cos, sin = cos_ref[...], sin_ref[...]
    gw = NSA_KV * HEAD_DIM
    kv_ref[:, 0:2 * gw] = z_ref[:, 0:2 * gw]
    kv_ref[:, 3 * gw:4 * gw] = z_ref[:, 3 * gw:4 * gw]
    win_ref[:, gw:2 * gw] = z_ref[:, 5 * gw:6 * gw]
    for g in range(NSA_KV):
        a = 2 * gw + g * HEAD_DIM
        kv_ref[:, a:a + HEAD_DIM] = _rope(z_ref[:, a:a + HEAD_DIM], cos, sin)
        a = 4 * gw + g * HEAD_DIM
        win_ref[:, g * HEAD_DIM:(g + 1) * HEAD_DIM] = _rope(z_ref[:, a:a + HEAD_DIM], cos, sin)


def _kvprep(zkv, cos, sin, T, tm=256):
    M = zkv.shape[0]
    tm = min(tm, T)
    nt = T // tm
    gw = NSA_KV * HEAD_DIM
    return pl.pallas_call(
        _kvprep_kernel,
        out_shape=(jax.ShapeDtypeStruct((M, 4 * gw), F32), jax.ShapeDtypeStruct((M, 2 * gw), F32)),
        grid=(M // tm,),
        in_specs=[pl.BlockSpec((tm, 6 * gw), lambda i: (i, 0)),
                  pl.BlockSpec((tm, HEAD_DIM), lambda i: (i % nt, 0)),
                  pl.BlockSpec((tm, HEAD_DIM), lambda i: (i % nt, 0))],
        out_specs=(pl.BlockSpec((tm, 4 * gw), lambda i: (i, 0)),
                   pl.BlockSpec((tm, 2 * gw), lambda i: (i, 0))),
        compiler_params=_cp("parallel"),
        name="kvprep",
    )(zkv, cos, sin)


CMP_PAGES = 8
PREFILL_SPAN = 256


def _cmp_bias_kernel(pe_ref, w1f_ref, b1_ref, o_ref):
    for kv in range(2):
        b = b1_ref[kv:kv + 1, :]
        for half in range(2):
            b = b + _dot(pe_ref[kv, half].astype(BF16), w1f_ref[kv, half])[0:1, :]
        o_ref[:, kv * HEAD_DIM:(kv + 1) * HEAD_DIM] = jnp.broadcast_to(b, (8, HEAD_DIM))


def _cmp_kernel(pt_ref, *refs, P):
    page_refs = refs[:P + 1]
    wa_ref, wb_ref, bias_ref, w2_ref, o_ref, res_s = refs[P + 1:]
    hp = PAGE_SIZE // CMP_STRIDE
    sg = 2 * NSA_KV
    rows = P * hp * sg
    acc_a = jnp.zeros((rows, 2 * HEAD_DIM), F32)
    acc_b = jnp.zeros((rows, 2 * HEAD_DIM), F32)
    for j in range(CMP_STRIDE):
        la, lb = [], []
        for p in range(P):
            la.append(page_refs[p][pl.ds(j, hp, stride=CMP_STRIDE), 0:sg, :].reshape(hp * sg, HEAD_DIM))
            lb.append(page_refs[p][pl.ds(j + CMP_STRIDE, hp - 1, stride=CMP_STRIDE), 0:sg, :]
                      .reshape((hp - 1) * sg, HEAD_DIM))
            lb.append(page_refs[p + 1][j, 0:sg, :])
        acc_a = acc_a + _dot(jnp.concatenate(la, axis=0).astype(BF16), wa_ref[j])
        acc_b = acc_b + _dot(jnp.concatenate(lb, axis=0).astype(BF16), wb_ref[j])
    h = _gelu(acc_a + acc_b + bias_ref[0:1, :])
    ok = _dot(h[:, :HEAD_DIM].astype(BF16), w2_ref[0].astype(BF16))
    ov = _dot(h[:, HEAD_DIM:].astype(BF16), w2_ref[1].astype(BF16))
    is_k = (lax.broadcasted_iota(I32, (rows, 1), 0) & (sg - 1)) < NSA_KV
    res_s[...] = jnp.where(is_k, ok, ov)
    for s in range(sg):
        o_ref[0, s] = res_s[pl.ds(s, P * hp, stride=sg), :]


def _compress(src3, page_table, page_off, cmp_pe, cmp_w1, cmp_b1, cmp_w2):
    B, n_pages = page_table.shape
    P = math.gcd(CMP_PAGES, n_pages)
    hp = PAGE_SIZE // CMP_STRIDE
    sg = 2 * NSA_KV
    src4 = src3.reshape(src3.shape[0], src3.shape[1] // sg, sg, HEAD_DIM)
    pe_flat = jnp.broadcast_to(cmp_pe.reshape(2, 2, 1, CMP_STRIDE * HEAD_DIM), (2, 2, 8, CMP_STRIDE * HEAD_DIM))
    w1_flat = cmp_w1.reshape(2, 2, CMP_STRIDE * HEAD_DIM, HEAD_DIM).astype(BF16)
    bias = pl.pallas_call(
        _cmp_bias_kernel,
        out_shape=jax.ShapeDtypeStruct((8, 2 * HEAD_DIM), F32),
        name="cmp_bias",
    )(pe_flat, w1_flat, cmp_b1)
    wa = jnp.concatenate([cmp_w1[0, :CMP_STRIDE], cmp_w1[1, :CMP_STRIDE]], -1).astype(BF16)
    wb = jnp.concatenate([cmp_w1[0, CMP_STRIDE:], cmp_w1[1, CMP_STRIDE:]], -1).astype(BF16)

    def page_spec(p):
        return pl.BlockSpec((PAGE_SIZE, None, sg, HEAD_DIM),
                            lambda b, c, pt: (page_off + pt[b, jnp.minimum(c * P + p, n_pages - 1)], 0, 0, 0))

    full = lambda shape: pl.BlockSpec(shape, lambda b, c, pt: (0,) * len(shape))
    return pl.pallas_call(
        functools.partial(_cmp_kernel, P=P),
        out_shape=jax.ShapeDtypeStruct((B, sg, n_pages * hp, HEAD_DIM), F32),
        grid_spec=pltpu.PrefetchScalarGridSpec(
            num_scalar_prefetch=1,
            grid=(B, n_pages // P),
            in_specs=[page_spec(p) for p in range(P + 1)]
            + [full(wa.shape), full(wb.shape), full(bias.shape), full(cmp_w2.shape)],
            out_specs=pl.BlockSpec((1, sg, P * hp, HEAD_DIM), lambda b, c, pt: (b, 0, c, 0)),
            scratch_shapes=[pltpu.VMEM((P * hp * sg, HEAD_DIM), F32)],
        ),
        compiler_params=_cp("parallel", "parallel"),
        name="cmp",
    )(page_table, *([src4] * (P + 1)), wa, wb, bias, cmp_w2)


def _cmp_to_slc_mask(ncp, nsp, n_cmp, n_slc):
    ci = lax.broadcasted_iota(I32, (ncp, nsp), 0)
    si = lax.broadcasted_iota(I32, (ncp, nsp), 1)
    ov = ((ci * CMP_STRIDE < (si + 1) * SLC_BLOCK) & (ci * CMP_STRIDE + CMP_BLOCK > si * SLC_BLOCK)
          & (ci < n_cmp) & (si < n_slc))
    return jnp.where(ov, 1.0, 0.0).astype(BF16)


def _nsa_prefill_kernel(q_ref, cos_ref, sin_ref, kc_ref, vc_ref, ks_ref, vs_ref, kw_ref, vw_ref, gt_ref, prev_ref,
                        o_ref, *, tq, T, n_cmp, n_slc, wlen, i0, kext):
    del prev_ref
    i = i0 + pl.program_id(2)
    t0 = i * tq
    scale = HEAD_DIM ** -0.5
    rep = NSA_REP
    q4 = q_ref[...]
    Q = jnp.concatenate([q4[:, r * HEAD_DIM:(r + 1) * HEAD_DIM] for r in range(rep)], axis=0)
    cos = jnp.concatenate([cos_ref[...]] * rep, axis=0)
    sin = jnp.concatenate([sin_ref[...]] * rep, axis=0)
    Qb = (Q * scale).astype(BF16)
    Qrb = (_rope(Q, cos, sin) * scale).astype(BF16)
    qpos1 = t0 + lax.broadcasted_iota(I32, (tq, 1), 0)
    qpos = jnp.concatenate([qpos1] * rep, axis=0)
    ncp = kc_ref.shape[2]
    kc = kc_ref[0, 0].astype(BF16)
    vc = vc_ref[0, 0].astype(BF16)
    lg = _dot_nt(Qb, kc)
    cidx = lax.broadcasted_iota(I32, (1, ncp), 1)
    cmask = (cidx * CMP_STRIDE + CMP_BLOCK - 1 <= qpos) & (cidx < n_cmp)
    p = _masked_softmax(lg, cmask)
    o_cmp = _dot(p.astype(BF16), vc)
    psum = p[0:tq]
    for r in range(1, rep):
        psum = psum + p[r * tq:(r + 1) * tq]
    nsp = 128
    imp = _split3_dot(psum, _cmp_to_slc_mask(ncp, nsp, n_cmp, n_slc))
    blk = lax.broadcasted_iota(I32, (1, nsp), 1)
    forced = (blk == 0) | (blk == jnp.right_shift(qpos1, SLC_SHIFT))
    future = blk * SLC_BLOCK > qpos1
    imp = jnp.where(forced, FORCE_SCORE, jnp.where(future, NEG_INF, imp))
    imp = jnp.where(blk < n_slc, imp, NEG_INF)
    rank = jnp.zeros((tq, nsp), F32)
    for s2 in range(n_slc):
        col = imp[:, s2:s2 + 1]
        beats = (col > imp) | ((col == imp) & (blk > s2))
        rank = rank + jnp.where(beats, 1.0, 0.0)
    sel = (rank < float(min(TOP_BLOCKS, n_slc))) & (blk < n_slc)
    srow = lax.broadcasted_iota(I32, (nsp, kext), 0)
    kcol = lax.broadcasted_iota(I32, (nsp, kext), 1)
    expand = jnp.where(jnp.right_shift(kcol, SLC_SHIFT) == srow, 1.0, 0.0).astype(BF16)
    selk1 = _dot(jnp.where(sel, 1.0, 0.0).astype(BF16), expand)
    selk = jnp.concatenate([selk1] * rep, axis=0)
    kpos = lax.broadcasted_iota(I32, (1, kext), 1)
    smask = (selk > 0.5) & (kpos <= qpos)
    o_s = _masked_attend(_dot_nt(Qrb, ks_ref[0:kext, :].astype(BF16)), smask, vs_ref[0:kext, :].astype(BF16))
    start = pl.multiple_of(jnp.clip(t0 - WINDOW, 0, T - wlen), 128)
    kw = kw_ref[pl.ds(start, wlen), :].astype(BF16)
    vw = vw_ref[pl.ds(start, wlen), :].astype(BF16)
    dpos = qpos - (start + lax.broadcasted_iota(I32, (1, wlen), 1))
    o_w = _masked_attend(_dot_nt(Qrb, kw), (dpos >= 0) & (dpos <= WINDOW), vw)
    gt = _sigmoid(gt_ref[0])
    for r in range(rep):
        rows = slice(r * tq, (r + 1) * tq)
        o = (gt[:, 3 * r:3 * r + 1] * o_cmp[rows] + gt[:, 3 * r + 1:3 * r + 2] * o_s[rows]
             + gt[:, 3 * r + 2:3 * r + 3] * o_w[rows])
        o_ref[:, r * HEAD_DIM:(r + 1) * HEAD_DIM] = o.astype(o_ref.dtype)


def _nsa_prefill(zq, new_kv, win, kvc, gates4, cos, sin, B, T, tq=256):
    tq = min(tq, T)
    nt = T // tq
    n_cmp = (T - CMP_BLOCK) // CMP_STRIDE + 1
    n_slc = -(-T // SLC_BLOCK)
    assert n_slc <= 128 and T % tq == 0
    wlen = min(T, WINDOW + tq)
    ncp = kvc.shape[2]
    gw = NSA_KV
    span = math.gcd(T, PREFILL_SPAN) // tq
    out = jnp.zeros((B * T, NSA_HEADS * HEAD_DIM), BF16)
    for i0 in range(0, nt, span):
        kern = functools.partial(_nsa_prefill_kernel, tq=tq, T=T, n_cmp=n_cmp, n_slc=n_slc, wlen=wlen,
                                 i0=i0, kext=(i0 + span) * tq)
        qrow = lambda b, g, i, i0=i0: (b * nt + i0 + i, g)
        out = pl.pallas_call(
            kern,
            out_shape=jax.ShapeDtypeStruct((B * T, NSA_HEADS * HEAD_DIM), BF16),
            grid=(B, NSA_KV, span),
            in_specs=[pl.BlockSpec((tq, NSA_REP * HEAD_DIM), qrow),
                      pl.BlockSpec((tq, HEAD_DIM), lambda b, g, i, i0=i0: (i0 + i, 0)),
                      pl.BlockSpec((tq, HEAD_DIM), lambda b, g, i, i0=i0: (i0 + i, 0)),
                      pl.BlockSpec((1, 1, ncp, HEAD_DIM), lambda b, g, i: (b, g, 0, 0)),
                      pl.BlockSpec((1, 1, ncp, HEAD_DIM), lambda b, g, i: (b, gw + g, 0, 0)),
                      pl.BlockSpec((T, HEAD_DIM), lambda b, g, i: (b, 2 * gw + g)),
                      pl.BlockSpec((T, HEAD_DIM), lambda b, g, i: (b, 3 * gw + g)),
                      pl.BlockSpec((T, HEAD_DIM), lambda b, g, i: (b, g)),
                      pl.BlockSpec((T, HEAD_DIM), lambda b, g, i: (b, gw + g)),
                      pl.BlockSpec((1, tq, 128), lambda b, g, i, i0=i0: (g, b * nt + i0 + i, 0)),
                      pl.BlockSpec(memory_space=pl.ANY)],
            out_specs=pl.BlockSpec((tq, NSA_REP * HEAD_DIM), qrow),
            input_output_aliases={10: 0},
            compiler_params=_cp("parallel", "parallel", "parallel"),
            name="nsa_prefill",
        )(zq, cos, sin, kvc, kvc, new_kv, new_kv, win, win, gates4, out)
    return out


def _nsa_dec_sel_kernel(q_ref, kc_ref, vc_ref, ocmp_ref, sel_ref, *, tp, n_cmp, n_slc, q_start):
    scale = HEAD_DIM ** -0.5
    rep = NSA_REP
    ncp = kc_ref.shape[2]
    nsp = -(-n_slc // 128) * 128
    qpos1 = q_start + lax.broadcasted_iota(I32, (tp, 1), 0)
    qpos = jnp.concatenate([qpos1] * rep, axis=0)
    cidx = lax.broadcasted_iota(I32, (1, ncp), 1)
    cmask = (cidx * CMP_STRIDE + CMP_BLOCK - 1 <= qpos) & (cidx < n_cmp)
    mm = _cmp_to_slc_mask(ncp, nsp, n_cmp, n_slc)
    blk = lax.broadcasted_iota(I32, (1, nsp), 1)
    lane16 = lax.broadcasted_iota(I32, (tp, 128), 1)
    for g in range(NSA_KV):
        c0 = g * rep * HEAD_DIM
        Q = jnp.concatenate([q_ref[0, :, c0 + r * HEAD_DIM:c0 + (r + 1) * HEAD_DIM] for r in range(rep)], axis=0)
        lg = _dot_nt(Q.astype(BF16), kc_ref[0, g].astype(BF16)) * scale
        p = _masked_softmax(lg, cmask)
        ocmp_ref[0, g] = _dot(p.astype(BF16), vc_ref[0, g].astype(BF16))
        psum = p[0:tp]
        for r in range(1, rep):
            psum = psum + p[r * tp:(r + 1) * tp]
        imp = _split3_dot(psum, mm)
        forced = (blk == 0) | (blk == jnp.right_shift(qpos1, SLC_SHIFT))
        future = blk * SLC_BLOCK > qpos1
        imp = jnp.where(forced, FORCE_SCORE, jnp.where(future, NEG_INF, imp))
        taken = jnp.broadcast_to(blk >= n_slc, (tp, nsp))
        sel = jnp.zeros((tp, 128), I32)
        for it in range(min(TOP_BLOCKS, n_slc)):
            cand = jnp.where(taken, NEG_INF, imp)
            mx = jnp.max(cand, -1, keepdims=True)
            hit = (cand == mx) & jnp.logical_not(taken)
            idx = jnp.min(jnp.where(hit, blk, nsp), -1, keepdims=True)
            taken = taken | (blk == idx)
            sel = jnp.where(lane16 == it, idx, sel)
        sel_ref[0, g] = sel


def _nsa_dec_sel(zq3, kvc, n_cmp, n_slc, q_start):
    B, tp, _ = zq3.shape
    ncp = kvc.shape[2]
    kern = functools.partial(_nsa_dec_sel_kernel, tp=tp, n_cmp=n_cmp, n_slc=n_slc, q_start=q_start)
    return pl.pallas_call(
        kern,
        out_shape=(jax.ShapeDtypeStruct((B, NSA_KV, NSA_REP * tp, HEAD_DIM), F32),
                   jax.ShapeDtypeStruct((B, NSA_KV, tp, 128), I32)),
        grid=(B,),
        in_specs=[pl.BlockSpec((1, tp, NSA_HEADS * HEAD_DIM), lambda b: (b, 0, 0)),
                  pl.BlockSpec((1, NSA_KV, ncp, HEAD_DIM), lambda b: (b, 0, 0, 0)),
                  pl.BlockSpec((1, NSA_KV, ncp, HEAD_DIM), lambda b: (b, 1, 0, 0))],
        out_specs=(pl.BlockSpec((1, NSA_KV, NSA_REP * tp, HEAD_DIM), lambda b: (b, 0, 0, 0)),
                   pl.BlockSpec((1, NSA_KV, tp, 128), lambda b: (b, 0, 0, 0))),
        compiler_params=_cp("parallel"),
        name="nsa_dec_sel",
    )(zq3, kvc, kvc)


def _nsa_dec_attn_kernel(pt_ref, sel_ref, q_ref, cos_ref, sin_ref, *refs, tp, tv, n_past_blk, q_start, ktop):
    nb = NSA_KV * tv
    blk_refs = refs[:nb]
    nkv_ref, wst_ref, win_ref, ocmp_ref, gt_ref, o_ref, qr_s, m_s, l_s, acc_s = refs[nb:]
    b, k = pl.program_id(0), pl.program_id(1)
    rep = NSA_REP
    scale = HEAD_DIM ** -0.5
    rows = rep * tp
    gw = NSA_KV * HEAD_DIM
    trow = lax.broadcasted_iota(I32, (rows, 1), 0) & (tp - 1)
    qpos = q_start + trow

    @pl.when(k == 0)
    def _():
        cos = jnp.concatenate([cos_ref[...]] * rep, axis=0)
        sin = jnp.concatenate([sin_ref[...]] * rep, axis=0)
        for g in range(NSA_KV):
            c0 = g * rep * HEAD_DIM
            Q = jnp.concatenate([q_ref[0, :, c0 + r * HEAD_DIM:c0 + (r + 1) * HEAD_DIM] for r in range(rep)], axis=0)
            qr_s[g] = _rope(Q, cos, sin)
        m_s[...] = jnp.full_like(m_s, NEG_INF)
        l_s[...] = jnp.zeros_like(l_s)
        acc_s[...] = jnp.zeros_like(acc_s)

    def online_update(g, lg, mask, v):
        lg = jnp.where(mask, lg, NEG_INF)
        m_old = m_s[g]
        m_new = jnp.maximum(m_old, jnp.max(lg, -1, keepdims=True))
        m_safe = jnp.where(m_new > NEG_INF, m_new, 0.0)
        a = jnp.exp(m_old - m_safe)
        e = jnp.exp(lg - m_safe)
        l_s[g] = a * l_s[g] + jnp.sum(e, -1, keepdims=True)
        acc_s[g] = a * acc_s[g] + _dot(e.astype(BF16), v)
        m_s[g] = m_new

    col = lax.broadcasted_iota(I32, (1, tv * SLC_BLOCK), 1)
    cblk = jnp.right_shift(col, SLC_SHIFT)
    far = 1 << 24
    for g in range(NSA_KV):
        Qrb = qr_s[g].astype(BF16)
        kb = jnp.concatenate([blk_refs[g * tv + t][:, g, :].astype(BF16) for t in range(tv)], axis=0)
        vb = jnp.concatenate([blk_refs[g * tv + t][:, NSA_KV + g, :].astype(BF16) for t in range(tv)], axis=0)
        lg = _dot_nt(Qrb, kb) * scale
        mask = None
        for t in range(tv):
            s = sel_ref[b, g, t, k]
            s = jnp.where(s < n_past_blk, s, far)
            kpos = s * SLC_BLOCK + (col - t * SLC_BLOCK)
            mt = (cblk == t) & (trow == t) & (kpos <= qpos)
            mask = mt if mask is None else (mask | mt)
        online_update(g, lg, mask, vb)

    @pl.when(k == ktop - 1)
    def _():
        jrow = lax.broadcasted_iota(I32, (1, tp), 1)
        P = wst_ref.shape[0]
        jw = lax.broadcasted_iota(I32, (1, P), 1)
        dpos_w = (qpos - (q_start - P)) - jw
        mask_w = (dpos_w >= 0) & (dpos_w <= WINDOW) & (q_start - P + jw >= 0)
        mask_n = (jrow <= trow) & (jrow < tv)
        for g in range(NSA_KV):
            Qrb = qr_s[g].astype(BF16)
            has_new = None
            for t in range(tv):
                f = sel_ref[b, g, t, 0] == n_past_blk
                for kk in range(1, ktop):
                    f = f | (sel_ref[b, g, t, kk] == n_past_blk)
                hn = trow == jnp.where(f, t, -1)
                has_new = hn if has_new is None else (has_new | hn)
            c_k = 2 * gw + g * HEAD_DIM
            c_v = 3 * gw + g * HEAD_DIM
            nk = nkv_ref[0, :, c_k:c_k + HEAD_DIM].astype(BF16)
            nv = nkv_ref[0, :, c_v:c_v + HEAD_DIM].astype(BF16)
            online_update(g, _dot_nt(Qrb, nk) * scale, has_new & mask_n, nv)
            o_s = acc_s[g] / jnp.maximum(l_s[g], 1e-30)
            wk = wst_ref[:, g, :].astype(BF16)
            wv = wst_ref[:, NSA_KV + g, :].astype(BF16)
            nwk = win_ref[0, :, g * HEAD_DIM:(g + 1) * HEAD_DIM].astype(BF16)
            nwv = win_ref[0, :, gw + g * HEAD_DIM:gw + (g + 1) * HEAD_DIM].astype(BF16)
            lw = jnp.where(mask_w, _dot_nt(Qrb, wk) * scale, NEG_INF)
            ln = jnp.where(mask_n, _dot_nt(Qrb, nwk) * scale, NEG_INF)
            mx = jnp.maximum(jnp.max(lw, -1, keepdims=True), jnp.max(ln, -1, keepdims=True))
            mx = jnp.where(mx > NEG_INF, mx, 0.0)
            ew, en = jnp.exp(lw - mx), jnp.exp(ln - mx)
            den = jnp.maximum(jnp.sum(ew, -1, keepdims=True) + jnp.sum(en, -1, keepdims=True), 1e-30)
            o_w = (_dot(ew.astype(BF16), wv) + _dot(en.astype(BF16), nwv)) / den
            gt = _sigmoid(gt_ref[g])
            o_c = ocmp_ref[0, g]
            for r in range(rep):
                rs = slice(r * tp, (r + 1) * tp)
                o = (gt[:, 3 * r:3 * r + 1] * o_c[rs] + gt[:, 3 * r + 1:3 * r + 2] * o_s[rs]
                     + gt[:, 3 * r + 2:3 * r + 3] * o_w[rs])
                c_o = (g * rep + r) * HEAD_DIM
                o_ref[0, :, c_o:c_o + HEAD_DIM] = o.astype(o_ref.dtype)


def _nsa_dec_attn(page_table, sel, zq3, cos, sin, cache4, page_off, new_kv3, wst3, win_off, win3, ocmp, gates4,
                  tv, q_start):
    B, tp, _ = zq3.shape
    bpp = PAGE_SIZE // SLC_BLOCK
    n_past_blk = page_table.shape[1] * bpp
    ktop = sel.shape[-1]
    P = wst3.shape[0] // (win_off[1])
    rows = NSA_REP * tp

    def cache_spec(g, t):
        def imap(b, k, pt, sl):
            s = jnp.minimum(sl[b, g, t, k], n_past_blk - 1)
            return ((page_off + pt[b, s // bpp]) * bpp + s % bpp, 1, 0, 0)
        return pl.BlockSpec((SLC_BLOCK, None, 2 * NSA_KV, HEAD_DIM), imap)

    cache_specs = [cache_spec(g, t) for g in range(NSA_KV) for t in range(tv)]
    kern = functools.partial(_nsa_dec_attn_kernel, tp=tp, tv=tv, n_past_blk=n_past_blk, q_start=q_start, ktop=ktop)
    whole = lambda w: pl.BlockSpec((1, tp, w), lambda b, k, pt, sl: (b, 0, 0))
    return pl.pallas_call(
        kern,
        out_shape=jax.ShapeDtypeStruct((B, tp, NSA_HEADS * HEAD_DIM), BF16),
        grid_spec=pltpu.PrefetchScalarGridSpec(
            num_scalar_prefetch=2,
            grid=(B, ktop),
            in_specs=[whole(NSA_HEADS * HEAD_DIM),
                      pl.BlockSpec((tp, HEAD_DIM), lambda b, k, pt, sl: (0, 0)),
                      pl.BlockSpec((tp, HEAD_DIM), lambda b, k, pt, sl: (0, 0))]
            + cache_specs
            + [whole(4 * NSA_KV * HEAD_DIM),
               pl.BlockSpec((P, 2 * NSA_KV, HEAD_DIM), lambda b, k, pt, sl: (win_off[0] + b, 0, 0)),
               whole(2 * NSA_KV * HEAD_DIM),
               pl.BlockSpec((1, NSA_KV, rows, HEAD_DIM), lambda b, k, pt, sl: (b, 0, 0, 0)),
               pl.BlockSpec((NSA_KV, tp, 128), lambda b, k, pt, sl: (0, b, 0))],
            out_specs=whole(NSA_HEADS * HEAD_DIM),
            scratch_shapes=[pltpu.VMEM((NSA_KV, rows, HEAD_DIM), F32), pltpu.VMEM((NSA_KV, rows, 1), F32),
                            pltpu.VMEM((NSA_KV, rows, 1), F32), pltpu.VMEM((NSA_KV, rows, HEAD_DIM), F32)],
        ),
        compiler_params=_cp("parallel", "arbitrary"),
        name="nsa_dec_attn",
    )(page_table, sel, zq3, cos, sin, *([cache4] * len(cache_specs)), new_kv3, wst3, win3, ocmp, gates4)


def _gmlp_kernel(z_ref, g_ref, b_ref, ws_ref, bst_ref, o_ref, v_ref, *, lc):
    z = _gelu(z_ref[...])
    u, v = z[:, :W_B], z[:, W_B:]
    mu = jnp.mean(v, -1, keepdims=True)
    d = v - mu
    var = jnp.mean(d * d, -1, keepdims=True)
    vn = d * lax.rsqrt(var + 1e-5) * g_ref[...] + b_ref[...]
    v_ref[...] = vn
    vb = vn.astype(BF16)
    ti = lax.broadcasted_iota(I32, (lc, lc), 0)
    si = lax.broadcasted_iota(I32, (lc, lc), 1)
    for g in range(GMLP_GROUPS):
        cs = slice(g * GMLP_GDIM, (g + 1) * GMLP_GDIM)
        w = jnp.where(si <= ti, ws_ref[g], 0.0).astype(BF16)
        mixed = _dot(w, vb[:, cs]) + bst_ref[:, g:g + 1]
        o_ref[:, cs] = (u[:, cs] * mixed).astype(o_ref.dtype)


def _gmlp(zuv, ln_g, ln_b, ws, bs, lc):
    M = zuv.shape[0]
    ws = ws[:, :lc, :lc]
    bst = bs[:, :lc].T
    return pl.pallas_call(
        functools.partial(_gmlp_kernel, lc=lc),
        out_shape=(jax.ShapeDtypeStruct((M, W_B), BF16), jax.ShapeDtypeStruct((M, W_B), F32)),
        grid=(M // lc,),
        in_specs=[pl.BlockSpec((lc, 2 * W_B), lambda i: (i, 0)),
                  pl.BlockSpec((1, W_B), lambda i: (0, 0)),
                  pl.BlockSpec((1, W_B), lambda i: (0, 0)),
                  pl.BlockSpec((GMLP_GROUPS, lc, lc), lambda i: (0, 0, 0)),
                  pl.BlockSpec((lc, GMLP_GROUPS), lambda i: (0, 0))],
        out_specs=(pl.BlockSpec((lc, W_B), lambda i: (i, 0)), pl.BlockSpec((lc, W_B), lambda i: (i, 0))),
        compiler_params=_cp("parallel"),
        name="gmlp",
    )(zuv, ln_g.reshape(1, W_B), ln_b.reshape(1, W_B), ws, bst)


def _mlstm_kernel(bi_ref, bf_ref, q_ref, k_ref, v_ref, og_ref, sm_ref, smt_ref, ng_ref, c0_ref, n0_ref, m0_ref,
                  h_ref, c_ref, n_ref, m_ref, c_s, n_s, m_s, *, lc, valid, ci_col, cf_col):
    c = pl.program_id(1)

    @pl.when(c == 0)
    def _():
        c_s[...] = c0_ref[0]
        n_s[...] = n0_ref[0]
        m_s[...] = m0_ref[0]

    ti = lax.broadcasted_iota(I32, (lc, lc), 0)
    si = lax.broadcasted_iota(I32, (lc, lc), 1)
    tril = (si <= ti) & (si < valid)
    rvalid = lax.broadcasted_iota(I32, (lc, 1), 0) < valid
    last = valid - 1
    kscale = MLSTM_DQK ** -0.5
    for h in range(MLSTM_HEADS):
        ig_c = sm_ref[:, ci_col + h:ci_col + h + 1] + bi_ref[h]
        lf_c = _log_sigmoid(sm_ref[:, cf_col + h:cf_col + h + 1] + bf_ref[h])
        ig_r = smt_ref[0, h:h + 1, :] + bi_ref[h]
        lf_r = _log_sigmoid(smt_ref[0, MLSTM_HEADS + h:MLSTM_HEADS + h + 1, :] + bf_ref[h])
        F_c = jnp.sum(jnp.where(si <= ti, lf_r, 0.0), axis=1, keepdims=True)
        F_r = jnp.sum(jnp.where(ti <= si, lf_c, 0.0), axis=0, keepdims=True)
        D = jnp.where(tril, F_c - F_r + ig_r, NEG_INF)
        m_prev = m_s[h:h + 1, 0:1]
        m_inter = F_c + m_prev
        m_t = jnp.maximum(m_inter, jnp.max(D, axis=1, keepdims=True))
        qh = q_ref[:, h * MLSTM_DQK:(h + 1) * MLSTM_DQK]
        ks = k_ref[:, h * MLSTM_DQK:(h + 1) * MLSTM_DQK] * kscale
        vh = v_ref[:, h * MLSTM_DV:(h + 1) * MLSTM_DV]
        qb, kb, vb = qh.astype(BF16), ks.astype(BF16), vh.astype(BF16)
        S = _dot_nt(qb, kb) * jnp.exp(D - m_t)
        dec = jnp.exp(m_inter - m_t)
        C = c_s[h]
        n_row = n_s[h:h + 1, :]
        num = _dot(S.astype(BF16), vb) + dec * _dot_nt(qb, C.astype(BF16))
        den = jnp.sum(S, axis=1, keepdims=True) + dec * jnp.sum(qh * n_row, axis=1, keepdims=True)
        hh = num / jnp.maximum(jnp.abs(den), 1.0)
        m_new = m_t[last:last + 1, :]
        F_last = F_c[last:last + 1, :]
        wl_c = jnp.where(rvalid, jnp.exp(F_last - F_c + ig_c - m_new), 0.0)
        dl = jnp.exp(F_last + m_prev - m_new)
        c_s[h] = dl * C + _dot_tn((vh * wl_c).astype(BF16), kb)
        n_s[h:h + 1, :] = dl * n_row + jnp.sum(wl_c * ks, axis=0, keepdims=True)
        m_s[h:h + 1, :] = jnp.broadcast_to(m_new, (1, 128))
        mu = jnp.mean(hh, -1, keepdims=True)
        d = hh - mu
        var = jnp.mean(d * d, -1, keepdims=True)
        hn = d * lax.rsqrt(var + 1e-5) * ng_ref[h:h + 1, :]
        vs = slice(h * MLSTM_DV, (h + 1) * MLSTM_DV)
        h_ref[:, vs] = (_sigmoid(og_ref[:, vs]) * hn).astype(h_ref.dtype)

    @pl.when(c == pl.num_programs(1) - 1)
    def _():
        c_ref[0] = c_s[...]
        n_ref[0] = n_s[...]
        m_ref[0] = m_s[...]


def _mlstm(zcq, zck, zcv, zco, zsm, b_i, b_f, norm_g, C0, n0, m0, B, T, lc, valid, ci_col, cf_col):
    nc = T // lc
    H, DQ, DV = MLSTM_HEADS, MLSTM_DQK, MLSTM_DV
    smt = jnp.concatenate([zsm[:, ci_col:ci_col + H], zsm[:, cf_col:cf_col + H]], -1)
    smt = smt.reshape(B * nc, lc, 2 * H).transpose(0, 2, 1)
    m0b = jnp.broadcast_to(m0[..., None], (B, H, 128))
    row = lambda w: pl.BlockSpec((lc, w), lambda b, c: (b * nc + c, 0))
    smem = pl.BlockSpec(memory_space=pltpu.SMEM)
    kern = functools.partial(_mlstm_kernel, lc=lc, valid=valid, ci_col=ci_col, cf_col=cf_col)
    c_spec = pl.BlockSpec((1, H, DV, DQ), lambda b, c: (b, 0, 0, 0))
    n_spec = pl.BlockSpec((1, H, DQ), lambda b, c: (b, 0, 0))
    h, C, n, m = pl.pallas_call(
        kern,
        out_shape=(jax.ShapeDtypeStruct((B * T, H * DV), BF16), jax.ShapeDtypeStruct((B, H, DV, DQ), F32),
                   jax.ShapeDtypeStruct((B, H, DQ), F32), jax.ShapeDtypeStruct((B, H, 128), F32)),
        grid=(B, nc),
        in_specs=[smem, smem, row(H * DQ), row(H * DQ), row(H * DV), row(H * DV), row(128),
                  pl.BlockSpec((1, 2 * H, lc), lambda b, c: (b * nc + c, 0, 0)),
                  pl.BlockSpec((H, DV), lambda b, c: (0, 0)), c_spec, n_spec, n_spec],
        out_specs=(row(H * DV), c_spec, n_spec, n_spec),
        scratch_shapes=[pltpu.VMEM((H, DV, DQ), F32), pltpu.VMEM((H, DQ), F32), pltpu.VMEM((H, 128), F32)],
        compiler_params=_cp("parallel", "arbitrary"),
        name="mlstm",
    )(b_i, b_f, zcq, zck, zcv, zco, zsm, smt, norm_g, C0, n0, m0b)
    return h, C, n, m[..., 0]


def _merge_kernel(a_ref, b_ref, c_ref, wa_ref, wb_ref, wc_ref, ga_ref, gb_ref, gc_ref, o_ref):
    o = (_sigmoid(ga_ref[...]) * _dot(a_ref[...], wa_ref[...])
         + _sigmoid(gb_ref[...]) * _dot(b_ref[...], wb_ref[...])
         + _sigmoid(gc_ref[...]) * _dot(c_ref[...], wc_ref[...]))
    o_ref[...] = o.astype(o_ref.dtype)


def _merge(oa, ob, oc, wa, wb, wc, layer, zmg, tm=512, tn=512):
    M, K = oa.shape
    D = wa.shape[-1]
    tm, tn = min(tm, M), min(tn, D)
    nj = D // tn
    x_spec = pl.BlockSpec((tm, K), lambda i, j: (i, 0))
    w_spec = _w_spec(wa, layer, (K, tn), lambda i, j: (0, j))
    g_spec = lambda o: pl.BlockSpec((tm, tn), lambda i, j: (i, o * nj + j))
    return pl.pallas_call(
        _merge_kernel,
        out_shape=jax.ShapeDtypeStruct((M, D), BF16),
        grid=(M // tm, nj),
        in_specs=[x_spec, x_spec, x_spec, w_spec, w_spec, w_spec, g_spec(0), g_spec(1), g_spec(2)],
        out_specs=pl.BlockSpec((tm, tn), lambda i, j: (i, j)),
        compiler_params=_cp("parallel", "parallel"),
        name="merge",
    )(oa, ob, oc, wa, wb, wc, zmg, zmg, zmg)


def _top16(s, ids, big, track_pos):
    vals, idxs = [], []
    pos = jnp.full(s.shape, -1, I32) if track_pos else None
    for it in range(PEER_TOPK):
        mx = jnp.max(s, axis=0, keepdims=True)
        idx = jnp.min(jnp.where(s == mx, ids, big), axis=0, keepdims=True)
        one = ids == idx
        s = jnp.where(one, NEG_INF, s)
        if track_pos:
            pos = jnp.where(one, it, pos)
        vals.append(mx)
        idxs.append(idx)
    return vals, idxs, pos


def _peer_route_kernel(q_ref, k1_ref, k2_ref, s1_ref, s2_ref, pk_ref):
    tm = q_ref.shape[0]
    half = PEER_DQ // 2
    K = PEER_TOPK
    k1 = k1_ref[...].astype(BF16)
    k2 = k2_ref[...].astype(BF16)
    kid = lax.broadcasted_iota(I32, (N_KEYS, 1), 0)
    i16 = lax.broadcasted_iota(I32, (K, 1), 0)
    i8 = lax.broadcasted_iota(I32, (8, 1), 0)
    cid = jnp.concatenate([i16, K + i8, 2 * K + i8, 3 * K + i8, i8 * K, i8 * K + 1, i8 * K + 2, (8 + i8) * K], axis=0)
    for h in range(PEER_HEADS):
        q1 = q_ref[:, h * PEER_DQ:h * PEER_DQ + half].astype(BF16)
        q2 = q_ref[:, h * PEER_DQ + half:(h + 1) * PEER_DQ].astype(BF16)
        s1 = _dot_nt(k1, q1)
        s2 = _dot_nt(k2, q2)
        v1, _, pos1 = _top16(s1, kid, N_KEYS, True)
        v2, _, pos2 = _top16(s2, kid, N_KEYS, True)
        v1m = jnp.concatenate(v1, axis=0)
        v2m = jnp.concatenate(v2, axis=0)
        pieces = [v1[0] + v2m, v1[1] + v2m[0:8], v1[2] + v2m[0:8], v1[3] + v2m[0:8]]
        for kk in range(3):
            pieces.append(jnp.where(i8 >= 4, v1m[0:8] + v2[kk], NEG_INF))
        pieces.append(v1m[8:16] + v2[0])
        cv, cj, _ = _top16(jnp.concatenate(pieces, axis=0), cid, K * K, False)
        z = jnp.zeros((1, tm), F32)
        wbits = jnp.zeros((K, tm), I32)
        for it in range(K):
            z = z + jnp.exp(cv[it] - cv[0])
            ii = jnp.right_shift(cj[it], 4)
            kk = cj[it] & (K - 1)
            wbits = jnp.where(i16 == ii, wbits | jnp.left_shift(1, kk), wbits)
        in1, in2 = pos1 >= 0, pos2 >= 0
        s1_ref[h] = jnp.where(in1, (s1 - cv[0]) * LOG2E - jnp.log2(z), NEG_INF)
        s2_ref[h] = jnp.where(in2, s2 * LOG2E, NEG_INF)
        rw = jnp.zeros((N_KEYS, tm), I32)
        for i in range(K):
            rw = jnp.where(pos1 == i, wbits[i:i + 1, :], rw)
        b2 = jnp.where(in2, jnp.left_shift(1, jnp.maximum(pos2, 0)), 0)
        pk_ref[h] = b2 | jnp.left_shift(rw, 16)


def _peer_route(q, k1, k2, tm=256):
    M = q.shape[0]
    tm = min(tm, M)
    spec = pl.BlockSpec((PEER_HEADS, N_KEYS, tm), lambda i: (0, 0, i))
    shp = lambda dt: jax.ShapeDtypeStruct((PEER_HEADS, N_KEYS, M), dt)
    return pl.pallas_call(
        _peer_route_kernel,
        out_shape=(shp(F32), shp(F32), shp(I32)),
        grid=(M // tm,),
        in_specs=[pl.BlockSpec((tm, PEER_HEADS * PEER_DQ), lambda i: (i, 0)),
                  pl.BlockSpec((N_KEYS, PEER_DQ // 2), lambda i: (0, 0)),
                  pl.BlockSpec((N_KEYS, PEER_DQ // 2), lambda i: (0, 0))],
        out_specs=(spec, spec, spec),
        compiler_params=_cp("parallel"),
        name="peer_route",
    )(q, k1, k2)


PEER_SUB = 256


def _peer_dense_kernel(x_ref, u_ref, v_ref, s1_ref, s2_ref, pk_ref, o_ref, *, te):
    e = pl.program_id(1)

    @pl.when(e == 0)
    def _():
        o_ref[...] = jnp.zeros_like(o_ref)

    x = x_ref[...]
    hs = [_dot_nt(x, u_ref[c * PEER_SUB:(c + 1) * PEER_SUB, :]) for c in range(te // PEER_SUB)]
    per = te // N_KEYS
    gates = []
    for al in range(per):
        a = e * per + al
        gate = None
        for hd in range(PEER_HEADS):
            wt = jnp.exp2(s1_ref[hd, pl.ds(a, 1), :] + s2_ref[hd])
            arow = pk_ref[hd, pl.ds(a, 1), :]
            abits = lax.shift_right_logical(arow, jnp.full_like(arow, 16))
            g = jnp.where((abits & pk_ref[hd]) != 0, wt, 0.0)
            gate = g if gate is None else gate + g
        gates.append(gate.T)
    gps = PEER_SUB // N_KEYS
    acts = [(_gelu(hs[c]) * jnp.concatenate(gates[c * gps:(c + 1) * gps], axis=1)).astype(BF16)
            for c in range(te // PEER_SUB)]
    o_ref[...] += _dot(jnp.concatenate(acts, axis=1), v_ref[...])


def _peer_dense(xb, u, v, layer, s1l, s2l, pk, tm=512, te=512):
    M, D = xb.shape
    NE = u.shape[-2]
    tm = min(tm, M)
    n_e = NE // te
    once = pl.Buffered(buffer_count=1)
    rspec = pl.BlockSpec((PEER_HEADS, N_KEYS, tm), lambda i, e: (0, 0, i), pipeline_mode=once)
    return pl.pallas_call(
        functools.partial(_peer_dense_kernel, te=te),
        out_shape=jax.ShapeDtypeStruct((M, D), F32),
        grid=(M // tm, n_e),
        in_specs=[pl.BlockSpec((tm, D), lambda i, e: (i, 0), pipeline_mode=once),
                  _w_spec(u, layer, (te, D), lambda i, e: (e, 0)),
                  _w_spec(v, layer, (te, D), lambda i, e: (e, 0)),
                  rspec, rspec, rspec],
        out_specs=pl.BlockSpec((tm, D), lambda i, e: (i, 0)),
        compiler_params=_cp("parallel", "arbitrary"),
        name="peer_dense",
    )(xb, u, v, s1l, s2l, pk)


def _peer(x1, x1b, lw, alpha):
    M, D = x1.shape
    mp = -(-M // 128) * 128
    xb = x1b if mp == M else jnp.pad(x1b, ((0, mp - M), (0, 0)))
    l = lw["layer"]
    q = _mm(xb, lw["peer_wq"], l)
    s1l, s2l, pk = _peer_route(q, lw["peer_k1"], lw["peer_k2"])
    y = _peer_dense(xb, lw["peer_u"], lw["peer_v"], l, s1l, s2l, pk)[:M]
    return _res_ln(x1, y, lw["ln2_g"], lw["ln2_b"], alpha)


Z_SEGS = ("a_q", "a_kv", "a_g", "b_uv", "c_q", "c_k", "c_v", "c_i", "c_f", "c_o", "mg")
STACKED_BF16 = ("w_br_a", "w_br_b", "w_br_c", "w_out", "peer_wq", "peer_u", "peer_v")
GATE_COL = 0
CI_COL = 3 * NSA_HEADS
CF_COL = CI_COL + MLSTM_HEADS


def _prep_layer_weights(l, w):
    D = w["w_in"].shape[1]
    sizes = (NSA_HEADS * HEAD_DIM, 6 * NSA_KV * HEAD_DIM, 3 * NSA_HEADS, 2 * W_B,
             MLSTM_HEADS * MLSTM_DQK, MLSTM_HEADS * MLSTM_DQK, MLSTM_HEADS * MLSTM_DV, MLSTM_HEADS, MLSTM_HEADS,
             MLSTM_HEADS * MLSTM_DV, 3 * D)
    offs = np.concatenate([[0], np.cumsum(sizes)])
    w_in = w["w_in"][l]
    seg = {n: (int(offs[i]), int(offs[i + 1])) for i, n in enumerate(Z_SEGS)}
    cut = lambda n: w_in[:, seg[n][0]:seg[n][1]].astype(BF16)
    lw = {n: cut(n) for n in ("a_q", "a_kv", "b_uv", "c_q", "c_k", "c_v", "c_o", "mg")}
    small = jnp.concatenate([w_in[:, seg[n][0]:seg[n][1]] for n in ("a_g", "c_i", "c_f")], -1)
    lw["small"] = jnp.pad(small, ((0, 0), (0, 128 - small.shape[1]))).astype(BF16)
    lw["layer"] = l
    for n in STACKED_BF16:
        lw[n] = w[n + "_bf16"]
    for n in ("cmp_pe", "cmp_w1", "cmp_b1", "cmp_w2", "gmlp_ln_g", "gmlp_ln_b", "gmlp_ws", "gmlp_bs",
              "mlstm_b_i", "mlstm_b_f", "mlstm_norm_g", "ln1_g", "ln1_b", "peer_k1", "peer_k2", "ln2_g", "ln2_b"):
        lw[n] = w[n][l]
    return lw


def _gates_by_group(zsm):
    M = zsm.shape[0]
    gt = zsm[:, GATE_COL:GATE_COL + 3 * NSA_HEADS].reshape(M, NSA_KV, 3 * NSA_REP).transpose(1, 0, 2)
    return jnp.pad(gt, ((0, 0), (0, 0), (0, 128 - 3 * NSA_REP)))


def _layer(x, xb, B, T, tv, lw, alpha, dec):
    M, D = x.shape
    z = {n: _mm(xb, lw[n]) for n in ("a_q", "a_kv", "b_uv", "c_q", "c_k", "c_v", "c_o", "mg", "small")}
    zsm = z["small"]
    q_start = 0 if dec is None else dec["q_start"]
    cos, sin = _rope_tables(q_start + jnp.arange(T))
    new_kv, win = _kvprep(z["a_kv"], cos, sin, T)
    gates4 = _gates_by_group(zsm)
    cmp_w = (lw["cmp_pe"], lw["cmp_w1"], lw["cmp_b1"], lw["cmp_w2"])
    gw = NSA_KV * HEAD_DIM
    if dec is None:
        n_pages = T // PAGE_SIZE
        ident = jnp.arange(B * n_pages, dtype=I32).reshape(B, n_pages)
        src3 = z["a_kv"][:, :2 * gw].reshape(M, 2 * NSA_KV, HEAD_DIM)
        kvc = _compress(src3, ident, 0, *cmp_w)
        o_a = _nsa_prefill(z["a_q"], new_kv, win, kvc, gates4, cos, sin, B, T)
        lc_g, lc_m = CHUNK, math.gcd(T, MLSTM_KERNEL_CHUNK)
        C0 = jnp.zeros((B, MLSTM_HEADS, MLSTM_DV, MLSTM_DQK), F32)
        n0 = jnp.zeros((B, MLSTM_HEADS, MLSTM_DQK), F32)
        m0 = jnp.zeros((B, MLSTM_HEADS), F32)
        valid = lc_m
    else:
        page_table = dec["page_table"]
        kvc = _compress(dec["cache3"], page_table, dec["page_off"], *cmp_w)
        L = q_start + tv
        n_cmp = (L - CMP_BLOCK) // CMP_STRIDE + 1
        n_slc = -(-L // SLC_BLOCK)
        assert n_cmp <= q_start // CMP_STRIDE - 1 and n_slc == q_start // SLC_BLOCK + 1
        assert q_start % SLC_BLOCK == 0 and tv <= T
        zq3 = z["a_q"].reshape(B, T, -1)
        ocmp, sel = _nsa_dec_sel(zq3, kvc, n_cmp, n_slc, q_start)
        sel = sel[:, :, :tv, :min(TOP_BLOCKS, n_slc)]
        o_a = _nsa_dec_attn(page_table, sel, zq3, cos, sin, dec["cache4"], dec["page_off"], new_kv.reshape(B, T, -1),
                            dec["wst3"], dec["win_off"], win.reshape(B, T, -1), ocmp, gates4, tv, q_start)
        o_a = o_a.reshape(M, -1)
        lc_g = lc_m = T
        C0, n0, m0 = dec["C"], dec["n"], dec["m"]
        valid = tv
    o_b, v_rows = _gmlp(z["b_uv"], lw["gmlp_ln_g"], lw["gmlp_ln_b"], lw["gmlp_ws"], lw["gmlp_bs"], lc_g)
    o_c, C, n, m = _mlstm(z["c_q"], z["c_k"], z["c_v"], z["c_o"], zsm, lw["mlstm_b_i"], lw["mlstm_b_f"],
                          lw["mlstm_norm_g"], C0, n0, m0, B, T, lc_m, valid, CI_COL, CF_COL)
    merged = _merge(o_a, o_b, o_c, lw["w_br_a"], lw["w_br_b"], lw["w_br_c"], lw["layer"], z["mg"])
    x1, x1b = _res_ln(x, _mm(merged, lw["w_out"], lw["layer"]), lw["ln1_g"], lw["ln1_b"], alpha)
    y, yb = _peer(x1, x1b, lw, alpha)
    return y, yb, new_kv, win, v_rows, C, n, m


def kernel(x_prompt, x_sample, cache_nsa_kv, state_nsa_win, state_mlstm_C, state_mlstm_n, state_mlstm_m,
           page_table, w_in, cmp_pe, cmp_w1, cmp_b1, cmp_w2, gmlp_ln_g, gmlp_ln_b, gmlp_ws, gmlp_bs,
           mlstm_b_i, mlstm_b_f, mlstm_norm_g, w_br_a, w_br_b, w_br_c, w_out, ln1_g, ln1_b,
           peer_wq, peer_k1, peer_k2, peer_u, peer_v, ln2_g, ln2_b):
    w = dict(w_in=w_in, cmp_pe=cmp_pe, cmp_w1=cmp_w1, cmp_b1=cmp_b1, cmp_w2=cmp_w2, gmlp_ln_g=gmlp_ln_g,
             gmlp_ln_b=gmlp_ln_b, gmlp_ws=gmlp_ws, gmlp_bs=gmlp_bs, mlstm_b_i=mlstm_b_i, mlstm_b_f=mlstm_b_f,
             mlstm_norm_g=mlstm_norm_g, w_br_a=w_br_a, w_br_b=w_br_b, w_br_c=w_br_c, w_out=w_out, ln1_g=ln1_g,
             ln1_b=ln1_b, peer_wq=peer_wq, peer_k1=peer_k1, peer_k2=peer_k2, peer_u=peer_u, peer_v=peer_v,
             ln2_g=ln2_g, ln2_b=ln2_b)
    for n in STACKED_BF16:
        w[n + "_bf16"] = w[n].astype(BF16)
    depth = w_in.shape[0]
    alpha = (2 * depth) ** 0.25
    Bp, Tp, D = x_prompt.shape
    Bs, Ts, _ = x_sample.shape
    Tpad = 8
    past_len = page_table.shape[1] * PAGE_SIZE
    n_pool = cache_nsa_kv.shape[1]
    P = state_nsa_win.shape[2]
    cache3 = cache_nsa_kv.reshape(depth * n_pool * PAGE_SIZE, 4 * NSA_KV, HEAD_DIM)
    cache4 = cache_nsa_kv.reshape(depth * n_pool * PAGE_SIZE, 2, 2 * NSA_KV, HEAD_DIM)
    wst3 = state_nsa_win.reshape(depth * Bs * P, 2 * NSA_KV, HEAD_DIM)
    yp = x_prompt.reshape(Bp * Tp, D)
    ys = jnp.pad(x_sample, ((0, 0), (0, Tpad - Ts), (0, 0))).reshape(Bs * Tpad, D)
    ypb, ysb = yp.astype(BF16), ys.astype(BF16)
    outs = [[] for _ in range(11)]
    wkeep_p = min(WINDOW, Tp)
    for l in range(depth):
        lw = _prep_layer_weights(l, w)
        yp, ypb, kv, win, _, C, n, m = _layer(yp, ypb, Bp, Tp, Tp, lw, alpha, None)
        outs[0].append(kv.reshape(Bp, Tp, 4, NSA_KV, HEAD_DIM))
        outs[2].append(win.reshape(Bp, Tp, 2, NSA_KV, HEAD_DIM)[:, Tp - wkeep_p:])
        outs[5].append(C); outs[6].append(n); outs[7].append(m)
        dec = dict(q_start=past_len, cache3=cache3, cache4=cache4, page_off=l * n_pool, page_table=page_table,
                   wst3=wst3, win_off=(l * Bs, depth * Bs), C=state_mlstm_C[l], n=state_mlstm_n[l], m=state_mlstm_m[l])
        ys, ysb, kv, win, v_rows, C, n, m = _layer(ys, ysb, Bs, Tpad, Ts, lw, alpha, dec)
        outs[1].append(kv.reshape(Bs, Tpad, 4, NSA_KV, HEAD_DIM)[:, :Ts])
        win_new = win.reshape(Bs, Tpad, 2, NSA_KV, HEAD_DIM)[:, :Ts]
        outs[3].append(jnp.concatenate([state_nsa_win[l], win_new], 1)[:, Ts:])
        outs[4].append(v_rows.reshape(Bs, Tpad, W_B)[:, :Ts])
        outs[8].append(C); outs[9].append(n); outs[10].append(m)
    y_prompt = yp.reshape(Bp, Tp, D)
    y_sample = ys.reshape(Bs, Tpad, D)[:, :Ts]
    st = [jnp.stack(o) for o in outs]
    return (y_prompt, y_sample, st[0], st[1], st[2], st[3], st[4], st[5], st[6], st[7], st[8], st[9], st[10])
```
